```python
import math
import jax, jax.numpy as jnp
from jax import lax
import numpy as np

D_MODEL = 1024
BATCH = 16
SEQ = 256
DEPTH = 2
DEC_BATCH = 4
DEC_SEQ = 2048
PAST_LEN = 512

GRID_W = 64
EPS = 1e-6
N_MOD = 6
CHUNK = 128
A_GROUPS = 4
A_GROUP_W = 128
A_WIDTH = A_GROUPS * A_GROUP_W
LRU_BLOCKS = 8
LRU_BLOCK_W = 64
LRU_WIDTH = LRU_BLOCKS * LRU_BLOCK_W
CONV_W = 4
LRU_C = 8.0
ATT_HEADS = 4
ATT_HD = 64
ATT_VD = 2 * ATT_HD
ATT_WIDTH = ATT_HEADS * ATT_VD
Q_BLOCK = 128
ROPE_BASE = 10000.0
ROPE_AXIS_DIM = ATT_HD // 2
ROPE_FREQS = ROPE_AXIS_DIM // 2
N_BRANCH = 3
IN_SPLITS = (A_WIDTH, 2 * A_WIDTH, 2 * A_WIDTH + LRU_WIDTH, 2 * A_WIDTH + 2 * LRU_WIDTH,
             2 * A_WIDTH + 2 * LRU_WIDTH + ATT_WIDTH, 2 * A_WIDTH + 2 * LRU_WIDTH + 2 * ATT_WIDTH,
             2 * A_WIDTH + 2 * LRU_WIDTH + 3 * ATT_WIDTH)
IN_WIDTH = IN_SPLITS[-1] + N_BRANCH * D_MODEL
PEER_HEADS = 8
N_KEYS = 128
N_EXPERTS = N_KEYS * N_KEYS
PEER_QDIM = 256
PEER_HALF = PEER_QDIM // 2
PEER_TOPK = 16
TOKEN_BLOCK = 128

kernel_name = 'hybrid_diffusion_step_gmlp_rglru_diffattn_peer'


def rmsnorm(x, gain):
    xf = x.astype(jnp.float32)
    y = xf * lax.rsqrt(jnp.mean(xf * xf, axis=-1, keepdims=True) + EPS)
    return (y * gain.astype(jnp.float32)).astype(x.dtype)


def axial_rope_tables(rows):
    row_ids = jnp.repeat(jnp.arange(rows), GRID_W).astype(jnp.float32)
    col_ids = jnp.tile(jnp.arange(GRID_W), rows).astype(jnp.float32)
    inv_freq = ROPE_BASE ** (-jnp.arange(ROPE_FREQS, dtype=jnp.float32) / ROPE_FREQS)
    ang_r = (row_ids[:, None] * inv_freq)[:, None, None, :]
    ang_c = (col_ids[:, None] * inv_freq)[:, None, None, :]
    return (jnp.cos(ang_r), jnp.sin(ang_r), jnp.cos(ang_c), jnp.sin(ang_c))


def rotate(x, cos, sin):
    x1, x2 = x[..., :ROPE_FREQS], x[..., ROPE_FREQS:]
    return jnp.concatenate([x1 * cos - x2 * sin, x1 * sin + x2 * cos], axis=-1)


def apply_axial_rope(x, tabs):
    cos_r, sin_r, cos_c, sin_c = tabs
    xf = x.astype(jnp.float32)
    out = jnp.concatenate([rotate(xf[..., :ROPE_AXIS_DIM], cos_r, sin_r),
                           rotate(xf[..., ROPE_AXIS_DIM:], cos_c, sin_c)], axis=-1)
    return out.astype(x.dtype)


def chunk_mlp(u, v, g_v, w_s, b_s):
    b, s, _ = u.shape
    vn = rmsnorm(v, g_v).reshape(b, s // CHUNK, CHUNK, A_GROUPS, A_GROUP_W)
    mixed = jnp.einsum('gpq,bnqgc->bnpgc', w_s, vn) + b_s.T[None, None, :, :, None]
    return u * mixed.reshape(b, s, A_WIDTH)


def centred_conv(x, w, bias):
    s = x.shape[1]
    left = CONV_W // 2
    xp = jnp.pad(x, ((0, 0), (left, CONV_W - 1 - left), (0, 0)))
    return sum(xp[:, j:j + s] * w[j] for j in range(CONV_W)) + bias


def blockdiag(x, w):
    xb = x.reshape(x.shape[:-1] + (LRU_BLOCKS, LRU_BLOCK_W))
    return jnp.einsum('bshi,hij->bshj', xb, w).reshape(x.shape)


def linear_scan(a, bx, h0):
    def combine(left, right):
        a_l, b_l = left
        a_r, b_r = right
        return a_l * a_r, a_r * b_l + b_r
    a_cum, b_cum = lax.associative_scan(combine, (a, bx), axis=1)
    return a_cum * h0[:, None, :] + b_cum


def lru_direction(xc, p, d, h0):
    xf = xc.astype(jnp.float32)
    r = jax.nn.sigmoid(blockdiag(xf, p['lru_w_r'][d]) + p['lru_b_r'][d])
    i = jax.nn.sigmoid(blockdiag(xf, p['lru_w_i'][d]) + p['lru_b_i'][d])
    log_a = -LRU_C * r * jax.nn.softplus(-p['lru_lam'][d].astype(jnp.float32))
    bx = jnp.sqrt(-jnp.expm1(2.0 * log_a)) * (i * xf)
    hs = linear_scan(jnp.exp(log_a), bx, h0.astype(jnp.float32))
    return hs, hs[:, -1]


def diff_attention(q, k, v, lam):
    b, sq, h = q.shape[0], q.shape[1], q.shape[2]
    nb = sq // Q_BLOCK
    qb = q.reshape(b, nb, Q_BLOCK, h, 2, ATT_HD).transpose(1, 0, 2, 3, 4, 5)
    scale = ATT_HD ** -0.5

    def one_block(qblk):
        s = jnp.einsum('bqhtd,bkhtd->bhtqk', qblk, k).astype(jnp.float32) * scale
        prob = jax.nn.softmax(s, axis=-1)
        w = prob[:, :, 0] - lam * prob[:, :, 1]
        return jnp.einsum('bhqk,bkhd->bqhd', w.astype(v.dtype), v)

    out = lax.map(one_block, qb)
    return out.transpose(1, 0, 2, 3, 4).reshape(b, sq, h, ATT_VD)


def peer(h, wq, sub_keys, u_tab, v_tab):
    b, s, d = h.shape
    blocks = h.reshape(-1, TOKEN_BLOCK, d)

    def one_block(xb):
        q = (xb @ wq).reshape(TOKEN_BLOCK, PEER_HEADS, 2, PEER_HALF)
        s1 = jnp.einsum('thd,kd->thk', q[:, :, 0], sub_keys[0]).astype(jnp.float32)
        s2 = jnp.einsum('thd,kd->thk', q[:, :, 1], sub_keys[1]).astype(jnp.float32)
        v1, i1 = lax.top_k(s1, PEER_TOPK)
        v2, i2 = lax.top_k(s2, PEER_TOPK)
        n_cand = PEER_TOPK * PEER_TOPK
        cand_s = (v1[..., :, None] + v2[..., None, :]).reshape(TOKEN_BLOCK, PEER_HEADS, n_cand)
        cand_i = (i1[..., :, None] * N_KEYS + i2[..., None, :]).reshape(TOKEN_BLOCK, PEER_HEADS, n_cand)
        top_s, pos = lax.top_k(cand_s, PEER_TOPK)
        idx = jnp.take_along_axis(cand_i, pos, axis=-1)
        gate = jax.nn.softmax(top_s, axis=-1)
        u_e = jnp.take(u_tab, idx, axis=0)
        act = jax.nn.gelu(jnp.einsum('thkd,td->thk', u_e, xb).astype(jnp.float32))
        v_e = jnp.take(v_tab, idx, axis=0)
        return jnp.einsum('thk,thkd->td', (gate * act).astype(v_tab.dtype), v_e)

    out = lax.map(one_block, blocks)
    return out.reshape(b, s, d).astype(h.dtype)


def trunk_layer(x, cond, p, lam_init, rope, lru_h0, ctx_kv):
    b, s, _ = x.shape
    mod = (jax.nn.silu(cond) @ p['w_mod'] + p['b_mod'])[:, None, :]
    sh1, sc1, g1, sh2, sc2, g2 = jnp.split(mod, N_MOD, axis=-1)
    h = rmsnorm(x, p['norm1']) * (1.0 + sc1) + sh1
    a_u, a_v, r_x, r_g, q, k, v, br = jnp.split(h @ p['w_in'], IN_SPLITS, axis=-1)
    y_a = chunk_mlp(a_u, a_v, p['a_norm'], p['a_ws'], p['a_bs'])
    xc = centred_conv(r_x, p['lru_conv_w'], p['lru_conv_b'])
    if lru_h0 is None:
        zero = jnp.zeros((b, LRU_WIDTH), jnp.float32)
        lru_h0 = (zero, zero)
    hs_f, h_f = lru_direction(xc, p, 0, lru_h0[0])
    hs_b, h_b = lru_direction(jnp.flip(xc, axis=1), p, 1, lru_h0[1])
    y_b = (hs_f + jnp.flip(hs_b, axis=1)).astype(x.dtype) * jax.nn.gelu(r_g)
    q = q.reshape(b, s, ATT_HEADS, 2, ATT_HD)
    k = k.reshape(b, s, ATT_HEADS, 2, ATT_HD)
    v = v.reshape(b, s, ATT_HEADS, ATT_VD)
    ctx_state = (k.reshape(b, s, ATT_HEADS, ATT_VD), v, jnp.stack([h_f, h_b], axis=1))
    if rope is not None:
        q = apply_axial_rope(q, rope)
        k = apply_axial_rope(k, rope)
    if ctx_kv is not None:
        k_ctx, v_ctx = ctx_kv
        k = jnp.concatenate([k, k_ctx.reshape(b, -1, ATT_HEADS, 2, ATT_HD).astype(k.dtype)], axis=1)
        v = jnp.concatenate([v, v_ctx.astype(v.dtype)], axis=1)
    lp = p['att_lam']
    lam = jnp.exp(jnp.sum(lp[0] * lp[1])) - jnp.exp(jnp.sum(lp[2] * lp[3])) + lam_init
    o = diff_attention(q, k, v, lam)
    y_c = (rmsnorm(o, p['att_subln']) * (1.0 - lam_init)).reshape(b, s, ATT_WIDTH)
    g_a, g_b, g_c = jnp.split(jax.nn.sigmoid(br), N_BRANCH, axis=-1)
    merged = g_a * (y_a @ p['w_up_a']) + g_b * (y_b @ p['w_up_b']) + g_c * (y_c @ p['w_up_c'])
    x = x + g1 * (merged @ p['w_out'])
    h2 = rmsnorm(x, p['norm2']) * (1.0 + sc2) + sh2
    x = x + g2 * peer(h2, p['peer_wq'], p['peer_keys'], p['peer_u'], p['peer_v'])
    return x, ctx_state


def setup_inputs(seed: int = 0) -> dict:
    key = jax.random.key(seed)
    keys = jax.random.split(key, 40)
    f32 = jnp.float32

    def nrm(i, shape, scale):
        return jax.random.normal(keys[i], shape, f32) * scale

    a0 = jax.random.uniform(keys[13], (DEPTH, 2, LRU_WIDTH), f32, 0.9, 0.999)
    return {
        'x_prompt': nrm(0, (BATCH, SEQ, D_MODEL), 1.0),
        'x_sample': nrm(1, (DEC_BATCH, DEC_SEQ, D_MODEL), 1.0),
        'cache_k': nrm(2, (DEC_BATCH, DEPTH, PAST_LEN, ATT_HEADS, ATT_VD), 1.0),
        'cache_v': nrm(3, (DEC_BATCH, DEPTH, PAST_LEN, ATT_HEADS, ATT_VD), 1.0),
        'state_lru': nrm(4, (DEC_BATCH, DEPTH, 2, LRU_WIDTH), 0.5),
        'c': nrm(5, (DEC_BATCH, D_MODEL), 1.0),
        'c_ctx': nrm(6, (D_MODEL,), 1.0),
        'w_mod': nrm(7, (DEPTH, D_MODEL, N_MOD * D_MODEL), 0.5 * D_MODEL ** -0.5),
        'b_mod': nrm(8, (DEPTH, N_MOD * D_MODEL), 0.02),
        'norm1': 1.0 + nrm(9, (DEPTH, D_MODEL), 0.02),
        'norm2': 1.0 + nrm(10, (DEPTH, D_MODEL), 0.02),
        'w_in': nrm(11, (DEPTH, D_MODEL, IN_WIDTH), D_MODEL ** -0.5),
        'a_norm': 1.0 + nrm(12, (DEPTH, A_WIDTH), 0.02),
        'a_ws': nrm(14, (DEPTH, A_GROUPS, CHUNK, CHUNK), CHUNK ** -0.5),
        'a_bs': 1.0 + nrm(15, (DEPTH, A_GROUPS, CHUNK), 0.02),
        'lru_conv_w': nrm(16, (DEPTH, CONV_W, LRU_WIDTH), CONV_W ** -0.5),
        'lru_conv_b': nrm(17, (DEPTH, LRU_WIDTH), 0.02),
        'lru_w_r': nrm(18, (DEPTH, 2, LRU_BLOCKS, LRU_BLOCK_W, LRU_BLOCK_W), LRU_BLOCK_W ** -0.5),
        'lru_b_r': nrm(19, (DEPTH, 2, LRU_WIDTH), 0.1),
        'lru_w_i': nrm(20, (DEPTH, 2, LRU_BLOCKS, LRU_BLOCK_W, LRU_BLOCK_W), LRU_BLOCK_W ** -0.5),
        'lru_b_i': nrm(21, (DEPTH, 2, LRU_WIDTH), 0.1),
        'lru_lam': jnp.log(a0) - jnp.log1p(-a0),
        'att_lam': nrm(22, (DEPTH, 4, ATT_HD), 0.1),
        'att_subln': 1.0 + nrm(23, (DEPTH, ATT_VD), 0.02),
        'w_up_a': nrm(24, (DEPTH, A_WIDTH, D_MODEL), A_WIDTH ** -0.5),
        'w_up_b': nrm(25, (DEPTH, LRU_WIDTH, D_MODEL), LRU_WIDTH ** -0.5),
        'w_up_c': nrm(26, (DEPTH, ATT_WIDTH, D_MODEL), ATT_WIDTH ** -0.5),
        'w_out': nrm(27, (DEPTH, D_MODEL, D_MODEL), D_MODEL ** -0.5),
        'peer_wq': nrm(28, (DEPTH, D_MODEL, PEER_HEADS * PEER_QDIM), D_MODEL ** -0.5),
        'peer_keys': nrm(29, (DEPTH, 2, N_KEYS, PEER_HALF), PEER_HALF ** -0.5),
        'peer_u': nrm(30, (DEPTH, N_EXPERTS, D_MODEL), D_MODEL ** -0.5),
        'peer_v': nrm(31, (DEPTH, N_EXPERTS, D_MODEL), 0.25),
        'final_norm': 1.0 + nrm(32, (D_MODEL,), 0.02),
    }


def reference(x_prompt, x_sample, cache_k, cache_v, state_lru, c, c_ctx, w_mod, b_mod, norm1, norm2, w_in,
              a_norm, a_ws, a_bs, lru_conv_w, lru_conv_b, lru_w_r, lru_b_r, lru_w_i, lru_b_i, lru_lam,
              att_lam, att_subln, w_up_a, w_up_b, w_up_c, w_out, peer_wq, peer_keys, peer_u, peer_v, final_norm):
    rows = x_sample.shape[1] // GRID_W
    rope = axial_rope_tables(rows)
    cond_ctx = jnp.broadcast_to(c_ctx, (x_prompt.shape[0], D_MODEL))
    xp, xs = x_prompt, x_sample
    ks, vs, hs = [], [], []
    for i in range(DEPTH):
        lam_init = 0.8 - 0.6 * math.exp(-0.3 * i)
        p = {'w_mod': w_mod[i], 'b_mod': b_mod[i], 'norm1': norm1[i], 'norm2': norm2[i], 'w_in': w_in[i],
             'a_norm': a_norm[i], 'a_ws': a_ws[i], 'a_bs': a_bs[i],
             'lru_conv_w': lru_conv_w[i], 'lru_conv_b': lru_conv_b[i], 'lru_w_r': lru_w_r[i], 'lru_b_r': lru_b_r[i],
             'lru_w_i': lru_w_i[i], 'lru_b_i': lru_b_i[i], 'lru_lam': lru_lam[i],
             'att_lam': att_lam[i], 'att_subln': att_subln[i],
             'w_up_a': w_up_a[i], 'w_up_b': w_up_b[i], 'w_up_c': w_up_c[i], 'w_out': w_out[i],
             'peer_wq': peer_wq[i], 'peer_keys': peer_keys[i], 'peer_u': peer_u[i], 'peer_v': peer_v[i]}
        xp, (k_i, v_i, h_i) = trunk_layer(xp, cond_ctx, p, lam_init, None, None, None)
        ks.append(k_i)
        vs.append(v_i)
        hs.append(h_i)
        xs, _ = trunk_layer(xs, c, p, lam_init, rope, (state_lru[:, i, 0], state_lru[:, i, 1]),
                            (cache_k[:, i], cache_v[:, i]))
    y_prompt = rmsnorm(xp, final_norm)
    y_sample = rmsnorm(xs, final_norm)
    return (y_prompt, y_sample, jnp.stack(ks, axis=1), jnp.stack(vs, axis=1), jnp.stack(hs, axis=1))
```

```python
import functools
import math

import jax
import jax.numpy as jnp
from jax import lax
from jax.experimental import pallas as pl
from jax.experimental.pallas import tpu as pltpu

D_MODEL = 1024
BATCH = 16
SEQ = 256
DEPTH = 2
DEC_BATCH = 4
DEC_SEQ = 2048
PAST_LEN = 512
GRID_W = 64
EPS = 1e-6
N_MOD = 6
CHUNK = 128
A_GROUPS = 4
A_WIDTH = 512
LRU_BLOCKS = 8
LRU_BLOCK_W = 64
LRU_WIDTH = 512
CONV_W = 4
LRU_C = 8.0
ATT_HEADS = 4
ATT_HD = 64
ATT_VD = 128
ATT_WIDTH = 512
ROPE_BASE = 10000.0
ROPE_FREQS = 16
N_BRANCH = 3
IN_SPLITS = (512, 1024, 1536, 2048, 2560, 3072, 3584)
IN_WIDTH = 3584 + N_BRANCH * D_MODEL
PEER_HEADS = 8
N_KEYS = 128
N_EXPERTS = N_KEYS * N_KEYS
PEER_QDIM = 256
PEER_HALF = 128
PEER_TOPK = 16
N_PICK = PEER_HEADS * PEER_TOPK

N_CTX_TOK = BATCH * SEQ
N_LAT_TOK = DEC_BATCH * DEC_SEQ
N_TOK = N_CTX_TOK + N_LAT_TOK
TM = 256
N_TILES = N_TOK // TM
CTX_TILES = N_CTX_TOK // TM
LAT_TILES_PER_SEQ = DEC_SEQ // TM
N_COND = 8
TP = 32
ROW_WORDS = D_MODEL // 2
ROW_SUB = ROW_WORDS // 128
VMEM_LIMIT = 56 * 1024 * 1024

F32 = jnp.float32
BF16 = jnp.bfloat16
HI = lax.Precision.HIGHEST


def _cond_row(i):
    return jnp.maximum(i - LAT_TILES_PER_SEQ, 0) // LAT_TILES_PER_SEQ


def _pos_block(i):
    return jnp.where(i < CTX_TILES, 0, 1 + i % LAT_TILES_PER_SEQ)


def _mod_spec(chunk):
    return pl.BlockSpec((None, 1, D_MODEL), lambda i: (_cond_row(i), 0, chunk))


def _const_spec(shape):
    nd = len(shape)
    return pl.BlockSpec(shape, lambda *_: (0,) * nd)


def _rms(x, gain):
    return x * lax.rsqrt(jnp.mean(x * x, axis=-1, keepdims=True) + EPS) * gain


def _dot(a, b):
    return jnp.dot(a.astype(BF16), b.astype(BF16), preferred_element_type=F32)


def _dot_nt(a, b):
    return lax.dot_general(a.astype(BF16), b.astype(BF16), (((1,), (1,)), ((), ())), preferred_element_type=F32)


def _mod_kernel(cond_ref, w_ref, b_ref, o_ref):
    cond = cond_ref[...]
    act = cond * jax.nn.sigmoid(cond)
    o_ref[...] = jnp.dot(act, w_ref[...], preferred_element_type=F32, precision=HI) + b_ref[...]


def _modulation(cond, w_mod, b_mod):
    nc = 4
    cw = N_MOD * D_MODEL // nc
    return pl.pallas_call(
        _mod_kernel,
        grid=(DEPTH, nc),
        in_specs=[pl.BlockSpec((N_COND, D_MODEL), lambda l, j: (0, 0)),
                  pl.BlockSpec((None, D_MODEL, cw), lambda l, j: (l, 0, j)),
                  pl.BlockSpec((None, 1, cw), lambda l, j: (l, 0, j))],
        out_specs=pl.BlockSpec((None, N_COND, cw), lambda l, j: (l, 0, j)),
        out_shape=jax.ShapeDtypeStruct((DEPTH, N_COND, N_MOD * D_MODEL), F32),
        compiler_params=pltpu.CompilerParams(vmem_limit_bytes=VMEM_LIMIT),
        name="modulation",
    )(cond, w_mod, b_mod.reshape(DEPTH, 1, N_MOD * D_MODEL))


def _inproj_kernel(has_res, *refs):
    if has_res:
        x_ref, p_ref, g2_ref = refs[:3]
        refs = refs[3:]
        x = x_ref[...] + g2_ref[...] * p_ref[...]
    else:
        x_ref = refs[0]
        refs = refs[1:]
        x = x_ref[...]
    (sh_ref, sc_ref, n1_ref, w_ref, cos_ref, sin_ref,
     au_ref, av_ref, rx_ref, rg_ref, q_ref, k_ref, kr_ref, v_ref, br_ref) = refs[:15]
    if has_res:
        refs[15][...] = x
    h = (_rms(x, n1_ref[...]) * (1.0 + sc_ref[...]) + sh_ref[...]).astype(BF16)

    def proj(lo, hi):
        return jnp.dot(h, w_ref[:, lo:hi], preferred_element_type=F32)

    au_ref[...] = proj(0, IN_SPLITS[0])
    av_ref[...] = proj(IN_SPLITS[0], IN_SPLITS[1])
    rx_ref[...] = proj(IN_SPLITS[1], IN_SPLITS[2])
    rg_ref[...] = proj(IN_SPLITS[2], IN_SPLITS[3])
    q = proj(IN_SPLITS[3], IN_SPLITS[4])
    k = proj(IN_SPLITS[4], IN_SPLITS[5])
    v_ref[...] = proj(IN_SPLITS[5], IN_SPLITS[6])
    for j in range(N_BRANCH):
        lo = IN_SPLITS[6] + j * D_MODEL
        br_ref[:, j * D_MODEL:(j + 1) * D_MODEL] = proj(lo, lo + D_MODEL)
    k_ref[...] = k
    lane = lax.broadcasted_iota(jnp.int32, (TM, ATT_WIDTH), 1)
    first = (lane % (2 * ROPE_FREQS)) < ROPE_FREQS
    cos = cos_ref[...]
    sin = sin_ref[...]

    def rot(t):
        partner = jnp.where(first, pltpu.roll(t, ATT_WIDTH - ROPE_FREQS, 1), pltpu.roll(t, ROPE_FREQS, 1))
        return t * cos + partner * sin

    q_ref[...] = rot(q) * (ATT_HD ** -0.5)
    kr_ref[...] = rot(k)


def _inproj(x, res, mod_l, norm1, w_in, cos_t, sin_t):
    has_res = res is not None
    tok = lambda w: pl.BlockSpec((TM, w), lambda i: (i, 0))
    in_specs = [tok(D_MODEL)]
    args = [x]
    if has_res:
        p, mod_prev = res
        in_specs += [tok(D_MODEL), _mod_spec(5)]
        args += [p, mod_prev]
    in_specs += [_mod_spec(0), _mod_spec(1), _const_spec((1, D_MODEL)),
                 pl.BlockSpec((D_MODEL, IN_WIDTH), lambda i: (0, 0), pipeline_mode=pl.Buffered(1)),
                 pl.BlockSpec((TM, ATT_WIDTH), lambda i: (_pos_block(i), 0)),
                 pl.BlockSpec((TM, ATT_WIDTH), lambda i: (_pos_block(i), 0))]
    args += [mod_l, mod_l, norm1.reshape(1, D_MODEL), w_in, cos_t, sin_t]
    widths = [512] * 8 + [N_BRANCH * D_MODEL]
    out_specs = [tok(w) for w in widths]
    out_shape = [jax.ShapeDtypeStruct((N_TOK, w), F32) for w in widths]
    if has_res:
        out_specs.append(tok(D_MODEL))
        out_shape.append(jax.ShapeDtypeStruct((N_TOK, D_MODEL), F32))
    return pl.pallas_call(
        functools.partial(_inproj_kernel, has_res),
        grid=(N_TILES,),
        in_specs=in_specs,
        out_specs=out_specs,
        out_shape=out_shape,
        compiler_params=pltpu.CompilerParams(vmem_limit_bytes=VMEM_LIMIT),
        name="inproj",
    )(*args)


def _lru_kernel(seq, x_ref, cw_ref, cb_ref, wr_ref, br_ref, wi_ref, bi_ref, lam_ref, h0_ref, hs_ref, hl_ref):
    x = x_ref[...]
    t = lax.broadcasted_iota(jnp.int32, (seq, 128), 0)

    def shifted(v, k, fill):
        r = pltpu.roll(v, k % seq, 0)
        ok = (t >= k) if k > 0 else (t < seq + k)
        return jnp.where(ok, r, fill)

    cw = cw_ref[...]
    xc = (shifted(x, 2, 0.0) * cw[0:1] + shifted(x, 1, 0.0) * cw[1:2] + x * cw[2:3]
          + shifted(x, -1, 0.0) * cw[3:4] + cb_ref[...])

    def direction(d):
        r = jax.nn.sigmoid(jnp.dot(xc, wr_ref[d, 0], preferred_element_type=F32, precision=HI) + br_ref[d:d + 1])
        g = jax.nn.sigmoid(jnp.dot(xc, wi_ref[d, 0], preferred_element_type=F32, precision=HI) + bi_ref[d:d + 1])
        z = -lam_ref[d:d + 1]
        softplus = jnp.maximum(z, 0.0) + jnp.log(1.0 + jnp.exp(-jnp.abs(z)))
        log_a = -LRU_C * r * softplus
        a = jnp.exp(log_a)
        b = jnp.sqrt(1.0 - jnp.exp(2.0 * log_a)) * (g * xc)
        sgn = 1 if d == 0 else -1
        k = 1
        while k < seq:
            a_prev = shifted(a, sgn * k, 1.0)
            b_prev = shifted(b, sgn * k, 0.0)
            b = a * b_prev + b
            a = a * a_prev
            k *= 2
        return a * h0_ref[0, d:d + 1] + b

    hf = direction(0)
    hb = direction(1)
    hs_ref[...] = hf + hb
    hl_ref[0] = jnp.concatenate([hf[seq - 1:seq], hb[0:1]], axis=0)


def _lru(rx, row0, nseq, seq, conv_w, conv_b, w_r, b_r, w_i, b_i, lam, h0):
    nb = LRU_WIDTH // 128
    blk0 = row0 // seq
    par = lambda shape: pl.BlockSpec(shape, lambda b, c: (0,) * (len(shape) - 1) + (c,))
    return pl.pallas_call(
        functools.partial(_lru_kernel, seq),
        grid=(nseq, nb),
        in_specs=[pl.BlockSpec((seq, 128), lambda b, c: (blk0 + b, c)),
                  par((CONV_W, 128)), par((1, 128)),
                  pl.BlockSpec((2, 1, 128, 128), lambda b, c: (0, c, 0, 0)), par((2, 128)),
                  pl.BlockSpec((2, 1, 128, 128), lambda b, c: (0, c, 0, 0)), par((2, 128)),
                  par((2, 128)),
                  pl.BlockSpec((1, 2, 128), lambda b, c: (b, 0, c))],
        out_specs=[pl.BlockSpec((seq, 128), lambda b, c: (b, c)),
                   pl.BlockSpec((1, 2, 128), lambda b, c: (b, 0, c))],
        out_shape=[jax.ShapeDtypeStruct((nseq * seq, LRU_WIDTH), F32),
                   jax.ShapeDtypeStruct((nseq, 2, LRU_WIDTH), F32)],
        compiler_params=pltpu.CompilerParams(vmem_limit_bytes=VMEM_LIMIT),
        name="rglru",
    )(rx, conv_w, conv_b.reshape(1, LRU_WIDTH), w_r, b_r, w_i, b_i, lam, h0)


def _blockdiag128(w):
    w = w.reshape(2, LRU_BLOCKS // 2, 2, LRU_BLOCK_W, LRU_BLOCK_W)
    z = jnp.zeros_like(w[:, :, 0])
    top = jnp.concatenate([w[:, :, 0], z], axis=-1)
    bot = jnp.concatenate([z, w[:, :, 1]], axis=-1)
    return jnp.concatenate([top, bot], axis=-2)


def _attn_kernel(has_ctx, lam_init, *refs):
    if has_ctx:
        q_ref, k_ref, v_ref, kc_ref, vc_ref, lp_ref, g_ref, o_ref = refs
    else:
        q_ref, k_ref, v_ref, lp_ref, g_ref, o_ref = refs
    lp = lp_ref[...]
    lam = (jnp.exp(jnp.sum(lp[0:1] * lp[1:2], axis=-1, keepdims=True))
           - jnp.exp(jnp.sum(lp[2:3] * lp[3:4], axis=-1, keepdims=True)) + lam_init)
    q = q_ref[...]
    lane = lax.broadcasted_iota(jnp.int32, q.shape, 1)
    halves = (jnp.where(lane < ATT_HD, q, 0.0), jnp.where(lane >= ATT_HD, q, 0.0))
    k = k_ref[...]
    v = v_ref[...]
    w = []
    for qh in halves:
        s = _dot_nt(qh, k)
        m = jnp.max(s, axis=-1, keepdims=True)
        if has_ctx:
            sc = _dot_nt(qh, kc_ref[0])
            m = jnp.maximum(m, jnp.max(sc, axis=-1, keepdims=True))
            ec = jnp.exp(sc - m)
        e = jnp.exp(s - m)
        den = jnp.sum(e, axis=-1, keepdims=True)
        if has_ctx:
            den = den + jnp.sum(ec, axis=-1, keepdims=True)
            w.append((e / den, ec / den))
        else:
            w.append((e / den,))
    o = _dot(w[0][0] - lam * w[1][0], v)
    if has_ctx:
        o = o + _dot(w[0][1] - lam * w[1][1], vc_ref[0])
    o_ref[...] = _rms(o, g_ref[...]) * (1.0 - lam_init)


def _attention(q, kr, v, row0, nseq, seq, ctx, att_lam, subln, lam_init):
    has_ctx = ctx is not None
    nq = seq // TM
    blk0 = row0 // seq
    in_specs = [pl.BlockSpec((TM, ATT_VD), lambda b, h, i: ((row0 // TM) + b * nq + i, h)),
                pl.BlockSpec((seq, ATT_VD), lambda b, h, i: (blk0 + b, h)),
                pl.BlockSpec((seq, ATT_VD), lambda b, h, i: (blk0 + b, h))]
    args = [q, kr, v]
    if has_ctx:
        in_specs += [pl.BlockSpec((1, PAST_LEN, ATT_VD), lambda b, h, i: (b, 0, h))] * 2
        args += list(ctx)
    in_specs += [_const_spec((4, ATT_HD)), _const_spec((1, ATT_VD))]
    args += [att_lam, subln.reshape(1, ATT_VD)]
    return pl.pallas_call(
        functools.partial(_attn_kernel, has_ctx, lam_init),
        grid=(nseq, ATT_HEADS, nq),
        in_specs=in_specs,
        out_specs=pl.BlockSpec((TM, ATT_VD), lambda b, h, i: (b * nq + i, h)),
        out_shape=jax.ShapeDtypeStruct((nseq * seq, ATT_WIDTH), F32),
        compiler_params=pltpu.CompilerParams(vmem_limit_bytes=VMEM_LIMIT),
        name="diffattn",
    )(*args)


def _merge_kernel(x_ref, au_ref, av_ref, hs_ref, rg_ref, o_ref, br_ref, g1_ref, an_ref, ws_ref, bs_ref,
                  wa_ref, wb_ref, wc_ref, wo_ref, xo_ref):
    vn = _rms(av_ref[...], an_ref[...])
    rows = []
    for c in range(TM // CHUNK):
        cols = []
        for g in range(A_GROUPS):
            blk = vn[c * CHUNK:(c + 1) * CHUNK, g * 128:(g + 1) * 128]
            cols.append(_dot(ws_ref[g], blk))
        rows.append(jnp.concatenate(cols, axis=1) + bs_ref[...])
    y_a = au_ref[...] * jnp.concatenate(rows, axis=0)
    y_b = hs_ref[...] * jax.nn.gelu(rg_ref[...])
    merged = (jax.nn.sigmoid(br_ref[:, 0:D_MODEL]) * _dot(y_a, wa_ref[...])
              + jax.nn.sigmoid(br_ref[:, D_MODEL:2 * D_MODEL]) * _dot(y_b, wb_ref[...])
              + jax.nn.sigmoid(br_ref[:, 2 * D_MODEL:3 * D_MODEL]) * _dot(o_ref[...], wc_ref[...]))
    xo_ref[...] = x_ref[...] + g1_ref[...] * _dot(merged, wo_ref[...])


def _merge(x, au, av, hs, rg, o, br, mod_l, a_norm, a_ws, bias, wa, wb, wc, wo):
    tok = lambda w: pl.BlockSpec((TM, w), lambda i: (i, 0))
    return pl.pallas_call(
        _merge_kernel,
        grid=(N_TILES,),
        in_specs=[tok(D_MODEL), tok(512), tok(512), tok(512), tok(512), tok(512), tok(N_BRANCH * D_MODEL),
                  _mod_spec(2), _const_spec((1, A_WIDTH)), _const_spec((A_GROUPS, CHUNK, CHUNK)),
                  _const_spec((CHUNK, A_WIDTH)),
                  _const_spec((A_WIDTH, D_MODEL)), _const_spec((LRU_WIDTH, D_MODEL)),
                  _const_spec((ATT_WIDTH, D_MODEL)), _const_spec((D_MODEL, D_MODEL))],
        out_specs=tok(D_MODEL),
        out_shape=jax.ShapeDtypeStruct((N_TOK, D_MODEL), F32),
        compiler_params=pltpu.CompilerParams(vmem_limit_bytes=VMEM_LIMIT),
        name="merge",
    )(x, au, av, hs, rg, o, br, mod_l, a_norm.reshape(1, A_WIDTH), a_ws, bias, wa, wb, wc, wo)


def _top16(s, n):
    pos = lax.broadcasted_iota(jnp.int32, s.shape, 0).astype(F32)
    vals, idxs = [], []
    for _ in range(PEER_TOPK):
        m = jnp.max(s, axis=0, keepdims=True)
        am = jnp.min(jnp.where(s == m, pos, float(n)), axis=0, keepdims=True)
        vals.append(m)
        idxs.append(am)
        s = jnp.where(pos == am, -jnp.inf, s)
    return jnp.concatenate(vals, axis=0), jnp.concatenate(idxs, axis=0)


def _take16(table, sel):
    out = jnp.zeros_like(table)
    for a in range(PEER_TOPK):
        out = jnp.where(sel == float(a), table[a:a + 1], out)
    return out


def _route_kernel(x_ref, sh_ref, sc_ref, n2_ref, wq_ref, keys_ref, h2_ref, idx_ref, gate_ref):
    h2 = _rms(x_ref[...], n2_ref[...]) * (1.0 + sc_ref[...]) + sh_ref[...]
    h2_ref[...] = h2
    q = _dot(h2, wq_ref[...])
    for h in range(PEER_HEADS):
        base = h * PEER_QDIM
        v1, i1 = _top16(_dot_nt(keys_ref[0], q[:, base:base + PEER_HALF]), N_KEYS)
        v2, i2 = _top16(_dot_nt(keys_ref[1], q[:, base + PEER_HALF:base + PEER_QDIM]), N_KEYS)
        cand = jnp.concatenate([v1[a:a + 1] + v2 for a in range(PEER_TOPK)], axis=0)
        top_s, pos = _top16(cand, PEER_TOPK * PEER_TOPK)
        a_sel = jnp.floor(pos * (1.0 / PEER_TOPK))
        b_sel = pos - a_sel * PEER_TOPK
        expert = _take16(i1, a_sel) * N_KEYS + _take16(i2, b_sel)
        e = jnp.exp(top_s - top_s[0:1])
        rows = slice(h * PEER_TOPK, (h + 1) * PEER_TOPK)
        gate_ref[rows, :] = e / jnp.sum(e, axis=0, keepdims=True)
        idx_ref[rows, :] = expert.astype(jnp.int32)


def _route(x, mod_l, norm2, wq, keys):
    tok = pl.BlockSpec((TM, D_MODEL), lambda i: (i, 0))
    pick = pl.BlockSpec((N_PICK, TM), lambda i: (0, i))
    return pl.pallas_call(
        _route_kernel,
        grid=(N_TILES,),
        in_specs=[tok, _mod_spec(3), _mod_spec(4), _const_spec((1, D_MODEL)),
                  _const_spec((D_MODEL, PEER_HEADS * PEER_QDIM)), _const_spec((2, N_KEYS, PEER_HALF))],
        out_specs=[tok, pick, pick],
        out_shape=[jax.ShapeDtypeStruct((N_TOK, D_MODEL), F32),
                   jax.ShapeDtypeStruct((N_PICK, N_TOK), jnp.int32),
                   jax.ShapeDtypeStruct((N_PICK, N_TOK), F32)],
        compiler_params=pltpu.CompilerParams(vmem_limit_bytes=VMEM_LIMIT),
        name="peer_route",
    )(x, mod_l, mod_l, norm2.reshape(1, D_MODEL), wq, keys)


def _pack_table(tab):
    bits = lax.bitcast_convert_type(tab.astype(BF16), jnp.uint16).astype(jnp.uint32)
    words = bits[:, :ROW_WORDS] | (bits[:, ROW_WORDS:] << 16)
    return lax.bitcast_convert_type(words, jnp.int32).reshape(N_EXPERTS * ROW_SUB, 128)


def _unpack(words):
    lo = lax.bitcast_convert_type(words << 16, F32)
    hi = lax.bitcast_convert_type(words & jnp.int32(-65536), F32)
    return lo, hi


def _expert_row(tab_ref, e):
    return tab_ref[pl.ds(pl.multiple_of(e * ROW_SUB, ROW_SUB), ROW_SUB), :]


def _peer_act_kernel(idx_ref, x_ref, gate_ref, tab_ref, o_ref, p_ref):
    ones = jnp.ones((128, 128), BF16)
    eye = (lax.broadcasted_iota(jnp.int32, (N_PICK, N_PICK), 0)
           == lax.broadcasted_iota(jnp.int32, (N_PICK, N_PICK), 1))

    def token(t, carry):
        x = x_ref[t]
        x_lo = x[0:ROW_SUB]
        x_hi = x[ROW_SUB:2 * ROW_SUB]
        for k in range(N_PICK):
            lo, hi = _unpack(_expert_row(tab_ref, idx_ref[t, k]))
            p_ref[k:k + 1, :] = jnp.sum(lo * x_lo + hi * x_hi, axis=0, keepdims=True)
        p = p_ref[...]
        p_hi = p.astype(BF16)
        p_lo = (p - p_hi.astype(F32)).astype(BF16)
        act = (jnp.dot(p_hi, ones, preferred_element_type=F32)
               + jnp.dot(p_lo, ones, preferred_element_type=F32))
        gate = gate_ref[pl.ds(t, 1), :]
        coef = jnp.where(eye, jax.nn.gelu(act) * gate, 0.0)
        o_ref[pl.ds(t, 1), :] = jnp.sum(coef, axis=0, keepdims=True)
        return carry

    lax.fori_loop(0, TP, token, 0)


def _peer_out_kernel(idx_ref, coef_ref, tab_ref, o_ref, c_ref):
    ones = jnp.ones((128, 128), BF16)
    eye = (lax.broadcasted_iota(jnp.int32, (N_PICK, N_PICK), 0)
           == lax.broadcasted_iota(jnp.int32, (N_PICK, N_PICK), 1))

    def token(t, carry):
        c = jnp.where(eye, coef_ref[pl.ds(t, 1), :], 0.0)
        c_hi = c.astype(BF16)
        c_lo = (c - c_hi.astype(F32)).astype(BF16)
        c_ref[...] = (jnp.dot(c_hi, ones, preferred_element_type=F32)
                      + jnp.dot(c_lo, ones, preferred_element_type=F32))
        acc_lo = jnp.zeros((ROW_SUB, 128), F32)
        acc_hi = jnp.zeros((ROW_SUB, 128), F32)
        for k in range(N_PICK):
            lo, hi = _unpack(_expert_row(tab_ref, idx_ref[t, k]))
            ck = c_ref[k:k + 1, :]
            acc_lo = acc_lo + ck * lo
            acc_hi = acc_hi + ck * hi
        o_ref[t] = jnp.concatenate([acc_lo, acc_hi], axis=0)
        return carry

    lax.fori_loop(0, TP, token, 0)


def _peer_experts(h2, idx, gate, u_words, v_words):
    smem_idx = pl.BlockSpec((TP, N_PICK), lambda i: (i, 0), memory_space=pltpu.SMEM)
    pick = pl.BlockSpec((TP, N_PICK), lambda i: (i, 0))
    rows = pl.BlockSpec((TP, 8, 128), lambda i: (i, 0, 0))
    table = pl.BlockSpec((N_EXPERTS * ROW_SUB, 128), lambda i: (0, 0), pipeline_mode=pl.Buffered(1))
    params = pltpu.CompilerParams(vmem_limit_bytes=VMEM_LIMIT)
    coef = pl.pallas_call(
        _peer_act_kernel,
        grid=(N_TOK // TP,),
        in_specs=[smem_idx, rows, pick, table],
        out_specs=pick,
        out_shape=jax.ShapeDtypeStruct((N_TOK, N_PICK), F32),
        scratch_shapes=[pltpu.VMEM((N_PICK, 128), F32)],
        compiler_params=params,
        name="peer_act",
    )(idx, h2.reshape(N_TOK, 8, 128), gate, u_words)
    out = pl.pallas_call(
        _peer_out_kernel,
        grid=(N_TOK // TP,),
        in_specs=[smem_idx, pick, table],
        out_specs=rows,
        out_shape=jax.ShapeDtypeStruct((N_TOK, 8, 128), F32),
        scratch_shapes=[pltpu.VMEM((N_PICK, 128), F32)],
        compiler_params=params,
        name="peer_out",
    )(idx, coef, v_words)
    return out.reshape(N_TOK, D_MODEL)


def _final_kernel(x_ref, p_ref, g2_ref, n_ref, o_ref):
    o_ref[...] = _rms(x_ref[...] + g2_ref[...] * p_ref[...], n_ref[...])


def _final(x, p, mod_l, final_norm):
    tok = pl.BlockSpec((TM, D_MODEL), lambda i: (i, 0))
    return pl.pallas_call(
        _final_kernel,
        grid=(N_TILES,),
        in_specs=[tok, tok, _mod_spec(5), _const_spec((1, D_MODEL))],
        out_specs=tok,
        out_shape=jax.ShapeDtypeStruct((N_TOK, D_MODEL), F32),
        name="final_norm",
    )(x, p, mod_l, final_norm.reshape(1, D_MODEL))


def _rope_tables():
    rows = DEC_SEQ // GRID_W
    row_ids = jnp.repeat(jnp.arange(rows), GRID_W).astype(F32)
    col_ids = jnp.tile(jnp.arange(GRID_W), rows).astype(F32)
    inv_freq = ROPE_BASE ** (-jnp.arange(ROPE_FREQS, dtype=F32) / ROPE_FREQS)
    ang_r = row_ids[:, None] * inv_freq
    ang_c = col_ids[:, None] * inv_freq
    cos = jnp.concatenate([jnp.cos(ang_r), jnp.cos(ang_r), jnp.cos(ang_c), jnp.cos(ang_c)], axis=1)
    sin = jnp.concatenate([-jnp.sin(ang_r), jnp.sin(ang_r), -jnp.sin(ang_c), jnp.sin(ang_c)], axis=1)
    reps = ATT_WIDTH // ATT_HD
    cos = jnp.concatenate([jnp.ones((TM, ATT_WIDTH), F32), jnp.tile(cos, (1, reps))], axis=0)
    sin = jnp.concatenate([jnp.zeros((TM, ATT_WIDTH), F32), jnp.tile(sin, (1, reps))], axis=0)
    return cos, sin


def kernel(x_prompt, x_sample, cache_k, cache_v, state_lru, c, c_ctx, w_mod, b_mod, norm1, norm2, w_in, a_norm, a_ws, a_bs, lru_conv_w, lru_conv_b, lru_w_r, lru_b_r, lru_w_i, lru_b_i, lru_lam, att_lam, att_subln, w_up_a, w_up_b, w_up_c, w_out, peer_wq, peer_keys, peer_u, peer_v, final_norm):
    x = jnp.concatenate([x_prompt.reshape(N_CTX_TOK, D_MODEL), x_sample.reshape(N_LAT_TOK, D_MODEL)], axis=0)
    cond = jnp.concatenate([c_ctx[None, :], c, jnp.zeros((N_COND - 1 - DEC_BATCH, D_MODEL), F32)], axis=0)
    mod = _modulation(cond, w_mod, b_mod).reshape(DEPTH, N_COND, 1, N_MOD * D_MODEL)
    cos_t, sin_t = _rope_tables()
    zero_h0 = jnp.zeros((BATCH, 2, LRU_WIDTH), F32)
    ks, vs, hs = [], [], []
    res = None
    for i in range(DEPTH):
        lam_init = 0.8 - 0.6 * math.exp(-0.3 * i)
        outs = _inproj(x, res, mod[i], norm1[i], w_in[i].astype(BF16), cos_t, sin_t)
        au, av, rx, rg, q, k, kr, v, br = outs[:9]
        if res is not None:
            x = outs[9]
        w_r = _blockdiag128(lru_w_r[i])
        w_i = _blockdiag128(lru_w_i[i])
        lru_args = (lru_conv_w[i], lru_conv_b[i], w_r, lru_b_r[i], w_i, lru_b_i[i], lru_lam[i])
        hs_ctx, hl_ctx = _lru(rx, 0, BATCH, SEQ, *lru_args, zero_h0)
        hs_lat, _ = _lru(rx, N_CTX_TOK, DEC_BATCH, DEC_SEQ, *lru_args, state_lru[:, i])
        o_ctx = _attention(q, kr, v, 0, BATCH, SEQ, None, att_lam[i], att_subln[i], lam_init)
        ctx = (cache_k[:, i].reshape(DEC_BATCH, PAST_LEN, ATT_WIDTH), cache_v[:, i].reshape(DEC_BATCH, PAST_LEN, ATT_WIDTH))
        o_lat = _attention(q, kr, v, N_CTX_TOK, DEC_BATCH, DEC_SEQ, ctx, att_lam[i], att_subln[i], lam_init)
        bias = jnp.repeat(a_bs[i].T, CHUNK, axis=1)
        x = _merge(x, au, av, jnp.concatenate([hs_ctx, hs_lat], axis=0), rg, jnp.concatenate([o_ctx, o_lat], axis=0),
                   br, mod[i], a_norm[i], a_ws[i], bias, w_up_a[i].astype(BF16), w_up_b[i].astype(BF16),
                   w_up_c[i].astype(BF16), w_out[i].astype(BF16))
        h2, idx, gate = _route(x, mod[i], norm2[i], peer_wq[i].astype(BF16), peer_keys[i])
        p = _peer_experts(h2, idx.T, gate.T, _pack_table(peer_u[i]), _pack_table(peer_v[i]))
        res = (p, mod[i])
        ks.append(k[:N_CTX_TOK].reshape(BATCH, SEQ, ATT_HEADS, ATT_VD))
        vs.append(v[:N_CTX_TOK].reshape(BATCH, SEQ, ATT_HEADS, ATT_VD))
        hs.append(hl_ctx)
    y = _final(x, res[0], res[1], final_norm)
    return (y[:N_CTX_TOK].reshape(BATCH, SEQ, D_MODEL), y[N_CTX_TOK:].reshape(DEC_BATCH, DEC_SEQ, D_MODEL),
            jnp.stack(ks, axis=1), jnp.stack(vs, axis=1), jnp.stack(hs, axis=1))
```

```python
import functools
import math

import jax
import jax.numpy as jnp
from jax import lax
from jax.experimental import pallas as pl
from jax.experimental.pallas import tpu as pltpu

D_MODEL = 1024
BATCH = 16
SEQ = 256
DEPTH = 2
DEC_BATCH = 4
DEC_SEQ = 2048
PAST_LEN = 512
GRID_W = 64
EPS = 1e-6
N_MOD = 6
CHUNK = 128
A_GROUPS = 4
A_WIDTH = 512
LRU_BLOCKS = 8
LRU_BLOCK_W = 64
LRU_WIDTH = 512
CONV_W = 4
LRU_C = 8.0
ATT_HEADS = 4
ATT_HD = 64
ATT_VD = 128
ATT_WIDTH = 512
ROPE_BASE = 10000.0
ROPE_FREQS = 16
N_BRANCH = 3
IN_SPLITS = (512, 1024, 1536, 2048, 2560, 3072, 3584)
IN_WIDTH = 3584 + N_BRANCH * D_MODEL
PEER_HEADS = 8
N_KEYS = 128
N_EXPERTS = N_KEYS * N_KEYS
PEER_QDIM = 256
PEER_HALF = 128
PEER_TOPK = 16
N_PICK = PEER_HEADS * PEER_TOPK

N_CTX_TOK = BATCH * SEQ
N_LAT_TOK = DEC_BATCH * DEC_SEQ
N_TOK = N_CTX_TOK + N_LAT_TOK
TM = 256
N_TILES = N_TOK // TM
CTX_TILES = N_CTX_TOK // TM
LAT_TILES_PER_SEQ = DEC_SEQ // TM
N_COND = 8
TP = 64
ROW_WORDS = D_MODEL // 2
ROW_SUB = ROW_WORDS // 128
PLANE_STRIDE = N_PICK + 8
VMEM_LIMIT = 56 * 1024 * 1024

F32 = jnp.float32
BF16 = jnp.bfloat16
HI = lax.Precision.HIGHEST


def _cond_row(i):
    return jnp.maximum(i - LAT_TILES_PER_SEQ, 0) // LAT_TILES_PER_SEQ


def _pos_block(i):
    return jnp.where(i < CTX_TILES, 0, 1 + i % LAT_TILES_PER_SEQ)


def _mod_spec(chunk):
    return pl.BlockSpec((None, 1, D_MODEL), lambda i: (_cond_row(i), 0, chunk))


def _const_spec(shape):
    nd = len(shape)
    return pl.BlockSpec(shape, lambda *_: (0,) * nd)


def _rms(x, gain):
    return x * lax.rsqrt(jnp.mean(x * x, axis=-1, keepdims=True) + EPS) * gain


def _dot(a, b):
    return jnp.dot(a.astype(BF16), b.astype(BF16), preferred_element_type=F32)


def _dot_nt(a, b):
    return lax.dot_general(a.astype(BF16), b.astype(BF16), (((1,), (1,)), ((), ())), preferred_element_type=F32)


def _mod_kernel(cond_ref, w_ref, b_ref, o_ref):
    cond = cond_ref[...]
    act = cond * jax.nn.sigmoid(cond)
    o_ref[...] = jnp.dot(act, w_ref[...], preferred_element_type=F32, precision=HI) + b_ref[...]


def _modulation(cond, w_mod, b_mod):
    nc = 4
    cw = N_MOD * D_MODEL // nc
    return pl.pallas_call(
        _mod_kernel,
        grid=(DEPTH, nc),
        in_specs=[pl.BlockSpec((N_COND, D_MODEL), lambda l, j: (0, 0)),
                  pl.BlockSpec((None, D_MODEL, cw), lambda l, j: (l, 0, j)),
                  pl.BlockSpec((None, 1, cw), lambda l, j: (l, 0, j))],
        out_specs=pl.BlockSpec((None, N_COND, cw), lambda l, j: (l, 0, j)),
        out_shape=jax.ShapeDtypeStruct((DEPTH, N_COND, N_MOD * D_MODEL), F32),
        compiler_params=pltpu.CompilerParams(vmem_limit_bytes=VMEM_LIMIT),
        name="modulation",
    )(cond, w_mod, b_mod.reshape(DEPTH, 1, N_MOD * D_MODEL))


def _inproj_kernel(has_res, *refs):
    if has_res:
        x_ref, p_ref, g2_ref = refs[:3]
        refs = refs[3:]
        x = x_ref[...] + g2_ref[...] * p_ref[...]
    else:
        x_ref = refs[0]
        refs = refs[1:]
        x = x_ref[...]
    (sh_ref, sc_ref, n1_ref, w_ref, cos_ref, sin_ref,
     au_ref, av_ref, rx_ref, rg_ref, q_ref, k_ref, kr_ref, v_ref, br_ref) = refs[:15]
    if has_res:
        refs[15][...] = x
    h = (_rms(x, n1_ref[...]) * (1.0 + sc_ref[...]) + sh_ref[...]).astype(BF16)

    def proj(lo, hi):
        return jnp.dot(h, w_ref[:, lo:hi], preferred_element_type=F32)

    au_ref[...] = proj(0, IN_SPLITS[0])
    av_ref[...] = proj(IN_SPLITS[0], IN_SPLITS[1])
    rx_ref[...] = proj(IN_SPLITS[1], IN_SPLITS[2])
    rg_ref[...] = proj(IN_SPLITS[2], IN_SPLITS[3])
    q = proj(IN_SPLITS[3], IN_SPLITS[4])
    k = proj(IN_SPLITS[4], IN_SPLITS[5])
    v_ref[...] = proj(IN_SPLITS[5], IN_SPLITS[6])
    for j in range(N_BRANCH):
        lo = IN_SPLITS[6] + j * D_MODEL
        br_ref[:, j * D_MODEL:(j + 1) * D_MODEL] = proj(lo, lo + D_MODEL)
    k_ref[...] = k
    lane = lax.broadcasted_iota(jnp.int32, (TM, ATT_WIDTH), 1)
    first = (lane % (2 * ROPE_FREQS)) < ROPE_FREQS
    cos = cos_ref[...]
    sin = sin_ref[...]

    def rot(t):
        partner = jnp.where(first, pltpu.roll(t, ATT_WIDTH - ROPE_FREQS, 1), pltpu.roll(t, ROPE_FREQS, 1))
        return t * cos + partner * sin

    q_ref[...] = rot(q) * (ATT_HD ** -0.5)
    kr_ref[...] = rot(k)


def _inproj(x, res, mod_l, norm1, w_in, cos_t, sin_t):
    has_res = res is not None
    tok = lambda w: pl.BlockSpec((TM, w), lambda i: (i, 0))
    in_specs = [tok(D_MODEL)]
    args = [x]
    if has_res:
        p, mod_prev = res
        in_specs += [tok(D_MODEL), _mod_spec(5)]
        args += [p, mod_prev]
    in_specs += [_mod_spec(0), _mod_spec(1), _const_spec((1, D_MODEL)),
                 pl.BlockSpec((D_MODEL, IN_WIDTH), lambda i: (0, 0), pipeline_mode=pl.Buffered(1)),
                 pl.BlockSpec((TM, ATT_WIDTH), lambda i: (_pos_block(i), 0)),
                 pl.BlockSpec((TM, ATT_WIDTH), lambda i: (_pos_block(i), 0))]
    args += [mod_l, mod_l, norm1.reshape(1, D_MODEL), w_in, cos_t, sin_t]
    widths = [512] * 8 + [N_BRANCH * D_MODEL]
    out_specs = [tok(w) for w in widths]
    out_shape = [jax.ShapeDtypeStruct((N_TOK, w), F32) for w in widths]
    if has_res:
        out_specs.append(tok(D_MODEL))
        out_shape.append(jax.ShapeDtypeStruct((N_TOK, D_MODEL), F32))
    return pl.pallas_call(
        functools.partial(_inproj_kernel, has_res),
        grid=(N_TILES,),
        in_specs=in_specs,
        out_specs=out_specs,
        out_shape=out_shape,
        compiler_params=pltpu.CompilerParams(vmem_limit_bytes=VMEM_LIMIT),
        name="inproj",
    )(*args)


def _lru_kernel(seq, x_ref, cw_ref, cb_ref, wr_ref, br_ref, wi_ref, bi_ref, lam_ref, h0_ref, hs_ref, hl_ref):
    x = x_ref[...]
    t = lax.broadcasted_iota(jnp.int32, (seq, 128), 0)

    def shifted(v, k, fill):
        r = pltpu.roll(v, k % seq, 0)
        ok = (t >= k) if k > 0 else (t < seq + k)
        return jnp.where(ok, r, fill)

    cw = cw_ref[...]
    xc = (shifted(x, 2, 0.0) * cw[0:1] + shifted(x, 1, 0.0) * cw[1:2] + x * cw[2:3]
          + shifted(x, -1, 0.0) * cw[3:4] + cb_ref[...])

    def direction(d):
        r = jax.nn.sigmoid(jnp.dot(xc, wr_ref[d, 0], preferred_element_type=F32, precision=HI) + br_ref[d:d + 1])
        g = jax.nn.sigmoid(jnp.dot(xc, wi_ref[d, 0], preferred_element_type=F32, precision=HI) + bi_ref[d:d + 1])
        z = -lam_ref[d:d + 1]
        softplus = jnp.maximum(z, 0.0) + jnp.log(1.0 + jnp.exp(-jnp.abs(z)))
        log_a = -LRU_C * r * softplus
        a = jnp.exp(log_a)
        b = jnp.sqrt(1.0 - jnp.exp(2.0 * log_a)) * (g * xc)
        sgn = 1 if d == 0 else -1
        k = 1
        while k < seq:
            a_prev = shifted(a, sgn * k, 1.0)
            b_prev = shifted(b, sgn * k, 0.0)
            b = a * b_prev + b
            a = a * a_prev
            k *= 2
        return a * h0_ref[0, d:d + 1] + b

    hf = direction(0)
    hb = direction(1)
    hs_ref[...] = hf + hb
    hl_ref[0] = jnp.concatenate([hf[seq - 1:seq], hb[0:1]], axis=0)


def _lru(rx, row0, nseq, seq, conv_w, conv_b, w_r, b_r, w_i, b_i, lam, h0):
    nb = LRU_WIDTH // 128
    blk0 = row0 // seq
    par = lambda shape: pl.BlockSpec(shape, lambda b, c: (0,) * (len(shape) - 1) + (c,))
    return pl.pallas_call(
        functools.partial(_lru_kernel, seq),
        grid=(nseq, nb),
        in_specs=[pl.BlockSpec((seq, 128), lambda b, c: (blk0 + b, c)),
                  par((CONV_W, 128)), par((1, 128)),
                  pl.BlockSpec((2, 1, 128, 128), lambda b, c: (0, c, 0, 0)), par((2, 128)),
                  pl.BlockSpec((2, 1, 128, 128), lambda b, c: (0, c, 0, 0)), par((2, 128)),
                  par((2, 128)),
                  pl.BlockSpec((1, 2, 128), lambda b, c: (b, 0, c))],
        out_specs=[pl.BlockSpec((seq, 128), lambda b, c: (b, c)),
                   pl.BlockSpec((1, 2, 128), lambda b, c: (b, 0, c))],
        out_shape=[jax.ShapeDtypeStruct((nseq * seq, LRU_WIDTH), F32),
                   jax.ShapeDtypeStruct((nseq, 2, LRU_WIDTH), F32)],
        compiler_params=pltpu.CompilerParams(vmem_limit_bytes=VMEM_LIMIT),
        name="rglru",
    )(rx, conv_w, conv_b.reshape(1, LRU_WIDTH), w_r, b_r, w_i, b_i, lam, h0)


def _blockdiag128(w):
    w = w.reshape(2, LRU_BLOCKS // 2, 2, LRU_BLOCK_W, LRU_BLOCK_W)
    z = jnp.zeros_like(w[:, :, 0])
    top = jnp.concatenate([w[:, :, 0], z], axis=-1)
    bot = jnp.concatenate([z, w[:, :, 1]], axis=-1)
    return jnp.concatenate([top, bot], axis=-2)


def _attn_kernel(has_ctx, lam_init, *refs):
    if has_ctx:
        q_ref, k_ref, v_ref, kc_ref, vc_ref, lp_ref, g_ref, o_ref = refs
    else:
        q_ref, k_ref, v_ref, lp_ref, g_ref, o_ref = refs
    lp = lp_ref[...]
    lam = (jnp.exp(jnp.sum(lp[0:1] * lp[1:2], axis=-1, keepdims=True))
           - jnp.exp(jnp.sum(lp[2:3] * lp[3:4], axis=-1, keepdims=True)) + lam_init)
    q = q_ref[...]
    lane = lax.broadcasted_iota(jnp.int32, q.shape, 1)
    halves = (jnp.where(lane < ATT_HD, q, 0.0), jnp.where(lane >= ATT_HD, q, 0.0))
    k = k_ref[...]
    v = v_ref[...]
    w = []
    for qh in halves:
        s = _dot_nt(qh, k)
        m = jnp.max(s, axis=-1, keepdims=True)
        if has_ctx:
            sc = _dot_nt(qh, kc_ref[0])
            m = jnp.maximum(m, jnp.max(sc, axis=-1, keepdims=True))
            ec = jnp.exp(sc - m)
        e = jnp.exp(s - m)
        den = jnp.sum(e, axis=-1, keepdims=True)
        if has_ctx:
            den = den + jnp.sum(ec, axis=-1, keepdims=True)
            w.append((e / den, ec / den))
        else:
            w.append((e / den,))
    o = _dot(w[0][0] - lam * w[1][0], v)
    if has_ctx:
        o = o + _dot(w[0][1] - lam * w[1][1], vc_ref[0])
    o_ref[...] = _rms(o, g_ref[...]) * (1.0 - lam_init)


def _attention(q, kr, v, row0, nseq, seq, ctx, att_lam, subln, lam_init):
    has_ctx = ctx is not None
    nq = seq // TM
    blk0 = row0 // seq
    in_specs = [pl.BlockSpec((TM, ATT_VD), lambda b, h, i: ((row0 // TM) + b * nq + i, h)),
                pl.BlockSpec((seq, ATT_VD), lambda b, h, i: (blk0 + b, h)),
                pl.BlockSpec((seq, ATT_VD), lambda b, h, i: (blk0 + b, h))]
    args = [q, kr, v]
    if has_ctx:
        in_specs += [pl.BlockSpec((1, PAST_LEN, ATT_VD), lambda b, h, i: (b, 0, h))] * 2
        args += list(ctx)
    in_specs += [_const_spec((4, ATT_HD)), _const_spec((1, ATT_VD))]
    args += [att_lam, subln.reshape(1, ATT_VD)]
    return pl.pallas_call(
        functools.partial(_attn_kernel, has_ctx, lam_init),
        grid=(nseq, ATT_HEADS, nq),
        in_specs=in_specs,
        out_specs=pl.BlockSpec((TM, ATT_VD), lambda b, h, i: (b * nq + i, h)),
        out_shape=jax.ShapeDtypeStruct((nseq * seq, ATT_WIDTH), F32),
        compiler_params=pltpu.CompilerParams(vmem_limit_bytes=VMEM_LIMIT),
        name="diffattn",
    )(*args)


def _merge_kernel(x_ref, au_ref, av_ref, hs_ref, rg_ref, o_ref, br_ref, g1_ref, an_ref, ws_ref, bs_ref,
                  wa_ref, wb_ref, wc_ref, wo_ref, xo_ref):
    vn = _rms(av_ref[...], an_ref[...])
    rows = []
    for c in range(TM // CHUNK):
        cols = []
        for g in range(A_GROUPS):
            blk = vn[c * CHUNK:(c + 1) * CHUNK, g * 128:(g + 1) * 128]
            cols.append(_dot(ws_ref[g], blk))
        rows.append(jnp.concatenate(cols, axis=1) + bs_ref[...])
    y_a = au_ref[...] * jnp.concatenate(rows, axis=0)
    y_b = hs_ref[...] * jax.nn.gelu(rg_ref[...])
    merged = (jax.nn.sigmoid(br_ref[:, 0:D_MODEL]) * _dot(y_a, wa_ref[...])
              + jax.nn.sigmoid(br_ref[:, D_MODEL:2 * D_MODEL]) * _dot(y_b, wb_ref[...])
              + jax.nn.sigmoid(br_ref[:, 2 * D_MODEL:3 * D_MODEL]) * _dot(o_ref[...], wc_ref[...]))
    xo_ref[...] = x_ref[...] + g1_ref[...] * _dot(merged, wo_ref[...])


def _merge(x, au, av, hs, rg, o, br, mod_l, a_norm, a_ws, bias, wa, wb, wc, wo):
    tok = lambda w: pl.BlockSpec((TM, w), lambda i: (i, 0))
    return pl.pallas_call(
        _merge_kernel,
        grid=(N_TILES,),
        in_specs=[tok(D_MODEL), tok(512), tok(512), tok(512), tok(512), tok(512), tok(N_BRANCH * D_MODEL),
                  _mod_spec(2), _const_spec((1, A_WIDTH)), _const_spec((A_GROUPS, CHUNK, CHUNK)),
                  _const_spec((CHUNK, A_WIDTH)),
                  _const_spec((A_WIDTH, D_MODEL)), _const_spec((LRU_WIDTH, D_MODEL)),
                  _const_spec((ATT_WIDTH, D_MODEL)), _const_spec((D_MODEL, D_MODEL))],
        out_specs=tok(D_MODEL),
        out_shape=jax.ShapeDtypeStruct((N_TOK, D_MODEL), F32),
        compiler_params=pltpu.CompilerParams(vmem_limit_bytes=VMEM_LIMIT),
        name="merge",
    )(x, au, av, hs, rg, o, br, mod_l, a_norm.reshape(1, A_WIDTH), a_ws, bias, wa, wb, wc, wo)


def _top16(s, n):
    pos = lax.broadcasted_iota(jnp.int32, s.shape, 0).astype(F32)
    vals, idxs = [], []
    for _ in range(PEER_TOPK):
        m = jnp.max(s, axis=0, keepdims=True)
        am = jnp.min(jnp.where(s == m, pos, float(n)), axis=0, keepdims=True)
        vals.append(m)
        idxs.append(am)
        s = jnp.where(pos == am, -jnp.inf, s)
    return jnp.concatenate(vals, axis=0), jnp.concatenate(idxs, axis=0)


def _take16(table, sel):
    out = jnp.zeros_like(table)
    for a in range(PEER_TOPK):
        out = jnp.where(sel == float(a), table[a:a + 1], out)
    return out


def _route_kernel(x_ref, sh_ref, sc_ref, n2_ref, wq_ref, keys_ref, h2_ref, idx_ref, gate_ref):
    h2 = _rms(x_ref[...], n2_ref[...]) * (1.0 + sc_ref[...]) + sh_ref[...]
    h2_ref[...] = h2
    q = _dot(h2, wq_ref[...])
    for h in range(PEER_HEADS):
        base = h * PEER_QDIM
        v1, i1 = _top16(_dot_nt(keys_ref[0], q[:, base:base + PEER_HALF]), N_KEYS)
        v2, i2 = _top16(_dot_nt(keys_ref[1], q[:, base + PEER_HALF:base + PEER_QDIM]), N_KEYS)
        cand = jnp.concatenate([v1[a:a + 1] + v2 for a in range(PEER_TOPK)], axis=0)
        top_s, pos = _top16(cand, PEER_TOPK * PEER_TOPK)
        a_sel = jnp.floor(pos * (1.0 / PEER_TOPK))
        b_sel = pos - a_sel * PEER_TOPK
        expert = _take16(i1, a_sel) * N_KEYS + _take16(i2, b_sel)
        e = jnp.exp(top_s - top_s[0:1])
        rows = slice(h * PEER_TOPK, (h + 1) * PEER_TOPK)
        gate_ref[rows, :] = e / jnp.sum(e, axis=0, keepdims=True)
        idx_ref[rows, :] = expert.astype(jnp.int32) * ROW_SUB


def _route(x, mod_l, norm2, wq, keys):
    tok = pl.BlockSpec((TM, D_MODEL), lambda i: (i, 0))
    pick = pl.BlockSpec((N_PICK, TM), lambda i: (0, i))
    return pl.pallas_call(
        _route_kernel,
        grid=(N_TILES,),
        in_specs=[tok, _mod_spec(3), _mod_spec(4), _const_spec((1, D_MODEL)),
                  _const_spec((D_MODEL, PEER_HEADS * PEER_QDIM)), _const_spec((2, N_KEYS, PEER_HALF))],
        out_specs=[tok, pick, pick],
        out_shape=[jax.ShapeDtypeStruct((N_TOK, D_MODEL), F32),
                   jax.ShapeDtypeStruct((N_PICK, N_TOK), jnp.int32),
                   jax.ShapeDtypeStruct((N_PICK, N_TOK), F32)],
        compiler_params=pltpu.CompilerParams(vmem_limit_bytes=VMEM_LIMIT),
        name="peer_route",
    )(x, mod_l, mod_l, norm2.reshape(1, D_MODEL), wq, keys)


def _pack_table(tab):
    bits = lax.bitcast_convert_type(tab.astype(BF16), jnp.uint16).astype(jnp.uint32)
    words = bits[:, :ROW_WORDS] | (bits[:, ROW_WORDS:] << 16)
    return lax.bitcast_convert_type(words, jnp.int32).reshape(N_EXPERTS * ROW_SUB, 128)


def _unpack(words):
    lo = lax.bitcast_convert_type(words << 16, F32)
    hi = lax.bitcast_convert_type(words & jnp.int32(-65536), F32)
    return lo, hi


def _gather_planes(tab_ref, idx_ref, t, g_ref):
    for k in range(N_PICK):
        row = pl.multiple_of(idx_ref[t, k], ROW_SUB)
        g_ref[pl.ds(k, ROW_SUB, stride=PLANE_STRIDE), :] = tab_ref[pl.ds(row, ROW_SUB), :]


def _plane(g_ref, s):
    return g_ref[s * PLANE_STRIDE:s * PLANE_STRIDE + N_PICK, :]


def _split_bf16(v):
    hi = v.astype(BF16)
    return hi, (v - hi.astype(F32)).astype(BF16)


def _pipelined_tokens(tab_ref, idx_ref, bufs, compute):
    _gather_planes(tab_ref, idx_ref, 0, bufs[0])

    def pair(j, carry):
        for p in range(2):
            t = 2 * j + p
            _gather_planes(tab_ref, idx_ref, jnp.minimum(t + 1, TP - 1), bufs[1 - p])
            compute(t, bufs[p])
        return carry

    lax.fori_loop(0, TP // 2, pair, 0)


def _peer_act_kernel(idx_ref, x_ref, gate2_ref, tab_ref, o_ref, ga_ref, gb_ref):
    ones = jnp.ones((128, 128), BF16)
    sub2 = 2 * lax.broadcasted_iota(jnp.int32, (N_PICK, 2 * N_PICK), 0)
    lane = lax.broadcasted_iota(jnp.int32, (N_PICK, 2 * N_PICK), 1)
    even = lane == sub2
    odd = lane == sub2 + 1

    def token(t, g_ref):
        x = x_ref[t]
        acc = jnp.zeros((N_PICK, 128), F32)
        for s in range(ROW_SUB):
            lo, hi = _unpack(_plane(g_ref, s))
            acc = acc + lo * x[s:s + 1] + hi * x[ROW_SUB + s:ROW_SUB + s + 1]
        a_hi, a_lo = _split_bf16(acc)
        act = (jnp.dot(a_hi, ones, preferred_element_type=F32)
               + jnp.dot(a_lo, ones, preferred_element_type=F32))
        g = jax.nn.gelu(act)
        coef = jnp.concatenate([g, g], axis=1) * gate2_ref[pl.ds(t, 1), :]
        o_ref[t] = jnp.concatenate([jnp.sum(jnp.where(even, coef, 0.0), axis=0, keepdims=True),
                                    jnp.sum(jnp.where(odd, coef, 0.0), axis=0, keepdims=True)], axis=0)

    _pipelined_tokens(tab_ref, idx_ref, (ga_ref, gb_ref), token)


def _peer_out_kernel(idx_ref, coef_ref, tab_ref, o_ref, ga_ref, gb_ref):
    def token(t, g_ref):
        c_hi, c_lo = _split_bf16(coef_ref[t])
        lhs = jnp.concatenate([c_hi, c_lo, jnp.zeros((4, 2 * N_PICK), BF16)], axis=0)
        lo_rows, hi_rows = [], []
        for s in range(ROW_SUB):
            w = pltpu.bitcast(_plane(g_ref, s), BF16)
            r = jnp.dot(lhs, w, preferred_element_type=F32)
            lo_rows.append(r[0:1] + r[2:3])
            hi_rows.append(r[1:2] + r[3:4])
        o_ref[t] = jnp.concatenate(lo_rows + hi_rows, axis=0)

    _pipelined_tokens(tab_ref, idx_ref, (ga_ref, gb_ref), token)


def _peer_experts(h2, idx, gate2, u_words, v_words):
    smem_idx = pl.BlockSpec((TP, N_PICK), lambda i: (i, 0), memory_space=pltpu.SMEM)
    rows = pl.BlockSpec((TP, 8, 128), lambda i: (i, 0, 0))
    pair = pl.BlockSpec((TP, 2, 2 * N_PICK), lambda i: (i, 0, 0))
    table = pl.BlockSpec((N_EXPERTS * ROW_SUB, 128), lambda i: (0, 0), pipeline_mode=pl.Buffered(1))
    params = pltpu.CompilerParams(vmem_limit_bytes=VMEM_LIMIT)
    planes = pltpu.VMEM((ROW_SUB * PLANE_STRIDE, 128), jnp.int32)
    coef = pl.pallas_call(
        _peer_act_kernel,
        grid=(N_TOK // TP,),
        in_specs=[smem_idx, rows, pl.BlockSpec((TP, 2 * N_PICK), lambda i: (i, 0)), table],
        out_specs=pair,
        out_shape=jax.ShapeDtypeStruct((N_TOK, 2, 2 * N_PICK), F32),
        scratch_shapes=[planes, planes],
        compiler_params=params,
        name="peer_act",
    )(idx, h2.reshape(N_TOK, 8, 128), gate2, u_words)
    out = pl.pallas_call(
        _peer_out_kernel,
        grid=(N_TOK // TP,),
        in_specs=[smem_idx, pair, table],
        out_specs=rows,
        out_shape=jax.ShapeDtypeStruct((N_TOK, 8, 128), F32),
        scratch_shapes=[planes, planes],
        compiler_params=params,
        name="peer_out",
    )(idx, coef, v_words)
    return out.reshape(N_TOK, D_MODEL)


def _final_kernel(x_ref, p_ref, g2_ref, n_ref, o_ref):
    o_ref[...] = _rms(x_ref[...] + g2_ref[...] * p_ref[...], n_ref[...])


def _final(x, p, mod_l, final_norm):
    tok = pl.BlockSpec((TM, D_MODEL), lambda i: (i, 0))
    return pl.pallas_call(
        _final_kernel,
        grid=(N_TILES,),
        in_specs=[tok, tok, _mod_spec(5), _const_spec((1, D_MODEL))],
        out_specs=tok,
        out_shape=jax.ShapeDtypeStruct((N_TOK, D_MODEL), F32),
        name="final_norm",
    )(x, p, mod_l, final_norm.reshape(1, D_MODEL))


def _rope_tables():
    rows = DEC_SEQ // GRID_W
    row_ids = jnp.repeat(jnp.arange(rows), GRID_W).astype(F32)
    col_ids = jnp.tile(jnp.arange(GRID_W), rows).astype(F32)
    inv_freq = ROPE_BASE ** (-jnp.arange(ROPE_FREQS, dtype=F32) / ROPE_FREQS)
    ang_r = row_ids[:, None] * inv_freq
    ang_c = col_ids[:, None] * inv_freq
    cos = jnp.concatenate([jnp.cos(ang_r), jnp.cos(ang_r), jnp.cos(ang_c), jnp.cos(ang_c)], axis=1)
    sin = jnp.concatenate([-jnp.sin(ang_r), jnp.sin(ang_r), -jnp.sin(ang_c), jnp.sin(ang_c)], axis=1)
    reps = ATT_WIDTH // ATT_HD
    cos = jnp.concatenate([jnp.ones((TM, ATT_WIDTH), F32), jnp.tile(cos, (1, reps))], axis=0)
    sin = jnp.concatenate([jnp.zeros((TM, ATT_WIDTH), F32), jnp.tile(sin, (1, reps))], axis=0)
    return cos, sin


def kernel(x_prompt, x_sample, cache_k, cache_v, state_lru, c, c_ctx, w_mod, b_mod, norm1, norm2, w_in, a_norm, a_ws, a_bs, lru_conv_w, lru_conv_b, lru_w_r, lru_b_r, lru_w_i, lru_b_i, lru_lam, att_lam, att_subln, w_up_a, w_up_b, w_up_c, w_out, peer_wq, peer_keys, peer_u, peer_v, final_norm):
    x = jnp.concatenate([x_prompt.reshape(N_CTX_TOK, D_MODEL), x_sample.reshape(N_LAT_TOK, D_MODEL)], axis=0)
    cond = jnp.concatenate([c_ctx[None, :], c, jnp.zeros((N_COND - 1 - DEC_BATCH, D_MODEL), F32)], axis=0)
    mod = _modulation(cond, w_mod, b_mod).reshape(DEPTH, N_COND, 1, N_MOD * D_MODEL)
    cos_t, sin_t = _rope_tables()
    zero_h0 = jnp.zeros((BATCH, 2, LRU_WIDTH), F32)
    ks, vs, hs = [], [], []
    res = None
    for i in range(DEPTH):
        lam_init = 0.8 - 0.6 * math.exp(-0.3 * i)
        outs = _inproj(x, res, mod[i], norm1[i], w_in[i].astype(BF16), cos_t, sin_t)
        au, av, rx, rg, q, k, kr, v, br = outs[:9]
        if res is not None:
            x = outs[9]
        w_r = _blockdiag128(lru_w_r[i])
        w_i = _blockdiag128(lru_w_i[i])
        lru_args = (lru_conv_w[i], lru_conv_b[i], w_r, lru_b_r[i], w_i, lru_b_i[i], lru_lam[i])
        hs_ctx, hl_ctx = _lru(rx, 0, BATCH, SEQ, *lru_args, zero_h0)
        hs_lat, _ = _lru(rx, N_CTX_TOK, DEC_BATCH, DEC_SEQ, *lru_args, state_lru[:, i])
        o_ctx = _attention(q, kr, v, 0, BATCH, SEQ, None, att_lam[i], att_subln[i], lam_init)
        ctx = (cache_k[:, i].reshape(DEC_BATCH, PAST_LEN, ATT_WIDTH), cache_v[:, i].reshape(DEC_BATCH, PAST_LEN, ATT_WIDTH))
        o_lat = _attention(q, kr, v, N_CTX_TOK, DEC_BATCH, DEC_SEQ, ctx, att_lam[i], att_subln[i], lam_init)
        bias = jnp.repeat(a_bs[i].T, CHUNK, axis=1)
        x = _merge(x, au, av, jnp.concatenate([hs_ctx, hs_lat], axis=0), rg, jnp.concatenate([o_ctx, o_lat], axis=0),
                   br, mod[i], a_norm[i], a_ws[i], bias, w_up_a[i].astype(BF16), w_up_b[i].astype(BF16),
                   w_up_c[i].astype(BF16), w_out[i].astype(BF16))
        h2, idx, gate = _route(x, mod[i], norm2[i], peer_wq[i].astype(BF16), peer_keys[i])
        p = _peer_experts(h2, idx.T, jnp.repeat(gate.T, 2, axis=1), _pack_table(peer_u[i]), _pack_table(peer_v[i]))
        res = (p, mod[i])
        ks.append(k[:N_CTX_TOK].reshape(BATCH, SEQ, ATT_HEADS, ATT_VD))
        vs.append(v[:N_CTX_TOK].reshape(BATCH, SEQ, ATT_HEADS, ATT_VD))
        hs.append(hl_ctx)
    y = _final(x, res[0], res[1], final_norm)
    return (y[:N_CTX_TOK].reshape(BATCH, SEQ, D_MODEL), y[N_CTX_TOK:].reshape(DEC_BATCH, DEC_SEQ, D_MODEL),
            jnp.stack(ks, axis=1), jnp.stack(vs, axis=1), jnp.stack(hs, axis=1))
```

```python
import functools
import math

import jax
import jax.numpy as jnp
from jax import lax
from jax.experimental import pallas as pl
from jax.experimental.pallas import tpu as pltpu

D_MODEL = 1024
BATCH = 16
SEQ = 256
DEPTH = 2
DEC_BATCH = 4
DEC_SEQ = 2048
PAST_LEN = 512
GRID_W = 64
EPS = 1e-6
N_MOD = 6
CHUNK = 128
A_GROUPS = 4
A_WIDTH = 512
LRU_BLOCKS = 8
LRU_BLOCK_W = 64
LRU_WIDTH = 512
CONV_W = 4
LRU_C = 8.0
ATT_HEADS = 4
ATT_HD = 64
ATT_VD = 128
ATT_WIDTH = 512
ROPE_BASE = 10000.0
ROPE_FREQS = 16
N_BRANCH = 3
IN_SPLITS = (512, 1024, 1536, 2048, 2560, 3072, 3584)
IN_WIDTH = 3584 + N_BRANCH * D_MODEL
PEER_HEADS = 8
N_KEYS = 128
N_EXPERTS = N_KEYS * N_KEYS
PEER_QDIM = 256
PEER_HALF = 128
PEER_TOPK = 16
N_PICK = PEER_HEADS * PEER_TOPK

N_CTX_TOK = BATCH * SEQ
N_LAT_TOK = DEC_BATCH * DEC_SEQ
N_TOK = N_CTX_TOK + N_LAT_TOK
TM = 256
N_TILES = N_TOK // TM
CTX_TILES = N_CTX_TOK // TM
LAT_TILES_PER_SEQ = DEC_SEQ // TM
N_COND = 8
TP = 64
TOKEN_UNROLL = 8
ROW_WORDS = D_MODEL // 2
ROW_SUB = ROW_WORDS // 128
PLANE_STRIDE = N_PICK + 8
VMEM_LIMIT = 56 * 1024 * 1024

F32 = jnp.float32
BF16 = jnp.bfloat16
HI = lax.Precision.HIGHEST


def _cond_row(i):
    return jnp.maximum(i - LAT_TILES_PER_SEQ, 0) // LAT_TILES_PER_SEQ


def _pos_block(i):
    return jnp.where(i < CTX_TILES, 0, 1 + i % LAT_TILES_PER_SEQ)


def _mod_spec(chunk):
    return pl.BlockSpec((None, 1, D_MODEL), lambda i: (_cond_row(i), 0, chunk))


def _const_spec(shape):
    nd = len(shape)
    return pl.BlockSpec(shape, lambda *_: (0,) * nd)


def _rms(x, gain):
    return x * lax.rsqrt(jnp.mean(x * x, axis=-1, keepdims=True) + EPS) * gain


def _dot(a, b):
    return jnp.dot(a.astype(BF16), b.astype(BF16), preferred_element_type=F32)


def _dot_nt(a, b):
    return lax.dot_general(a.astype(BF16), b.astype(BF16), (((1,), (1,)), ((), ())), preferred_element_type=F32)


def _mod_kernel(cond_ref, w_ref, b_ref, o_ref):
    cond = cond_ref[...]
    act = cond * jax.nn.sigmoid(cond)
    o_ref[...] = jnp.dot(act, w_ref[...], preferred_element_type=F32, precision=HI) + b_ref[...]


def _modulation(cond, w_mod, b_mod):
    nc = 4
    cw = N_MOD * D_MODEL // nc
    return pl.pallas_call(
        _mod_kernel,
        grid=(DEPTH, nc),
        in_specs=[pl.BlockSpec((N_COND, D_MODEL), lambda l, j: (0, 0)),
                  pl.BlockSpec((None, D_MODEL, cw), lambda l, j: (l, 0, j)),
                  pl.BlockSpec((None, 1, cw), lambda l, j: (l, 0, j))],
        out_specs=pl.BlockSpec((None, N_COND, cw), lambda l, j: (l, 0, j)),
        out_shape=jax.ShapeDtypeStruct((DEPTH, N_COND, N_MOD * D_MODEL), F32),
        compiler_params=pltpu.CompilerParams(vmem_limit_bytes=VMEM_LIMIT),
        name="modulation",
    )(cond, w_mod, b_mod.reshape(DEPTH, 1, N_MOD * D_MODEL))


def _inproj_kernel(has_res, *refs):
    if has_res:
        x_ref, p_ref, g2_ref = refs[:3]
        refs = refs[3:]
        x = x_ref[...] + g2_ref[...] * p_ref[...]
    else:
        x_ref = refs[0]
        refs = refs[1:]
        x = x_ref[...]
    (sh_ref, sc_ref, n1_ref, w_ref, cos_ref, sin_ref,
     au_ref, av_ref, rx_ref, rg_ref, q_ref, k_ref, kr_ref, v_ref, br_ref) = refs[:15]
    if has_res:
        refs[15][...] = x
    h = (_rms(x, n1_ref[...]) * (1.0 + sc_ref[...]) + sh_ref[...]).astype(BF16)

    def proj(lo, hi):
        return jnp.dot(h, w_ref[:, lo:hi], preferred_element_type=F32)

    au_ref[...] = proj(0, IN_SPLITS[0])
    av_ref[...] = proj(IN_SPLITS[0], IN_SPLITS[1])
    rx_ref[...] = proj(IN_SPLITS[1], IN_SPLITS[2])
    rg_ref[...] = proj(IN_SPLITS[2], IN_SPLITS[3])
    q = proj(IN_SPLITS[3], IN_SPLITS[4])
    k = proj(IN_SPLITS[4], IN_SPLITS[5])
    v_ref[...] = proj(IN_SPLITS[5], IN_SPLITS[6])
    for j in range(N_BRANCH):
        lo = IN_SPLITS[6] + j * D_MODEL
        br_ref[:, j * D_MODEL:(j + 1) * D_MODEL] = proj(lo, lo + D_MODEL)
    k_ref[...] = k
    lane = lax.broadcasted_iota(jnp.int32, (TM, ATT_WIDTH), 1)
    first = (lane % (2 * ROPE_FREQS)) < ROPE_FREQS
    cos = cos_ref[...]
    sin = sin_ref[...]

    def rot(t):
        partner = jnp.where(first, pltpu.roll(t, ATT_WIDTH - ROPE_FREQS, 1), pltpu.roll(t, ROPE_FREQS, 1))
        return t * cos + partner * sin

    q_ref[...] = rot(q) * (ATT_HD ** -0.5)
    kr_ref[...] = rot(k)


def _inproj(x, res, mod_l, norm1, w_in, cos_t, sin_t):
    has_res = res is not None
    tok = lambda w: pl.BlockSpec((TM, w), lambda i: (i, 0))
    in_specs = [tok(D_MODEL)]
    args = [x]
    if has_res:
        p, mod_prev = res
        in_specs += [tok(D_MODEL), _mod_spec(5)]
        args += [p, mod_prev]
    in_specs += [_mod_spec(0), _mod_spec(1), _const_spec((1, D_MODEL)),
                 pl.BlockSpec((D_MODEL, IN_WIDTH), lambda i: (0, 0), pipeline_mode=pl.Buffered(1)),
                 pl.BlockSpec((TM, ATT_WIDTH), lambda i: (_pos_block(i), 0)),
                 pl.BlockSpec((TM, ATT_WIDTH), lambda i: (_pos_block(i), 0))]
    args += [mod_l, mod_l, norm1.reshape(1, D_MODEL), w_in, cos_t, sin_t]
    widths = [512] * 8 + [N_BRANCH * D_MODEL]
    out_specs = [tok(w) for w in widths]
    out_shape = [jax.ShapeDtypeStruct((N_TOK, w), F32) for w in widths]
    if has_res:
        out_specs.append(tok(D_MODEL))
        out_shape.append(jax.ShapeDtypeStruct((N_TOK, D_MODEL), F32))
    return pl.pallas_call(
        functools.partial(_inproj_kernel, has_res),
        grid=(N_TILES,),
        in_specs=in_specs,
        out_specs=out_specs,
        out_shape=out_shape,
        compiler_params=pltpu.CompilerParams(vmem_limit_bytes=VMEM_LIMIT),
        name="inproj",
    )(*args)


def _lru_kernel(seq, x_ref, cw_ref, cb_ref, wr_ref, br_ref, wi_ref, bi_ref, lam_ref, h0_ref, hs_ref, hl_ref):
    x = x_ref[...]
    t = lax.broadcasted_iota(jnp.int32, (seq, 128), 0)

    def shifted(v, k, fill):
        r = pltpu.roll(v, k % seq, 0)
        ok = (t >= k) if k > 0 else (t < seq + k)
        return jnp.where(ok, r, fill)

    cw = cw_ref[...]
    xc = (shifted(x, 2, 0.0) * cw[0:1] + shifted(x, 1, 0.0) * cw[1:2] + x * cw[2:3]
          + shifted(x, -1, 0.0) * cw[3:4] + cb_ref[...])

    def direction(d):
        r = jax.nn.sigmoid(jnp.dot(xc, wr_ref[d, 0], preferred_element_type=F32, precision=HI) + br_ref[d:d + 1])
        g = jax.nn.sigmoid(jnp.dot(xc, wi_ref[d, 0], preferred_element_type=F32, precision=HI) + bi_ref[d:d + 1])
        z = -lam_ref[d:d + 1]
        softplus = jnp.maximum(z, 0.0) + jnp.log(1.0 + jnp.exp(-jnp.abs(z)))
        log_a = -LRU_C * r * softplus
        a = jnp.exp(log_a)
        b = jnp.sqrt(1.0 - jnp.exp(2.0 * log_a)) * (g * xc)
        sgn = 1 if d == 0 else -1
        k = 1
        while k < seq:
            a_prev = shifted(a, sgn * k, 1.0)
            b_prev = shifted(b, sgn * k, 0.0)
            b = a * b_prev + b
            a = a * a_prev
            k *= 2
        return a * h0_ref[0, d:d + 1] + b

    hf = direction(0)
    hb = direction(1)
    hs_ref[...] = hf + hb
    hl_ref[0] = jnp.concatenate([hf[seq - 1:seq], hb[0:1]], axis=0)


def _lru(rx, row0, nseq, seq, conv_w, conv_b, w_r, b_r, w_i, b_i, lam, h0):
    nb = LRU_WIDTH // 128
    blk0 = row0 // seq
    par = lambda shape: pl.BlockSpec(shape, lambda b, c: (0,) * (len(shape) - 1) + (c,))
    return pl.pallas_call(
        functools.partial(_lru_kernel, seq),
        grid=(nseq, nb),
        in_specs=[pl.BlockSpec((seq, 128), lambda b, c: (blk0 + b, c)),
                  par((CONV_W, 128)), par((1, 128)),
                  pl.BlockSpec((2, 1, 128, 128), lambda b, c: (0, c, 0, 0)), par((2, 128)),
                  pl.BlockSpec((2, 1, 128, 128), lambda b, c: (0, c, 0, 0)), par((2, 128)),
                  par((2, 128)),
                  pl.BlockSpec((1, 2, 128), lambda b, c: (b, 0, c))],
        out_specs=[pl.BlockSpec((seq, 128), lambda b, c: (b, c)),
                   pl.BlockSpec((1, 2, 128), lambda b, c: (b, 0, c))],
        out_shape=[jax.ShapeDtypeStruct((nseq * seq, LRU_WIDTH), F32),
                   jax.ShapeDtypeStruct((nseq, 2, LRU_WIDTH), F32)],
        compiler_params=pltpu.CompilerParams(vmem_limit_bytes=VMEM_LIMIT),
        name="rglru",
    )(rx, conv_w, conv_b.reshape(1, LRU_WIDTH), w_r, b_r, w_i, b_i, lam, h0)


def _blockdiag128(w):
    w = w.reshape(2, LRU_BLOCKS // 2, 2, LRU_BLOCK_W, LRU_BLOCK_W)
    z = jnp.zeros_like(w[:, :, 0])
    top = jnp.concatenate([w[:, :, 0], z], axis=-1)
    bot = jnp.concatenate([z, w[:, :, 1]], axis=-1)
    return jnp.concatenate([top, bot], axis=-2)


def _attn_kernel(has_ctx, lam_init, *refs):
    if has_ctx:
        q_ref, k_ref, v_ref, kc_ref, vc_ref, lp_ref, g_ref, o_ref = refs
    else:
        q_ref, k_ref, v_ref, lp_ref, g_ref, o_ref = refs
    lp = lp_ref[...]
    lam = (jnp.exp(jnp.sum(lp[0:1] * lp[1:2], axis=-1, keepdims=True))
           - jnp.exp(jnp.sum(lp[2:3] * lp[3:4], axis=-1, keepdims=True)) + lam_init)
    q = q_ref[...]
    lane = lax.broadcasted_iota(jnp.int32, q.shape, 1)
    halves = (jnp.where(lane < ATT_HD, q, 0.0), jnp.where(lane >= ATT_HD, q, 0.0))
    k = k_ref[...]
    v = v_ref[...]
    w = []
    for qh in halves:
        s = _dot_nt(qh, k)
        m = jnp.max(s, axis=-1, keepdims=True)
        if has_ctx:
            sc = _dot_nt(qh, kc_ref[0])
            m = jnp.maximum(m, jnp.max(sc, axis=-1, keepdims=True))
            ec = jnp.exp(sc - m)
        e = jnp.exp(s - m)
        den = jnp.sum(e, axis=-1, keepdims=True)
        if has_ctx:
            den = den + jnp.sum(ec, axis=-1, keepdims=True)
            w.append((e / den, ec / den))
        else:
            w.append((e / den,))
    o = _dot(w[0][0] - lam * w[1][0], v)
    if has_ctx:
        o = o + _dot(w[0][1] - lam * w[1][1], vc_ref[0])
    o_ref[...] = _rms(o, g_ref[...]) * (1.0 - lam_init)


def _attention(q, kr, v, row0, nseq, seq, ctx, att_lam, subln, lam_init):
    has_ctx = ctx is not None
    nq = seq // TM
    blk0 = row0 // seq
    in_specs = [pl.BlockSpec((TM, ATT_VD), lambda b, h, i: ((row0 // TM) + b * nq + i, h)),
                pl.BlockSpec((seq, ATT_VD), lambda b, h, i: (blk0 + b, h)),
                pl.BlockSpec((seq, ATT_VD), lambda b, h, i: (blk0 + b, h))]
    args = [q, kr, v]
    if has_ctx:
        in_specs += [pl.BlockSpec((1, PAST_LEN, ATT_VD), lambda b, h, i: (b, 0, h))] * 2
        args += list(ctx)
    in_specs += [_const_spec((4, ATT_HD)), _const_spec((1, ATT_VD))]
    args += [att_lam, subln.reshape(1, ATT_VD)]
    return pl.pallas_call(
        functools.partial(_attn_kernel, has_ctx, lam_init),
        grid=(nseq, ATT_HEADS, nq),
        in_specs=in_specs,
        out_specs=pl.BlockSpec((TM, ATT_VD), lambda b, h, i: (b * nq + i, h)),
        out_shape=jax.ShapeDtypeStruct((nseq * seq, ATT_WIDTH), F32),
        compiler_params=pltpu.CompilerParams(vmem_limit_bytes=VMEM_LIMIT),
        name="diffattn",
    )(*args)


def _merge_kernel(x_ref, au_ref, av_ref, hs_ref, rg_ref, o_ref, br_ref, g1_ref, an_ref, ws_ref, bs_ref,
                  wa_ref, wb_ref, wc_ref, wo_ref, xo_ref):
    vn = _rms(av_ref[...], an_ref[...])
    rows = []
    for c in range(TM // CHUNK):
        cols = []
        for g in range(A_GROUPS):
            blk = vn[c * CHUNK:(c + 1) * CHUNK, g * 128:(g + 1) * 128]
            cols.append(_dot(ws_ref[g], blk))
        rows.append(jnp.concatenate(cols, axis=1) + bs_ref[...])
    y_a = au_ref[...] * jnp.concatenate(rows, axis=0)
    y_b = hs_ref[...] * jax.nn.gelu(rg_ref[...])
    merged = (jax.nn.sigmoid(br_ref[:, 0:D_MODEL]) * _dot(y_a, wa_ref[...])
              + jax.nn.sigmoid(br_ref[:, D_MODEL:2 * D_MODEL]) * _dot(y_b, wb_ref[...])
              + jax.nn.sigmoid(br_ref[:, 2 * D_MODEL:3 * D_MODEL]) * _dot(o_ref[...], wc_ref[...]))
    xo_ref[...] = x_ref[...] + g1_ref[...] * _dot(merged, wo_ref[...])


def _merge(x, au, av, hs, rg, o, br, mod_l, a_norm, a_ws, bias, wa, wb, wc, wo):
    tok = lambda w: pl.BlockSpec((TM, w), lambda i: (i, 0))
    return pl.pallas_call(
        _merge_kernel,
        grid=(N_TILES,),
        in_specs=[tok(D_MODEL), tok(512), tok(512), tok(512), tok(512), tok(512), tok(N_BRANCH * D_MODEL),
                  _mod_spec(2), _const_spec((1, A_WIDTH)), _const_spec((A_GROUPS, CHUNK, CHUNK)),
                  _const_spec((CHUNK, A_WIDTH)),
                  _const_spec((A_WIDTH, D_MODEL)), _const_spec((LRU_WIDTH, D_MODEL)),
                  _const_spec((ATT_WIDTH, D_MODEL)), _const_spec((D_MODEL, D_MODEL))],
        out_specs=tok(D_MODEL),
        out_shape=jax.ShapeDtypeStruct((N_TOK, D_MODEL), F32),
        compiler_params=pltpu.CompilerParams(vmem_limit_bytes=VMEM_LIMIT),
        name="merge",
    )(x, au, av, hs, rg, o, br, mod_l, a_norm.reshape(1, A_WIDTH), a_ws, bias, wa, wb, wc, wo)


def _top16(s, n):
    pos = lax.broadcasted_iota(jnp.int32, s.shape, 0).astype(F32)
    vals, idxs = [], []
    for _ in range(PEER_TOPK):
        m = jnp.max(s, axis=0, keepdims=True)
        am = jnp.min(jnp.where(s == m, pos, float(n)), axis=0, keepdims=True)
        vals.append(m)
        idxs.append(am)
        s = jnp.where(pos == am, -jnp.inf, s)
    return jnp.concatenate(vals, axis=0), jnp.concatenate(idxs, axis=0)


def _take16(table, sel):
    out = jnp.zeros_like(table)
    for a in range(PEER_TOPK):
        out = jnp.where(sel == float(a), table[a:a + 1], out)
    return out


def _route_kernel(x_ref, sh_ref, sc_ref, n2_ref, wq_ref, keys_ref, h2_ref, idx_ref, gate_ref):
    h2 = _rms(x_ref[...], n2_ref[...]) * (1.0 + sc_ref[...]) + sh_ref[...]
    h2_ref[...] = h2
    q = _dot(h2, wq_ref[...])
    for h in range(PEER_HEADS):
        base = h * PEER_QDIM
        v1, i1 = _top16(_dot_nt(keys_ref[0], q[:, base:base + PEER_HALF]), N_KEYS)
        v2, i2 = _top16(_dot_nt(keys_ref[1], q[:, base + PEER_HALF:base + PEER_QDIM]), N_KEYS)
        cand = jnp.concatenate([v1[a:a + 1] + v2 for a in range(PEER_TOPK)], axis=0)
        top_s, pos = _top16(cand, PEER_TOPK * PEER_TOPK)
        a_sel = jnp.floor(pos * (1.0 / PEER_TOPK))
        b_sel = pos - a_sel * PEER_TOPK
        expert = _take16(i1, a_sel) * N_KEYS + _take16(i2, b_sel)
        e = jnp.exp(top_s - top_s[0:1])
        rows = slice(h * PEER_TOPK, (h + 1) * PEER_TOPK)
        gate_ref[rows, :] = e / jnp.sum(e, axis=0, keepdims=True)
        idx_ref[rows, :] = expert.astype(jnp.int32) * ROW_SUB


def _route(x, mod_l, norm2, wq, keys):
    tok = pl.BlockSpec((TM, D_MODEL), lambda i: (i, 0))
    pick = pl.BlockSpec((N_PICK, TM), lambda i: (0, i))
    return pl.pallas_call(
        _route_kernel,
        grid=(N_TILES,),
        in_specs=[tok, _mod_spec(3), _mod_spec(4), _const_spec((1, D_MODEL)),
                  _const_spec((D_MODEL, PEER_HEADS * PEER_QDIM)), _const_spec((2, N_KEYS, PEER_HALF))],
        out_specs=[tok, pick, pick],
        out_shape=[jax.ShapeDtypeStruct((N_TOK, D_MODEL), F32),
                   jax.ShapeDtypeStruct((N_PICK, N_TOK), jnp.int32),
                   jax.ShapeDtypeStruct((N_PICK, N_TOK), F32)],
        compiler_params=pltpu.CompilerParams(vmem_limit_bytes=VMEM_LIMIT),
        name="peer_route",
    )(x, mod_l, mod_l, norm2.reshape(1, D_MODEL), wq, keys)


def _pack_table(tab):
    bits = lax.bitcast_convert_type(tab.astype(BF16), jnp.uint16).astype(jnp.uint32)
    words = bits[:, :ROW_WORDS] | (bits[:, ROW_WORDS:] << 16)
    return lax.bitcast_convert_type(words, jnp.int32).reshape(N_EXPERTS * ROW_SUB, 128)


def _unpack(words):
    lo = lax.bitcast_convert_type(words << 16, F32)
    hi = lax.bitcast_convert_type(words & jnp.int32(-65536), F32)
    return lo, hi


def _gather_planes(tab_ref, idx_ref, t, g_ref):
    for k in range(N_PICK):
        row = pl.multiple_of(idx_ref[t, k], ROW_SUB)
        g_ref[pl.ds(k, ROW_SUB, stride=PLANE_STRIDE), :] = tab_ref[pl.ds(row, ROW_SUB), :]


def _plane(g_ref, s):
    return g_ref[s * PLANE_STRIDE:s * PLANE_STRIDE + N_PICK, :]


def _split_bf16(v):
    hi = v.astype(BF16)
    return hi, (v - hi.astype(F32)).astype(BF16)


def _pipelined_tokens(tab_ref, idx_ref, bufs, compute):
    _gather_planes(tab_ref, idx_ref, 0, bufs[0])

    def group(j, carry):
        for p in range(TOKEN_UNROLL):
            t = TOKEN_UNROLL * j + p
            compute(t, bufs[p % 2])
            _gather_planes(tab_ref, idx_ref, jnp.minimum(t + 1, TP - 1), bufs[(p + 1) % 2])
        return carry

    lax.fori_loop(0, TP // TOKEN_UNROLL, group, 0)


def _peer_act_kernel(idx_ref, x_ref, gate_ref, tab_ref, o_ref, ga_ref, gb_ref):
    ones = jnp.ones((8, 2 * 128), BF16)

    def token(t, g_ref):
        x = x_ref[t]
        acc = jnp.zeros((N_PICK, 128), F32)
        for s in range(ROW_SUB):
            lo, hi = _unpack(_plane(g_ref, s))
            acc = acc + lo * x[s:s + 1] + hi * x[ROW_SUB + s:ROW_SUB + s + 1]
        act = _dot_nt(ones, jnp.concatenate(_split_bf16(acc), axis=1))
        o_ref[pl.ds(t, 1), :] = jax.nn.gelu(act[0:1]) * gate_ref[pl.ds(t, 1), :]

    _pipelined_tokens(tab_ref, idx_ref, (ga_ref, gb_ref), token)


def _peer_out_kernel(idx_ref, coef_ref, tab_ref, o_ref, ga_ref, gb_ref):
    def token(t, g_ref):
        c_hi, c_lo = _split_bf16(coef_ref[t])
        lhs = jnp.concatenate([c_hi, c_lo, jnp.zeros((4, 2 * N_PICK), BF16)], axis=0)
        lo_rows, hi_rows = [], []
        for s in range(ROW_SUB):
            w = pltpu.bitcast(_plane(g_ref, s), BF16)
            r = jnp.dot(lhs, w, preferred_element_type=F32)
            lo_rows.append(r[0:1] + r[2:3])
            hi_rows.append(r[1:2] + r[3:4])
        o_ref[t] = jnp.concatenate(lo_rows + hi_rows, axis=0)

    _pipelined_tokens(tab_ref, idx_ref, (ga_ref, gb_ref), token)


def _peer_experts(h2, idx, gate, u_words, v_words):
    smem_idx = pl.BlockSpec((TP, N_PICK), lambda i: (i, 0), memory_space=pltpu.SMEM)
    rows = pl.BlockSpec((TP, 8, 128), lambda i: (i, 0, 0))
    pick = pl.BlockSpec((TP, N_PICK), lambda i: (i, 0))
    pair = pl.BlockSpec((TP, 2, 2 * N_PICK), lambda i: (i, 0, 0))
    table = pl.BlockSpec((N_EXPERTS * ROW_SUB, 128), lambda i: (0, 0), pipeline_mode=pl.Buffered(1))
    params = pltpu.CompilerParams(vmem_limit_bytes=VMEM_LIMIT)
    planes = pltpu.VMEM((ROW_SUB * PLANE_STRIDE, 128), jnp.int32)
    coef = pl.pallas_call(
        _peer_act_kernel,
        grid=(N_TOK // TP,),
        in_specs=[smem_idx, rows, pick, table],
        out_specs=pick,
        out_shape=jax.ShapeDtypeStruct((N_TOK, N_PICK), F32),
        scratch_shapes=[planes, planes],
        compiler_params=params,
        name="peer_act",
    )(idx, h2.reshape(N_TOK, 8, 128), gate, u_words)
    zero = jnp.zeros_like(coef)
    coef = jnp.stack([jnp.stack([coef, zero], axis=-1), jnp.stack([zero, coef], axis=-1)], axis=1)
    coef = coef.reshape(N_TOK, 2, 2 * N_PICK)
    out = pl.pallas_call(
        _peer_out_kernel,
        grid=(N_TOK // TP,),
        in_specs=[smem_idx, pair, table],
        out_specs=rows,
        out_shape=jax.ShapeDtypeStruct((N_TOK, 8, 128), F32),
        scratch_shapes=[planes, planes],
        compiler_params=params,
        name="peer_out",
    )(idx, coef, v_words)
    return out.reshape(N_TOK, D_MODEL)


def _final_kernel(x_ref, p_ref, g2_ref, n_ref, o_ref):
    o_ref[...] = _rms(x_ref[...] + g2_ref[...] * p_ref[...], n_ref[...])


def _final(x, p, mod_l, final_norm):
    tok = pl.BlockSpec((TM, D_MODEL), lambda i: (i, 0))
    return pl.pallas_call(
        _final_kernel,
        grid=(N_TILES,),
        in_specs=[tok, tok, _mod_spec(5), _const_spec((1, D_MODEL))],
        out_specs=tok,
        out_shape=jax.ShapeDtypeStruct((N_TOK, D_MODEL), F32),
        name="final_norm",
    )(x, p, mod_l, final_norm.reshape(1, D_MODEL))


def _rope_tables():
    rows = DEC_SEQ // GRID_W
    row_ids = jnp.repeat(jnp.arange(rows), GRID_W).astype(F32)
    col_ids = jnp.tile(jnp.arange(GRID_W), rows).astype(F32)
    inv_freq = ROPE_BASE ** (-jnp.arange(ROPE_FREQS, dtype=F32) / ROPE_FREQS)
    ang_r = row_ids[:, None] * inv_freq
    ang_c = col_ids[:, None] * inv_freq
    cos = jnp.concatenate([jnp.cos(ang_r), jnp.cos(ang_r), jnp.cos(ang_c), jnp.cos(ang_c)], axis=1)
    sin = jnp.concatenate([-jnp.sin(ang_r), jnp.sin(ang_r), -jnp.sin(ang_c), jnp.sin(ang_c)], axis=1)
    reps = ATT_WIDTH // ATT_HD
    cos = jnp.concatenate([jnp.ones((TM, ATT_WIDTH), F32), jnp.tile(cos, (1, reps))], axis=0)
    sin = jnp.concatenate([jnp.zeros((TM, ATT_WIDTH), F32), jnp.tile(sin, (1, reps))], axis=0)
    return cos, sin


def kernel(x_prompt, x_sample, cache_k, cache_v, state_lru, c, c_ctx, w_mod, b_mod, norm1, norm2, w_in, a_norm, a_ws, a_bs, lru_conv_w, lru_conv_b, lru_w_r, lru_b_r, lru_w_i, lru_b_i, lru_lam, att_lam, att_subln, w_up_a, w_up_b, w_up_c, w_out, peer_wq, peer_keys, peer_u, peer_v, final_norm):
    x = jnp.concatenate([x_prompt.reshape(N_CTX_TOK, D_MODEL), x_sample.reshape(N_LAT_TOK, D_MODEL)], axis=0)
    cond = jnp.concatenate([c_ctx[None, :], c, jnp.zeros((N_COND - 1 - DEC_BATCH, D_MODEL), F32)], axis=0)
    mod = _modulation(cond, w_mod, b_mod).reshape(DEPTH, N_COND, 1, N_MOD * D_MODEL)
    cos_t, sin_t = _rope_tables()
    zero_h0 = jnp.zeros((BATCH, 2, LRU_WIDTH), F32)
    ks, vs, hs = [], [], []
    res = None
    for i in range(DEPTH):
        lam_init = 0.8 - 0.6 * math.exp(-0.3 * i)
        outs = _inproj(x, res, mod[i], norm1[i], w_in[i].astype(BF16), cos_t, sin_t)
        au, av, rx, rg, q, k, kr, v, br = outs[:9]
        if res is not None:
            x = outs[9]
        w_r = _blockdiag128(lru_w_r[i])
        w_i = _blockdiag128(lru_w_i[i])
        lru_args = (lru_conv_w[i], lru_conv_b[i], w_r, lru_b_r[i], w_i, lru_b_i[i], lru_lam[i])
        hs_ctx, hl_ctx = _lru(rx, 0, BATCH, SEQ, *lru_args, zero_h0)
        hs_lat, _ = _lru(rx, N_CTX_TOK, DEC_BATCH, DEC_SEQ, *lru_args, state_lru[:, i])
        o_ctx = _attention(q, kr, v, 0, BATCH, SEQ, None, att_lam[i], att_subln[i], lam_init)
        ctx = (cache_k[:, i].reshape(DEC_BATCH, PAST_LEN, ATT_WIDTH), cache_v[:, i].reshape(DEC_BATCH, PAST_LEN, ATT_WIDTH))
        o_lat = _attention(q, kr, v, N_CTX_TOK, DEC_BATCH, DEC_SEQ, ctx, att_lam[i], att_subln[i], lam_init)
        bias = jnp.repeat(a_bs[i].T, CHUNK, axis=1)
        x = _merge(x, au, av, jnp.concatenate([hs_ctx, hs_lat], axis=0), rg, jnp.concatenate([o_ctx, o_lat], axis=0),
                   br, mod[i], a_norm[i], a_ws[i], bias, w_up_a[i].astype(BF16), w_up_b[i].astype(BF16),
                   w_up_c[i].astype(BF16), w_out[i].astype(BF16))
        h2, idx, gate = _route(x, mod[i], norm2[i], peer_wq[i].astype(BF16), peer_keys[i])
        p = _peer_experts(h2, idx.T, gate.T, _pack_table(peer_u[i]), _pack_table(peer_v[i]))
        res = (p, mod[i])
        ks.append(k[:N_CTX_TOK].reshape(BATCH, SEQ, ATT_HEADS, ATT_VD))
        vs.append(v[:N_CTX_TOK].reshape(BATCH, SEQ, ATT_HEADS, ATT_VD))
        hs.append(hl_ctx)
    y = _final(x, res[0], res[1], final_norm)
    return (y[:N_CTX_TOK].reshape(BATCH, SEQ, D_MODEL), y[N_CTX_TOK:].reshape(DEC_BATCH, DEC_SEQ, D_MODEL),
            jnp.stack(ks, axis=1), jnp.stack(vs, axis=1), jnp.stack(hs, axis=1))
```

```python
import functools
import math

import jax
import jax.numpy as jnp
from jax import lax
from jax.experimental import pallas as pl
from jax.experimental.pallas import tpu as pltpu

D_MODEL = 1024
BATCH = 16
SEQ = 256
DEPTH = 2
DEC_BATCH = 4
DEC_SEQ = 2048
PAST_LEN = 512
GRID_W = 64
EPS = 1e-6
N_MOD = 6
CHUNK = 128
A_GROUPS = 4
A_WIDTH = 512
LRU_BLOCKS = 8
LRU_BLOCK_W = 64
LRU_WIDTH = 512
CONV_W = 4
LRU_C = 8.0
ATT_HEADS = 4
ATT_HD = 64
ATT_VD = 128
ATT_WIDTH = 512
ROPE_BASE = 10000.0
ROPE_FREQS = 16
N_BRANCH = 3
IN_SPLITS = (512, 1024, 1536, 2048, 2560, 3072, 3584)
IN_WIDTH = 3584 + N_BRANCH * D_MODEL
PEER_HEADS = 8
N_KEYS = 128
N_EXPERTS = N_KEYS * N_KEYS
PEER_QDIM = 256
PEER_HALF = 128
PEER_TOPK = 16
N_PICK = PEER_HEADS * PEER_TOPK
CAND_COLS = tuple(PEER_TOPK // (a + 1) for a in range(PEER_TOPK))
CAND_START = tuple(sum(CAND_COLS[:a]) for a in range(PEER_TOPK))
N_CAND = sum(CAND_COLS)
CAND_PAD = -N_CAND % 8

N_CTX_TOK = BATCH * SEQ
N_LAT_TOK = DEC_BATCH * DEC_SEQ
N_TOK = N_CTX_TOK + N_LAT_TOK
TM = 256
N_TILES = N_TOK // TM
CTX_TILES = N_CTX_TOK // TM
LAT_TILES_PER_SEQ = DEC_SEQ // TM
N_COND = 8
TP = 64
TOKEN_UNROLL = 16
PACK_ROWS = 256
ROW_WORDS = D_MODEL // 2
ROW_SUB = ROW_WORDS // 128
PLANE_STRIDE = N_PICK + 8
VMEM_LIMIT = 56 * 1024 * 1024

F32 = jnp.float32
BF16 = jnp.bfloat16
HI = lax.Precision.HIGHEST


def _cond_row(i):
    return jnp.maximum(i - LAT_TILES_PER_SEQ, 0) // LAT_TILES_PER_SEQ


def _pos_block(i):
    return jnp.where(i < CTX_TILES, 0, 1 + i % LAT_TILES_PER_SEQ)


def _mod_spec(chunk):
    return pl.BlockSpec((None, 1, D_MODEL), lambda i: (_cond_row(i), 0, chunk))


def _const_spec(shape):
    nd = len(shape)
    return pl.BlockSpec(shape, lambda *_: (0,) * nd)


def _rms(x, gain):
    return x * lax.rsqrt(jnp.mean(x * x, axis=-1, keepdims=True) + EPS) * gain


def _dot(a, b):
    return jnp.dot(a.astype(BF16), b.astype(BF16), preferred_element_type=F32)


def _dot_nt(a, b):
    return lax.dot_general(a.astype(BF16), b.astype(BF16), (((1,), (1,)), ((), ())), preferred_element_type=F32)


def _mod_kernel(cond_ref, w_ref, b_ref, o_ref):
    cond = cond_ref[...]
    act = cond * jax.nn.sigmoid(cond)
    o_ref[...] = jnp.dot(act, w_ref[...], preferred_element_type=F32, precision=HI) + b_ref[...]


def _modulation(cond, w_mod, b_mod):
    nc = 4
    cw = N_MOD * D_MODEL // nc
    return pl.pallas_call(
        _mod_kernel,
        grid=(DEPTH, nc),
        in_specs=[pl.BlockSpec((N_COND, D_MODEL), lambda l, j: (0, 0)),
                  pl.BlockSpec((None, D_MODEL, cw), lambda l, j: (l, 0, j)),
                  pl.BlockSpec((None, 1, cw), lambda l, j: (l, 0, j))],
        out_specs=pl.BlockSpec((None, N_COND, cw), lambda l, j: (l, 0, j)),
        out_shape=jax.ShapeDtypeStruct((DEPTH, N_COND, N_MOD * D_MODEL), F32),
        compiler_params=pltpu.CompilerParams(vmem_limit_bytes=VMEM_LIMIT),
        name="modulation",
    )(cond, w_mod, b_mod.reshape(DEPTH, 1, N_MOD * D_MODEL))


def _inproj_kernel(has_res, *refs):
    if has_res:
        x_ref, p_ref, g2_ref = refs[:3]
        refs = refs[3:]
        x = x_ref[...] + g2_ref[...] * p_ref[...]
    else:
        x_ref = refs[0]
        refs = refs[1:]
        x = x_ref[...]
    (sh_ref, sc_ref, n1_ref, w_ref, cos_ref, sin_ref,
     au_ref, av_ref, rx_ref, rg_ref, q_ref, k_ref, kr_ref, v_ref, br_ref) = refs[:15]
    if has_res:
        refs[15][...] = x
    h = (_rms(x, n1_ref[...]) * (1.0 + sc_ref[...]) + sh_ref[...]).astype(BF16)

    def proj(lo, hi):
        return jnp.dot(h, w_ref[:, lo:hi], preferred_element_type=F32)

    au_ref[...] = proj(0, IN_SPLITS[0])
    av_ref[...] = proj(IN_SPLITS[0], IN_SPLITS[1])
    rx_ref[...] = proj(IN_SPLITS[1], IN_SPLITS[2])
    rg_ref[...] = proj(IN_SPLITS[2], IN_SPLITS[3])
    q = proj(IN_SPLITS[3], IN_SPLITS[4])
    k = proj(IN_SPLITS[4], IN_SPLITS[5])
    v_ref[...] = proj(IN_SPLITS[5], IN_SPLITS[6])
    for j in range(N_BRANCH):
        lo = IN_SPLITS[6] + j * D_MODEL
        br_ref[:, j * D_MODEL:(j + 1) * D_MODEL] = proj(lo, lo + D_MODEL)
    k_ref[...] = k
    lane = lax.broadcasted_iota(jnp.int32, (TM, ATT_WIDTH), 1)
    first = (lane % (2 * ROPE_FREQS)) < ROPE_FREQS
    cos = cos_ref[...]
    sin = sin_ref[...]

    def rot(t):
        partner = jnp.where(first, pltpu.roll(t, ATT_WIDTH - ROPE_FREQS, 1), pltpu.roll(t, ROPE_FREQS, 1))
        return t * cos + partner * sin

    q_ref[...] = rot(q) * (ATT_HD ** -0.5)
    kr_ref[...] = rot(k)


def _inproj(x, res, mod_l, norm1, w_in, cos_t, sin_t):
    has_res = res is not None
    tok = lambda w: pl.BlockSpec((TM, w), lambda i: (i, 0))
    in_specs = [tok(D_MODEL)]
    args = [x]
    if has_res:
        p, mod_prev = res
        in_specs += [tok(D_MODEL), _mod_spec(5)]
        args += [p, mod_prev]
    in_specs += [_mod_spec(0), _mod_spec(1), _const_spec((1, D_MODEL)),
                 pl.BlockSpec((D_MODEL, IN_WIDTH), lambda i: (0, 0), pipeline_mode=pl.Buffered(1)),
                 pl.BlockSpec((TM, ATT_WIDTH), lambda i: (_pos_block(i), 0)),
                 pl.BlockSpec((TM, ATT_WIDTH), lambda i: (_pos_block(i), 0))]
    args += [mod_l, mod_l, norm1.reshape(1, D_MODEL), w_in, cos_t, sin_t]
    widths = [512] * 8 + [N_BRANCH * D_MODEL]
    out_specs = [tok(w) for w in widths]
    out_shape = [jax.ShapeDtypeStruct((N_TOK, w), F32) for w in widths]
    if has_res:
        out_specs.append(tok(D_MODEL))
        out_shape.append(jax.ShapeDtypeStruct((N_TOK, D_MODEL), F32))
    return pl.pallas_call(
        functools.partial(_inproj_kernel, has_res),
        grid=(N_TILES,),
        in_specs=in_specs,
        out_specs=out_specs,
        out_shape=out_shape,
        compiler_params=pltpu.CompilerParams(vmem_limit_bytes=VMEM_LIMIT),
        name="inproj",
    )(*args)


def _lru_kernel(seq, x_ref, cw_ref, cb_ref, wr_ref, br_ref, wi_ref, bi_ref, lam_ref, h0_ref, hs_ref, hl_ref):
    x = x_ref[...]
    t = lax.broadcasted_iota(jnp.int32, (seq, 128), 0)

    def shifted(v, k, fill):
        r = pltpu.roll(v, k % seq, 0)
        ok = (t >= k) if k > 0 else (t < seq + k)
        return jnp.where(ok, r, fill)

    cw = cw_ref[...]
    xc = (shifted(x, 2, 0.0) * cw[0:1] + shifted(x, 1, 0.0) * cw[1:2] + x * cw[2:3]
          + shifted(x, -1, 0.0) * cw[3:4] + cb_ref[...])

    def direction(d):
        r = jax.nn.sigmoid(jnp.dot(xc, wr_ref[d, 0], preferred_element_type=F32, precision=HI) + br_ref[d:d + 1])
        g = jax.nn.sigmoid(jnp.dot(xc, wi_ref[d, 0], preferred_element_type=F32, precision=HI) + bi_ref[d:d + 1])
        z = -lam_ref[d:d + 1]
        softplus = jnp.maximum(z, 0.0) + jnp.log(1.0 + jnp.exp(-jnp.abs(z)))
        log_a = -LRU_C * r * softplus
        a = jnp.exp(log_a)
        b = jnp.sqrt(1.0 - jnp.exp(2.0 * log_a)) * (g * xc)
        sgn = 1 if d == 0 else -1
        k = 1
        while k < seq:
            a_prev = shifted(a, sgn * k, 1.0)
            b_prev = shifted(b, sgn * k, 0.0)
            b = a * b_prev + b
            a = a * a_prev
            k *= 2
        return a * h0_ref[0, d:d + 1] + b

    hf = direction(0)
    hb = direction(1)
    hs_ref[...] = hf + hb
    hl_ref[0] = jnp.concatenate([hf[seq - 1:seq], hb[0:1]], axis=0)


def _lru(rx, row0, nseq, seq, conv_w, conv_b, w_r, b_r, w_i, b_i, lam, h0):
    nb = LRU_WIDTH // 128
    blk0 = row0 // seq
    par = lambda shape: pl.BlockSpec(shape, lambda b, c: (0,) * (len(shape) - 1) + (c,))
    return pl.pallas_call(
        functools.partial(_lru_kernel, seq),
        grid=(nseq, nb),
        in_specs=[pl.BlockSpec((seq, 128), lambda b, c: (blk0 + b, c)),
                  par((CONV_W, 128)), par((1, 128)),
                  pl.BlockSpec((2, 1, 128, 128), lambda b, c: (0, c, 0, 0)), par((2, 128)),
                  pl.BlockSpec((2, 1, 128, 128), lambda b, c: (0, c, 0, 0)), par((2, 128)),
                  par((2, 128)),
                  pl.BlockSpec((1, 2, 128), lambda b, c: (b, 0, c))],
        out_specs=[pl.BlockSpec((seq, 128), lambda b, c: (b, c)),
                   pl.BlockSpec((1, 2, 128), lambda b, c: (b, 0, c))],
        out_shape=[jax.ShapeDtypeStruct((nseq * seq, LRU_WIDTH), F32),
                   jax.ShapeDtypeStruct((nseq, 2, LRU_WIDTH), F32)],
        compiler_params=pltpu.CompilerParams(vmem_limit_bytes=VMEM_LIMIT),
        name="rglru",
    )(rx, conv_w, conv_b.reshape(1, LRU_WIDTH), w_r, b_r, w_i, b_i, lam, h0)


def _blockdiag128(w):
    w = w.reshape(2, LRU_BLOCKS // 2, 2, LRU_BLOCK_W, LRU_BLOCK_W)
    z = jnp.zeros_like(w[:, :, 0])
    top = jnp.concatenate([w[:, :, 0], z], axis=-1)
    bot = jnp.concatenate([z, w[:, :, 1]], axis=-1)
    return jnp.concatenate([top, bot], axis=-2)


def _attn_kernel(has_ctx, lam_init, *refs):
    if has_ctx:
        q_ref, k_ref, v_ref, kc_ref, vc_ref, lp_ref, g_ref, o_ref = refs
    else:
        q_ref, k_ref, v_ref, lp_ref, g_ref, o_ref = refs
    lp = lp_ref[...]
    lam = (jnp.exp(jnp.sum(lp[0:1] * lp[1:2], axis=-1, keepdims=True))
           - jnp.exp(jnp.sum(lp[2:3] * lp[3:4], axis=-1, keepdims=True)) + lam_init)
    q = q_ref[...]
    lane = lax.broadcasted_iota(jnp.int32, q.shape, 1)
    halves = (jnp.where(lane < ATT_HD, q, 0.0), jnp.where(lane >= ATT_HD, q, 0.0))
    k = k_ref[...]
    v = v_ref[...]
    w = []
    for qh in halves:
        s = _dot_nt(qh, k)
        m = jnp.max(s, axis=-1, keepdims=True)
        if has_ctx:
            sc = _dot_nt(qh, kc_ref[0])
            m = jnp.maximum(m, jnp.max(sc, axis=-1, keepdims=True))
            ec = jnp.exp(sc - m)
        e = jnp.exp(s - m)
        den = jnp.sum(e, axis=-1, keepdims=True)
        if has_ctx:
            den = den + jnp.sum(ec, axis=-1, keepdims=True)
            w.append((e / den, ec / den))
        else:
            w.append((e / den,))
    o = _dot(w[0][0] - lam * w[1][0], v)
    if has_ctx:
        o = o + _dot(w[0][1] - lam * w[1][1], vc_ref[0])
    o_ref[...] = _rms(o, g_ref[...]) * (1.0 - lam_init)


def _attention(q, kr, v, row0, nseq, seq, ctx, att_lam, subln, lam_init):
    has_ctx = ctx is not None
    nq = seq // TM
    blk0 = row0 // seq
    in_specs = [pl.BlockSpec((TM, ATT_VD), lambda b, h, i: ((row0 // TM) + b * nq + i, h)),
                pl.BlockSpec((seq, ATT_VD), lambda b, h, i: (blk0 + b, h)),
                pl.BlockSpec((seq, ATT_VD), lambda b, h, i: (blk0 + b, h))]
    args = [q, kr, v]
    if has_ctx:
        in_specs += [pl.BlockSpec((1, PAST_LEN, ATT_VD), lambda b, h, i: (b, 0, h))] * 2
        args += list(ctx)
    in_specs += [_const_spec((4, ATT_HD)), _const_spec((1, ATT_VD))]
    args += [att_lam, subln.reshape(1, ATT_VD)]
    return pl.pallas_call(
        functools.partial(_attn_kernel, has_ctx, lam_init),
        grid=(nseq, ATT_HEADS, nq),
        in_specs=in_specs,
        out_specs=pl.BlockSpec((TM, ATT_VD), lambda b, h, i: (b * nq + i, h)),
        out_shape=jax.ShapeDtypeStruct((nseq * seq, ATT_WIDTH), F32),
        compiler_params=pltpu.CompilerParams(vmem_limit_bytes=VMEM_LIMIT),
        name="diffattn",
    )(*args)


def _merge_kernel(x_ref, au_ref, av_ref, hs_ref, rg_ref, o_ref, br_ref, g1_ref, an_ref, ws_ref, bs_ref,
                  wa_ref, wb_ref, wc_ref, wo_ref, xo_ref):
    vn = _rms(av_ref[...], an_ref[...])
    rows = []
    for c in range(TM // CHUNK):
        cols = []
        for g in range(A_GROUPS):
            blk = vn[c * CHUNK:(c + 1) * CHUNK, g * 128:(g + 1) * 128]
            cols.append(_dot(ws_ref[g], blk))
        rows.append(jnp.concatenate(cols, axis=1) + bs_ref[...])
    y_a = au_ref[...] * jnp.concatenate(rows, axis=0)
    y_b = hs_ref[...] * jax.nn.gelu(rg_ref[...])
    merged = (jax.nn.sigmoid(br_ref[:, 0:D_MODEL]) * _dot(y_a, wa_ref[...])
              + jax.nn.sigmoid(br_ref[:, D_MODEL:2 * D_MODEL]) * _dot(y_b, wb_ref[...])
              + jax.nn.sigmoid(br_ref[:, 2 * D_MODEL:3 * D_MODEL]) * _dot(o_ref[...], wc_ref[...]))
    xo_ref[...] = x_ref[...] + g1_ref[...] * _dot(merged, wo_ref[...])


def _merge(x, au, av, hs, rg, o, br, mod_l, a_norm, a_ws, bias, wa, wb, wc, wo):
    tok = lambda w: pl.BlockSpec((TM, w), lambda i: (i, 0))
    return pl.pallas_call(
        _merge_kernel,
        grid=(N_TILES,),
        in_specs=[tok(D_MODEL), tok(512), tok(512), tok(512), tok(512), tok(512), tok(N_BRANCH * D_MODEL),
                  _mod_spec(2), _const_spec((1, A_WIDTH)), _const_spec((A_GROUPS, CHUNK, CHUNK)),
                  _const_spec((CHUNK, A_WIDTH)),
                  _const_spec((A_WIDTH, D_MODEL)), _const_spec((LRU_WIDTH, D_MODEL)),
                  _const_spec((ATT_WIDTH, D_MODEL)), _const_spec((D_MODEL, D_MODEL))],
        out_specs=tok(D_MODEL),
        out_shape=jax.ShapeDtypeStruct((N_TOK, D_MODEL), F32),
        compiler_params=pltpu.CompilerParams(vmem_limit_bytes=VMEM_LIMIT),
        name="merge",
    )(x, au, av, hs, rg, o, br, mod_l, a_norm.reshape(1, A_WIDTH), a_ws, bias, wa, wb, wc, wo)


def _top16(s, n):
    pos = lax.broadcasted_iota(jnp.int32, s.shape, 0).astype(F32)
    vals, idxs = [], []
    for _ in range(PEER_TOPK):
        m = jnp.max(s, axis=0, keepdims=True)
        am = jnp.min(jnp.where(s == m, pos, float(n)), axis=0, keepdims=True)
        vals.append(m)
        idxs.append(am)
        s = jnp.where(pos == am, -jnp.inf, s)
    return jnp.concatenate(vals, axis=0), jnp.concatenate(idxs, axis=0)


def _take16(table, sel):
    out = jnp.zeros_like(table)
    for a in range(PEER_TOPK):
        out = jnp.where(sel == float(a), table[a:a + 1], out)
    return out


def _route_kernel(x_ref, sh_ref, sc_ref, n2_ref, wq_ref, keys_ref, h2_ref, idx_ref, gate_ref):
    h2 = _rms(x_ref[...], n2_ref[...]) * (1.0 + sc_ref[...]) + sh_ref[...]
    h2_ref[...] = h2
    q = _dot(h2, wq_ref[...])
    for h in range(PEER_HEADS):
        base = h * PEER_QDIM
        v1, i1 = _top16(_dot_nt(keys_ref[0], q[:, base:base + PEER_HALF]), N_KEYS)
        v2, i2 = _top16(_dot_nt(keys_ref[1], q[:, base + PEER_HALF:base + PEER_QDIM]), N_KEYS)
        cand = jnp.concatenate([v1[a:a + 1] + v2[0:n] for a, n in enumerate(CAND_COLS)]
                               + [jnp.full((CAND_PAD, TM), -jnp.inf, F32)], axis=0)
        top_s, pos = _top16(cand, N_CAND + CAND_PAD)
        a_sel = jnp.zeros_like(pos)
        b_sel = pos
        for a in range(1, PEER_TOPK):
            later = pos >= float(CAND_START[a])
            a_sel = a_sel + jnp.where(later, 1.0, 0.0)
            b_sel = b_sel - jnp.where(later, float(CAND_COLS[a - 1]), 0.0)
        expert = _take16(i1, a_sel) * N_KEYS + _take16(i2, b_sel)
        e = jnp.exp(top_s - top_s[0:1])
        rows = slice(h * PEER_TOPK, (h + 1) * PEER_TOPK)
        gate_ref[rows, :] = e / jnp.sum(e, axis=0, keepdims=True)
        idx_ref[rows, :] = expert.astype(jnp.int32) * ROW_SUB


def _route(x, mod_l, norm2, wq, keys):
    tok = pl.BlockSpec((TM, D_MODEL), lambda i: (i, 0))
    pick = pl.BlockSpec((N_PICK, TM), lambda i: (0, i))
    return pl.pallas_call(
        _route_kernel,
        grid=(N_TILES,),
        in_specs=[tok, _mod_spec(3), _mod_spec(4), _const_spec((1, D_MODEL)),
                  _const_spec((D_MODEL, PEER_HEADS * PEER_QDIM)), _const_spec((2, N_KEYS, PEER_HALF))],
        out_specs=[tok, pick, pick],
        out_shape=[jax.ShapeDtypeStruct((N_TOK, D_MODEL), F32),
                   jax.ShapeDtypeStruct((N_PICK, N_TOK), jnp.int32),
                   jax.ShapeDtypeStruct((N_PICK, N_TOK), F32)],
        compiler_params=pltpu.CompilerParams(vmem_limit_bytes=VMEM_LIMIT),
        name="peer_route",
    )(x, mod_l, mod_l, norm2.reshape(1, D_MODEL), wq, keys)


def _pack_kernel(t_ref, o_ref):
    def rounded_bits(v):
        return lax.bitcast_convert_type(v.astype(BF16).astype(F32), jnp.int32)

    lo = lax.shift_right_logical(rounded_bits(t_ref[:, 0:ROW_WORDS]), 16)
    hi = rounded_bits(t_ref[:, ROW_WORDS:D_MODEL]) & jnp.int32(-65536)
    words = lo | hi
    for g in range(PACK_ROWS // 8):
        for s in range(ROW_SUB):
            o_ref[pl.ds(8 * g * ROW_SUB + s, 8, stride=ROW_SUB), :] = words[8 * g:8 * g + 8, s * 128:(s + 1) * 128]


def _pack_table(tab):
    return pl.pallas_call(
        _pack_kernel,
        grid=(N_EXPERTS // PACK_ROWS,),
        in_specs=[pl.BlockSpec((PACK_ROWS, D_MODEL), lambda i: (i, 0))],
        out_specs=pl.BlockSpec((PACK_ROWS * ROW_SUB, 128), lambda i: (i, 0)),
        out_shape=jax.ShapeDtypeStruct((N_EXPERTS * ROW_SUB, 128), jnp.int32),
        name="pack_table",
    )(tab)


def _unpack(words):
    lo = lax.bitcast_convert_type(words << 16, F32)
    hi = lax.bitcast_convert_type(words & jnp.int32(-65536), F32)
    return lo, hi


def _gather_planes(tab_ref, idx_ref, t, g_ref):
    for k in range(N_PICK):
        row = pl.multiple_of(idx_ref[t, k], ROW_SUB)
        g_ref[pl.ds(k, ROW_SUB, stride=PLANE_STRIDE), :] = tab_ref[pl.ds(row, ROW_SUB), :]


def _plane(g_ref, s):
    return g_ref[s * PLANE_STRIDE:s * PLANE_STRIDE + N_PICK, :]


def _split_bf16(v):
    hi = v.astype(BF16)
    return hi, (v - hi.astype(F32)).astype(BF16)


def _pipelined_tokens(tab_ref, idx_ref, bufs, compute):
    _gather_planes(tab_ref, idx_ref, 0, bufs[0])

    def group(j, carry):
        for p in range(TOKEN_UNROLL):
            t = TOKEN_UNROLL * j + p
            compute(t, bufs[p % 2])
            _gather_planes(tab_ref, idx_ref, jnp.minimum(t + 1, TP - 1), bufs[(p + 1) % 2])
        return carry

    lax.fori_loop(0, TP // TOKEN_UNROLL, group, 0)


def _peer_act_kernel(idx_ref, x_ref, gate_ref, tab_ref, o_ref, ga_ref, gb_ref):
    ones = jnp.ones((8, 2 * 128), BF16)

    def token(t, g_ref):
        x = x_ref[t]
        acc = jnp.zeros((N_PICK, 128), F32)
        for s in range(ROW_SUB):
            lo, hi = _unpack(_plane(g_ref, s))
            acc = acc + lo * x[s:s + 1] + hi * x[ROW_SUB + s:ROW_SUB + s + 1]
        act = _dot_nt(ones, jnp.concatenate(_split_bf16(acc), axis=1))
        o_ref[pl.ds(t, 1), :] = jax.nn.gelu(act[0:1]) * gate_ref[pl.ds(t, 1), :]

    _pipelined_tokens(tab_ref, idx_ref, (ga_ref, gb_ref), token)


def _peer_out_kernel(idx_ref, coef_ref, tab_ref, o_ref, ga_ref, gb_ref):
    def token(t, g_ref):
        c_hi, c_lo = _split_bf16(coef_ref[t])
        lhs = jnp.concatenate([c_hi, c_lo, jnp.zeros((4, 2 * N_PICK), BF16)], axis=0)
        lo_rows, hi_rows = [], []
        for s in range(ROW_SUB):
            w = pltpu.bitcast(_plane(g_ref, s), BF16)
            r = jnp.dot(lhs, w, preferred_element_type=F32)
            lo_rows.append(r[0:1] + r[2:3])
            hi_rows.append(r[1:2] + r[3:4])
        o_ref[t] = jnp.concatenate(lo_rows + hi_rows, axis=0)

    _pipelined_tokens(tab_ref, idx_ref, (ga_ref, gb_ref), token)


def _peer_experts(h2, idx, gate, u_words, v_words):
    smem_idx = pl.BlockSpec((TP, N_PICK), lambda i: (i, 0), memory_space=pltpu.SMEM)
    rows = pl.BlockSpec((TP, 8, 128), lambda i: (i, 0, 0))
    pick = pl.BlockSpec((TP, N_PICK), lambda i: (i, 0))
    pair = pl.BlockSpec((TP, 2, 2 * N_PICK), lambda i: (i, 0, 0))
    table = pl.BlockSpec((N_EXPERTS * ROW_SUB, 128), lambda i: (0, 0), pipeline_mode=pl.Buffered(1))
    params = pltpu.CompilerParams(vmem_limit_bytes=VMEM_LIMIT)
    planes = pltpu.VMEM((ROW_SUB * PLANE_STRIDE, 128), jnp.int32)
    coef = pl.pallas_call(
        _peer_act_kernel,
        grid=(N_TOK // TP,),
        in_specs=[smem_idx, rows, pick, table],
        out_specs=pick,
        out_shape=jax.ShapeDtypeStruct((N_TOK, N_PICK), F32),
        scratch_shapes=[planes, planes],
        compiler_params=params,
        name="peer_act",
    )(idx, h2.reshape(N_TOK, 8, 128), gate, u_words)
    zero = jnp.zeros_like(coef)
    coef = jnp.stack([jnp.stack([coef, zero], axis=-1), jnp.stack([zero, coef], axis=-1)], axis=1)
    coef = coef.reshape(N_TOK, 2, 2 * N_PICK)
    out = pl.pallas_call(
        _peer_out_kernel,
        grid=(N_TOK // TP,),
        in_specs=[smem_idx, pair, table],
        out_specs=rows,
        out_shape=jax.ShapeDtypeStruct((N_TOK, 8, 128), F32),
        scratch_shapes=[planes, planes],
        compiler_params=params,
        name="peer_out",
    )(idx, coef, v_words)
    return out.reshape(N_TOK, D_MODEL)


def _final_kernel(x_ref, p_ref, g2_ref, n_ref, o_ref):
    o_ref[...] = _rms(x_ref[...] + g2_ref[...] * p_ref[...], n_ref[...])


def _final(x, p, mod_l, final_norm):
    tok = pl.BlockSpec((TM, D_MODEL), lambda i: (i, 0))
    return pl.pallas_call(
        _final_kernel,
        grid=(N_TILES,),
        in_specs=[tok, tok, _mod_spec(5), _const_spec((1, D_MODEL))],
        out_specs=tok,
        out_shape=jax.ShapeDtypeStruct((N_TOK, D_MODEL), F32),
        name="final_norm",
    )(x, p, mod_l, final_norm.reshape(1, D_MODEL))


def _rope_tables():
    rows = DEC_SEQ // GRID_W
    row_ids = jnp.repeat(jnp.arange(rows), GRID_W).astype(F32)
    col_ids = jnp.tile(jnp.arange(GRID_W), rows).astype(F32)
    inv_freq = ROPE_BASE ** (-jnp.arange(ROPE_FREQS, dtype=F32) / ROPE_FREQS)
    ang_r = row_ids[:, None] * inv_freq
    ang_c = col_ids[:, None] * inv_freq
    cos = jnp.concatenate([jnp.cos(ang_r), jnp.cos(ang_r), jnp.cos(ang_c), jnp.cos(ang_c)], axis=1)
    sin = jnp.concatenate([-jnp.sin(ang_r), jnp.sin(ang_r), -jnp.sin(ang_c), jnp.sin(ang_c)], axis=1)
    reps = ATT_WIDTH // ATT_HD
    cos = jnp.concatenate([jnp.ones((TM, ATT_WIDTH), F32), jnp.tile(cos, (1, reps))], axis=0)
    sin = jnp.concatenate([jnp.zeros((TM, ATT_WIDTH), F32), jnp.tile(sin, (1, reps))], axis=0)
    return cos, sin


def kernel(x_prompt, x_sample, cache_k, cache_v, state_lru, c, c_ctx, w_mod, b_mod, norm1, norm2, w_in, a_norm, a_ws, a_bs, lru_conv_w, lru_conv_b, lru_w_r, lru_b_r, lru_w_i, lru_b_i, lru_lam, att_lam, att_subln, w_up_a, w_up_b, w_up_c, w_out, peer_wq, peer_keys, peer_u, peer_v, final_norm):
    x = jnp.concatenate([x_prompt.reshape(N_CTX_TOK, D_MODEL), x_sample.reshape(N_LAT_TOK, D_MODEL)], axis=0)
    cond = jnp.concatenate([c_ctx[None, :], c, jnp.zeros((N_COND - 1 - DEC_BATCH, D_MODEL), F32)], axis=0)
    mod = _modulation(cond, w_mod, b_mod).reshape(DEPTH, N_COND, 1, N_MOD * D_MODEL)
    cos_t, sin_t = _rope_tables()
    zero_h0 = jnp.zeros((BATCH, 2, LRU_WIDTH), F32)
    ks, vs, hs = [], [], []
    res = None
    for i in range(DEPTH):
        lam_init = 0.8 - 0.6 * math.exp(-0.3 * i)
        outs = _inproj(x, res, mod[i], norm1[i], w_in[i].astype(BF16), cos_t, sin_t)
        au, av, rx, rg, q, k, kr, v, br = outs[:9]
        if res is not None:
            x = outs[9]
        w_r = _blockdiag128(lru_w_r[i])
        w_i = _blockdiag128(lru_w_i[i])
        lru_args = (lru_conv_w[i], lru_conv_b[i], w_r, lru_b_r[i], w_i, lru_b_i[i], lru_lam[i])
        hs_ctx, hl_ctx = _lru(rx, 0, BATCH, SEQ, *lru_args, zero_h0)
        hs_lat, _ = _lru(rx, N_CTX_TOK, DEC_BATCH, DEC_SEQ, *lru_args, state_lru[:, i])
        o_ctx = _attention(q, kr, v, 0, BATCH, SEQ, None, att_lam[i], att_subln[i], lam_init)
        ctx = (cache_k[:, i].reshape(DEC_BATCH, PAST_LEN, ATT_WIDTH), cache_v[:, i].reshape(DEC_BATCH, PAST_LEN, ATT_WIDTH))
        o_lat = _attention(q, kr, v, N_CTX_TOK, DEC_BATCH, DEC_SEQ, ctx, att_lam[i], att_subln[i], lam_init)
        bias = jnp.repeat(a_bs[i].T, CHUNK, axis=1)
        x = _merge(x, au, av, jnp.concatenate([hs_ctx, hs_lat], axis=0), rg, jnp.concatenate([o_ctx, o_lat], axis=0),
                   br, mod[i], a_norm[i], a_ws[i], bias, w_up_a[i].astype(BF16), w_up_b[i].astype(BF16),
                   w_up_c[i].astype(BF16), w_out[i].astype(BF16))
        h2, idx, gate = _route(x, mod[i], norm2[i], peer_wq[i].astype(BF16), peer_keys[i])
        p = _peer_experts(h2, idx.T, gate.T, _pack_table(peer_u[i]), _pack_table(peer_v[i]))
        res = (p, mod[i])
        ks.append(k[:N_CTX_TOK].reshape(BATCH, SEQ, ATT_HEADS, ATT_VD))
        vs.append(v[:N_CTX_TOK].reshape(BATCH, SEQ, ATT_HEADS, ATT_VD))
        hs.append(hl_ctx)
    y = _final(x, res[0], res[1], final_norm)
    return (y[:N_CTX_TOK].reshape(BATCH, SEQ, D_MODEL), y[N_CTX_TOK:].reshape(DEC_BATCH, DEC_SEQ, D_MODEL),
            jnp.stack(ks, axis=1), jnp.stack(vs, axis=1), jnp.stack(hs, axis=1))
```

```python
import functools
import math

import jax
import jax.numpy as jnp
from jax import lax
from jax.experimental import pallas as pl
from jax.experimental.pallas import tpu as pltpu

D_MODEL = 1024
BATCH = 16
SEQ = 256
DEPTH = 2
DEC_BATCH = 4
DEC_SEQ = 2048
PAST_LEN = 512
GRID_W = 64
EPS = 1e-6
N_MOD = 6
CHUNK = 128
A_GROUPS = 4
A_WIDTH = 512
LRU_BLOCKS = 8
LRU_BLOCK_W = 64
LRU_WIDTH = 512
CONV_W = 4
LRU_C = 8.0
ATT_HEADS = 4
ATT_HD = 64
ATT_VD = 128
ATT_WIDTH = 512
ROPE_BASE = 10000.0
ROPE_FREQS = 16
N_BRANCH = 3
IN_SPLITS = (512, 1024, 1536, 2048, 2560, 3072, 3584)
IN_WIDTH = 3584 + N_BRANCH * D_MODEL
PEER_HEADS = 8
N_KEYS = 128
N_EXPERTS = N_KEYS * N_KEYS
PEER_QDIM = 256
PEER_HALF = 128
PEER_TOPK = 16
N_PICK = PEER_HEADS * PEER_TOPK
CAND_COLS = tuple(PEER_TOPK // (a + 1) for a in range(PEER_TOPK))
CAND_START = tuple(sum(CAND_COLS[:a]) for a in range(PEER_TOPK))
N_CAND = sum(CAND_COLS)
CAND_PAD = -N_CAND % 8

N_CTX_TOK = BATCH * SEQ
N_LAT_TOK = DEC_BATCH * DEC_SEQ
N_TOK = N_CTX_TOK + N_LAT_TOK
TM = 256
N_TILES = N_TOK // TM
CTX_TILES = N_CTX_TOK // TM
LAT_TILES_PER_SEQ = DEC_SEQ // TM
N_COND = 8
TP = 64
TOKEN_UNROLL = 16
PACK_ROWS = 256
ROW_WORDS = D_MODEL // 2
ROW_SUB = ROW_WORDS // 128
PLANE_STRIDE = N_PICK + 8
VMEM_LIMIT = 56 * 1024 * 1024

F32 = jnp.float32
BF16 = jnp.bfloat16
HI = lax.Precision.HIGHEST


def _cond_row(i):
    return jnp.maximum(i - LAT_TILES_PER_SEQ, 0) // LAT_TILES_PER_SEQ


def _pos_block(i):
    return jnp.where(i < CTX_TILES, 0, 1 + i % LAT_TILES_PER_SEQ)


def _mod_spec(chunk):
    return pl.BlockSpec((None, 1, D_MODEL), lambda i: (_cond_row(i), 0, chunk))


def _const_spec(shape):
    nd = len(shape)
    return pl.BlockSpec(shape, lambda *_: (0,) * nd)


def _rms(x, gain):
    return x * lax.rsqrt(jnp.mean(x * x, axis=-1, keepdims=True) + EPS) * gain


def _dot(a, b):
    return jnp.dot(a.astype(BF16), b.astype(BF16), preferred_element_type=F32)


def _dot_nt(a, b):
    return lax.dot_general(a.astype(BF16), b.astype(BF16), (((1,), (1,)), ((), ())), preferred_element_type=F32)


def _mod_kernel(cond_ref, w_ref, b_ref, o_ref):
    cond = cond_ref[...]
    act = cond * jax.nn.sigmoid(cond)
    o_ref[...] = jnp.dot(act, w_ref[...], preferred_element_type=F32, precision=HI) + b_ref[...]


def _modulation(cond, w_mod, b_mod):
    nc = 4
    cw = N_MOD * D_MODEL // nc
    return pl.pallas_call(
        _mod_kernel,
        grid=(DEPTH, nc),
        in_specs=[pl.BlockSpec((N_COND, D_MODEL), lambda l, j: (0, 0)),
                  pl.BlockSpec((None, D_MODEL, cw), lambda l, j: (l, 0, j)),
                  pl.BlockSpec((None, 1, cw), lambda l, j: (l, 0, j))],
        out_specs=pl.BlockSpec((None, N_COND, cw), lambda l, j: (l, 0, j)),
        out_shape=jax.ShapeDtypeStruct((DEPTH, N_COND, N_MOD * D_MODEL), F32),
        compiler_params=pltpu.CompilerParams(vmem_limit_bytes=VMEM_LIMIT),
        name="modulation",
    )(cond, w_mod, b_mod.reshape(DEPTH, 1, N_MOD * D_MODEL))


def _inproj_kernel(has_res, *refs):
    if has_res:
        x_ref, p_ref, g2_ref = refs[:3]
        refs = refs[3:]
        x = x_ref[...] + g2_ref[...] * p_ref[...]
    else:
        x_ref = refs[0]
        refs = refs[1:]
        x = x_ref[...]
    (sh_ref, sc_ref, n1_ref, w_ref, cos_ref, sin_ref,
     au_ref, av_ref, rx_ref, rg_ref, q_ref, k_ref, kr_ref, v_ref, br_ref) = refs[:15]
    if has_res:
        refs[15][...] = x
    h = (_rms(x, n1_ref[...]) * (1.0 + sc_ref[...]) + sh_ref[...]).astype(BF16)

    def proj(lo, hi):
        return jnp.dot(h, w_ref[:, lo:hi], preferred_element_type=F32)

    au_ref[...] = proj(0, IN_SPLITS[0])
    av_ref[...] = proj(IN_SPLITS[0], IN_SPLITS[1])
    rx_ref[...] = proj(IN_SPLITS[1], IN_SPLITS[2])
    rg_ref[...] = proj(IN_SPLITS[2], IN_SPLITS[3])
    q = proj(IN_SPLITS[3], IN_SPLITS[4])
    k = proj(IN_SPLITS[4], IN_SPLITS[5])
    v_ref[...] = proj(IN_SPLITS[5], IN_SPLITS[6])
    for j in range(N_BRANCH):
        lo = IN_SPLITS[6] + j * D_MODEL
        br_ref[:, j * D_MODEL:(j + 1) * D_MODEL] = proj(lo, lo + D_MODEL)
    k_ref[...] = k
    lane = lax.broadcasted_iota(jnp.int32, (TM, ATT_WIDTH), 1)
    first = (lane % (2 * ROPE_FREQS)) < ROPE_FREQS
    cos = cos_ref[...]
    sin = sin_ref[...]

    def rot(t):
        partner = jnp.where(first, pltpu.roll(t, ATT_WIDTH - ROPE_FREQS, 1), pltpu.roll(t, ROPE_FREQS, 1))
        return t * cos + partner * sin

    q_ref[...] = rot(q) * (ATT_HD ** -0.5)
    kr_ref[...] = rot(k)


def _inproj(x, res, mod_l, norm1, w_in, cos_t, sin_t):
    has_res = res is not None
    tok = lambda w: pl.BlockSpec((TM, w), lambda i: (i, 0))
    in_specs = [tok(D_MODEL)]
    args = [x]
    if has_res:
        p, mod_prev = res
        in_specs += [tok(D_MODEL), _mod_spec(5)]
        args += [p, mod_prev]
    in_specs += [_mod_spec(0), _mod_spec(1), _const_spec((1, D_MODEL)),
                 pl.BlockSpec((D_MODEL, IN_WIDTH), lambda i: (0, 0), pipeline_mode=pl.Buffered(1)),
                 pl.BlockSpec((TM, ATT_WIDTH), lambda i: (_pos_block(i), 0)),
                 pl.BlockSpec((TM, ATT_WIDTH), lambda i: (_pos_block(i), 0))]
    args += [mod_l, mod_l, norm1.reshape(1, D_MODEL), w_in, cos_t, sin_t]
    widths = [512] * 8 + [N_BRANCH * D_MODEL]
    out_specs = [tok(w) for w in widths]
    out_shape = [jax.ShapeDtypeStruct((N_TOK, w), F32) for w in widths]
    if has_res:
        out_specs.append(tok(D_MODEL))
        out_shape.append(jax.ShapeDtypeStruct((N_TOK, D_MODEL), F32))
    return pl.pallas_call(
        functools.partial(_inproj_kernel, has_res),
        grid=(N_TILES,),
        in_specs=in_specs,
        out_specs=out_specs,
        out_shape=out_shape,
        compiler_params=pltpu.CompilerParams(vmem_limit_bytes=VMEM_LIMIT),
        name="inproj",
    )(*args)


def _lru_kernel(seq, x_ref, cw_ref, cb_ref, wr_ref, br_ref, wi_ref, bi_ref, lam_ref, h0_ref, hs_ref, hl_ref):
    x = x_ref[...]
    t = lax.broadcasted_iota(jnp.int32, (seq, 128), 0)

    def shifted(v, k, fill):
        r = pltpu.roll(v, k % seq, 0)
        ok = (t >= k) if k > 0 else (t < seq + k)
        return jnp.where(ok, r, fill)

    cw = cw_ref[...]
    xc = (shifted(x, 2, 0.0) * cw[0:1] + shifted(x, 1, 0.0) * cw[1:2] + x * cw[2:3]
          + shifted(x, -1, 0.0) * cw[3:4] + cb_ref[...])

    def direction(d):
        r = jax.nn.sigmoid(jnp.dot(xc, wr_ref[d, 0], preferred_element_type=F32, precision=HI) + br_ref[d:d + 1])
        g = jax.nn.sigmoid(jnp.dot(xc, wi_ref[d, 0], preferred_element_type=F32, precision=HI) + bi_ref[d:d + 1])
        z = -lam_ref[d:d + 1]
        softplus = jnp.maximum(z, 0.0) + jnp.log(1.0 + jnp.exp(-jnp.abs(z)))
        log_a = -LRU_C * r * softplus
        a = jnp.exp(log_a)
        b = jnp.sqrt(1.0 - jnp.exp(2.0 * log_a)) * (g * xc)
        sgn = 1 if d == 0 else -1
        k = 1
        while k < seq:
            a_prev = shifted(a, sgn * k, 1.0)
            b_prev = shifted(b, sgn * k, 0.0)
            b = a * b_prev + b
            a = a * a_prev
            k *= 2
        return a * h0_ref[0, d:d + 1] + b

    hf = direction(0)
    hb = direction(1)
    hs_ref[...] = hf + hb
    hl_ref[0] = jnp.concatenate([hf[seq - 1:seq], hb[0:1]], axis=0)


def _lru(rx, row0, nseq, seq, conv_w, conv_b, w_r, b_r, w_i, b_i, lam, h0):
    nb = LRU_WIDTH // 128
    blk0 = row0 // seq
    par = lambda shape: pl.BlockSpec(shape, lambda b, c: (0,) * (len(shape) - 1) + (c,))
    return pl.pallas_call(
        functools.partial(_lru_kernel, seq),
        grid=(nseq, nb),
        in_specs=[pl.BlockSpec((seq, 128), lambda b, c: (blk0 + b, c)),
                  par((CONV_W, 128)), par((1, 128)),
                  pl.BlockSpec((2, 1, 128, 128), lambda b, c: (0, c, 0, 0)), par((2, 128)),
                  pl.BlockSpec((2, 1, 128, 128), lambda b, c: (0, c, 0, 0)), par((2, 128)),
                  par((2, 128)),
                  pl.BlockSpec((1, 2, 128), lambda b, c: (b, 0, c))],
        out_specs=[pl.BlockSpec((seq, 128), lambda b, c: (b, c)),
                   pl.BlockSpec((1, 2, 128), lambda b, c: (b, 0, c))],
        out_shape=[jax.ShapeDtypeStruct((nseq * seq, LRU_WIDTH), F32),
                   jax.ShapeDtypeStruct((nseq, 2, LRU_WIDTH), F32)],
        compiler_params=pltpu.CompilerParams(vmem_limit_bytes=VMEM_LIMIT),
        name="rglru",
    )(rx, conv_w, conv_b.reshape(1, LRU_WIDTH), w_r, b_r, w_i, b_i, lam, h0)


def _blockdiag128(w):
    w = w.reshape(2, LRU_BLOCKS // 2, 2, LRU_BLOCK_W, LRU_BLOCK_W)
    z = jnp.zeros_like(w[:, :, 0])
    top = jnp.concatenate([w[:, :, 0], z], axis=-1)
    bot = jnp.concatenate([z, w[:, :, 1]], axis=-1)
    return jnp.concatenate([top, bot], axis=-2)


def _attn_kernel(has_ctx, lam_init, *refs):
    if has_ctx:
        q_ref, k_ref, v_ref, kc_ref, vc_ref, lp_ref, g_ref, o_ref = refs
    else:
        q_ref, k_ref, v_ref, lp_ref, g_ref, o_ref = refs
    lp = lp_ref[...]
    lam = (jnp.exp(jnp.sum(lp[0:1] * lp[1:2], axis=-1, keepdims=True))
           - jnp.exp(jnp.sum(lp[2:3] * lp[3:4], axis=-1, keepdims=True)) + lam_init)
    q = q_ref[...]
    lane = lax.broadcasted_iota(jnp.int32, q.shape, 1)
    halves = (jnp.where(lane < ATT_HD, q, 0.0), jnp.where(lane >= ATT_HD, q, 0.0))
    k = k_ref[...]
    v = v_ref[...]
    w = []
    for qh in halves:
        s = _dot_nt(qh, k)
        m = jnp.max(s, axis=-1, keepdims=True)
        if has_ctx:
            sc = _dot_nt(qh, kc_ref[0])
            m = jnp.maximum(m, jnp.max(sc, axis=-1, keepdims=True))
            ec = jnp.exp(sc - m)
        e = jnp.exp(s - m)
        den = jnp.sum(e, axis=-1, keepdims=True)
        if has_ctx:
            den = den + jnp.sum(ec, axis=-1, keepdims=True)
            w.append((e / den, ec / den))
        else:
            w.append((e / den,))
    o = _dot(w[0][0] - lam * w[1][0], v)
    if has_ctx:
        o = o + _dot(w[0][1] - lam * w[1][1], vc_ref[0])
    o_ref[...] = _rms(o, g_ref[...]) * (1.0 - lam_init)


def _attention(q, kr, v, row0, nseq, seq, ctx, att_lam, subln, lam_init):
    has_ctx = ctx is not None
    nq = seq // TM
    blk0 = row0 // seq
    in_specs = [pl.BlockSpec((TM, ATT_VD), lambda b, h, i: ((row0 // TM) + b * nq + i, h)),
                pl.BlockSpec((seq, ATT_VD), lambda b, h, i: (blk0 + b, h)),
                pl.BlockSpec((seq, ATT_VD), lambda b, h, i: (blk0 + b, h))]
    args = [q, kr, v]
    if has_ctx:
        in_specs += [pl.BlockSpec((1, PAST_LEN, ATT_VD), lambda b, h, i: (b, 0, h))] * 2
        args += list(ctx)
    in_specs += [_const_spec((4, ATT_HD)), _const_spec((1, ATT_VD))]
    args += [att_lam, subln.reshape(1, ATT_VD)]
    return pl.pallas_call(
        functools.partial(_attn_kernel, has_ctx, lam_init),
        grid=(nseq, ATT_HEADS, nq),
        in_specs=in_specs,
        out_specs=pl.BlockSpec((TM, ATT_VD), lambda b, h, i: (b * nq + i, h)),
        out_shape=jax.ShapeDtypeStruct((nseq * seq, ATT_WIDTH), F32),
        compiler_params=pltpu.CompilerParams(vmem_limit_bytes=VMEM_LIMIT),
        name="diffattn",
    )(*args)


def _merge_kernel(x_ref, au_ref, av_ref, hsc_ref, hsl_ref, rg_ref, oc_ref, ol_ref, br_ref, g1_ref, an_ref, ws_ref,
                  bs_ref, wa_ref, wb_ref, wc_ref, wo_ref, xo_ref):
    is_ctx = pl.program_id(0) < CTX_TILES
    hs = jnp.where(is_ctx, hsc_ref[...], hsl_ref[...])
    o = jnp.where(is_ctx, oc_ref[...], ol_ref[...])
    vn = _rms(av_ref[...], an_ref[...])
    rows = []
    for c in range(TM // CHUNK):
        cols = []
        for g in range(A_GROUPS):
            blk = vn[c * CHUNK:(c + 1) * CHUNK, g * 128:(g + 1) * 128]
            cols.append(_dot(ws_ref[g], blk))
        rows.append(jnp.concatenate(cols, axis=1) + bs_ref[...])
    y_a = au_ref[...] * jnp.concatenate(rows, axis=0)
    y_b = hs * jax.nn.gelu(rg_ref[...])
    merged = (jax.nn.sigmoid(br_ref[:, 0:D_MODEL]) * _dot(y_a, wa_ref[...])
              + jax.nn.sigmoid(br_ref[:, D_MODEL:2 * D_MODEL]) * _dot(y_b, wb_ref[...])
              + jax.nn.sigmoid(br_ref[:, 2 * D_MODEL:3 * D_MODEL]) * _dot(o, wc_ref[...]))
    xo_ref[...] = x_ref[...] + g1_ref[...] * _dot(merged, wo_ref[...])


def _merge(x, au, av, hs_ctx, hs_lat, rg, o_ctx, o_lat, br, mod_l, a_norm, a_ws, bias, wa, wb, wc, wo):
    tok = lambda w: pl.BlockSpec((TM, w), lambda i: (i, 0))
    ctx = pl.BlockSpec((TM, 512), lambda i: (jnp.minimum(i, CTX_TILES - 1), 0))
    lat = pl.BlockSpec((TM, 512), lambda i: (jnp.maximum(i - CTX_TILES, 0), 0))
    return pl.pallas_call(
        _merge_kernel,
        grid=(N_TILES,),
        in_specs=[tok(D_MODEL), tok(512), tok(512), ctx, lat, tok(512), ctx, lat, tok(N_BRANCH * D_MODEL),
                  _mod_spec(2), _const_spec((1, A_WIDTH)), _const_spec((A_GROUPS, CHUNK, CHUNK)),
                  _const_spec((CHUNK, A_WIDTH)),
                  _const_spec((A_WIDTH, D_MODEL)), _const_spec((LRU_WIDTH, D_MODEL)),
                  _const_spec((ATT_WIDTH, D_MODEL)), _const_spec((D_MODEL, D_MODEL))],
        out_specs=tok(D_MODEL),
        out_shape=jax.ShapeDtypeStruct((N_TOK, D_MODEL), F32),
        compiler_params=pltpu.CompilerParams(vmem_limit_bytes=VMEM_LIMIT),
        name="merge",
    )(x, au, av, hs_ctx, hs_lat, rg, o_ctx, o_lat, br, mod_l, a_norm.reshape(1, A_WIDTH), a_ws, bias, wa, wb, wc, wo)


def _top16(s, n):
    pos = lax.broadcasted_iota(jnp.int32, s.shape, 0).astype(F32)
    vals, idxs = [], []
    for _ in range(PEER_TOPK):
        m = jnp.max(s, axis=0, keepdims=True)
        am = jnp.min(jnp.where(s == m, pos, float(n)), axis=0, keepdims=True)
        vals.append(m)
        idxs.append(am)
        s = jnp.where(pos == am, -jnp.inf, s)
    return jnp.concatenate(vals, axis=0), jnp.concatenate(idxs, axis=0)


def _take16(table, sel):
    out = jnp.zeros_like(table)
    for a in range(PEER_TOPK):
        out = jnp.where(sel == float(a), table[a:a + 1], out)
    return out


def _route_kernel(x_ref, sh_ref, sc_ref, n2_ref, wq_ref, keys_ref, h2_ref, idx_ref, gate_ref):
    h2 = _rms(x_ref[...], n2_ref[...]) * (1.0 + sc_ref[...]) + sh_ref[...]
    h2_ref[...] = h2
    q = _dot(h2, wq_ref[...])
    for h in range(PEER_HEADS):
        base = h * PEER_QDIM
        v1, i1 = _top16(_dot_nt(keys_ref[0], q[:, base:base + PEER_HALF]), N_KEYS)
        v2, i2 = _top16(_dot_nt(keys_ref[1], q[:, base + PEER_HALF:base + PEER_QDIM]), N_KEYS)
        cand = jnp.concatenate([v1[a:a + 1] + v2[0:n] for a, n in enumerate(CAND_COLS)]
                               + [jnp.full((CAND_PAD, TM), -jnp.inf, F32)], axis=0)
        top_s, pos = _top16(cand, N_CAND + CAND_PAD)
        a_sel = jnp.zeros_like(pos)
        b_sel = pos
        for a in range(1, PEER_TOPK):
            later = pos >= float(CAND_START[a])
            a_sel = a_sel + jnp.where(later, 1.0, 0.0)
            b_sel = b_sel - jnp.where(later, float(CAND_COLS[a - 1]), 0.0)
        expert = _take16(i1, a_sel) * N_KEYS + _take16(i2, b_sel)
        e = jnp.exp(top_s - top_s[0:1])
        rows = slice(h * PEER_TOPK, (h + 1) * PEER_TOPK)
        gate_ref[rows, :] = e / jnp.sum(e, axis=0, keepdims=True)
        idx_ref[rows, :] = expert.astype(jnp.int32) * ROW_SUB


def _route(x, mod_l, norm2, wq, keys):
    tok = pl.BlockSpec((TM, D_MODEL), lambda i: (i, 0))
    pick = pl.BlockSpec((N_PICK, TM), lambda i: (0, i))
    return pl.pallas_call(
        _route_kernel,
        grid=(N_TILES,),
        in_specs=[tok, _mod_spec(3), _mod_spec(4), _const_spec((1, D_MODEL)),
                  _const_spec((D_MODEL, PEER_HEADS * PEER_QDIM)), _const_spec((2, N_KEYS, PEER_HALF))],
        out_specs=[tok, pick, pick],
        out_shape=[jax.ShapeDtypeStruct((N_TOK, D_MODEL), F32),
                   jax.ShapeDtypeStruct((N_PICK, N_TOK), jnp.int32),
                   jax.ShapeDtypeStruct((N_PICK, N_TOK), F32)],
        compiler_params=pltpu.CompilerParams(vmem_limit_bytes=VMEM_LIMIT),
        name="peer_route",
    )(x, mod_l, mod_l, norm2.reshape(1, D_MODEL), wq, keys)


def _pack_kernel(t_ref, o_ref):
    def rounded_bits(v):
        return lax.bitcast_convert_type(v.astype(BF16).astype(F32), jnp.int32)

    lo = lax.shift_right_logical(rounded_bits(t_ref[:, 0:ROW_WORDS]), 16)
    hi = rounded_bits(t_ref[:, ROW_WORDS:D_MODEL]) & jnp.int32(-65536)
    words = lo | hi
    for g in range(PACK_ROWS // 8):
        for s in range(ROW_SUB):
            o_ref[pl.ds(8 * g * ROW_SUB + s, 8, stride=ROW_SUB), :] = words[8 * g:8 * g + 8, s * 128:(s + 1) * 128]


def _pack_table(tabs, layer):
    return pl.pallas_call(
        _pack_kernel,
        grid=(N_EXPERTS // PACK_ROWS,),
        in_specs=[pl.BlockSpec((None, PACK_ROWS, D_MODEL), lambda i: (layer, i, 0))],
        out_specs=pl.BlockSpec((PACK_ROWS * ROW_SUB, 128), lambda i: (i, 0)),
        out_shape=jax.ShapeDtypeStruct((N_EXPERTS * ROW_SUB, 128), jnp.int32),
        name="pack_table",
    )(tabs)


def _unpack(words):
    lo = lax.bitcast_convert_type(words << 16, F32)
    hi = lax.bitcast_convert_type(words & jnp.int32(-65536), F32)
    return lo, hi


def _gather_planes(tab_ref, idx_ref, t, g_ref):
    for k in range(N_PICK):
        row = pl.multiple_of(idx_ref[t, k], ROW_SUB)
        g_ref[pl.ds(k, ROW_SUB, stride=PLANE_STRIDE), :] = tab_ref[pl.ds(row, ROW_SUB), :]


def _plane(g_ref, s):
    return g_ref[s * PLANE_STRIDE:s * PLANE_STRIDE + N_PICK, :]


def _split_bf16(v):
    hi = v.astype(BF16)
    return hi, (v - hi.astype(F32)).astype(BF16)


def _pipelined_tokens(tab_ref, idx_ref, bufs, compute):
    _gather_planes(tab_ref, idx_ref, 0, bufs[0])

    def group(j, carry):
        for p in range(TOKEN_UNROLL):
            t = TOKEN_UNROLL * j + p
            compute(t, pl.multiple_of(TOKEN_UNROLL * j + p // 8 * 8, 8), p % 8, bufs[p % 2])
            _gather_planes(tab_ref, idx_ref, jnp.minimum(t + 1, TP - 1), bufs[(p + 1) % 2])
        return carry

    lax.fori_loop(0, TP // TOKEN_UNROLL, group, 0)


def _peer_act_kernel(idx_ref, x_ref, gate_ref, tab_ref, o_ref, ga_ref, gb_ref):
    ones = jnp.ones((8, 2 * 128), BF16)

    def token(t, t8, r, g_ref):
        def x_row(j):
            return x_ref[pl.ds(t8, 8), j * 128:(j + 1) * 128][r:r + 1]

        acc = jnp.zeros((N_PICK, 128), F32)
        for s in range(ROW_SUB):
            lo, hi = _unpack(_plane(g_ref, s))
            acc = acc + lo * x_row(s) + hi * x_row(ROW_SUB + s)
        act = _dot_nt(ones, jnp.concatenate(_split_bf16(acc), axis=1))
        o_ref[pl.ds(t, 1), :] = jax.nn.gelu(act[0:1]) * gate_ref[pl.ds(t, 1), :]

    _pipelined_tokens(tab_ref, idx_ref, (ga_ref, gb_ref), token)


def _peer_out_kernel(idx_ref, coef_ref, tab_ref, o_ref, ga_ref, gb_ref, st_ref):
    def token(t, t8, r, g_ref):
        c_hi, c_lo = _split_bf16(coef_ref[t])
        lhs = jnp.concatenate([c_hi, c_lo, jnp.zeros((4, 2 * N_PICK), BF16)], axis=0)
        for s in range(ROW_SUB):
            w = pltpu.bitcast(_plane(g_ref, s), BF16)
            acc = jnp.dot(lhs, w, preferred_element_type=F32)
            st_ref[8 * s + r:8 * s + r + 1, :] = acc[0:1] + acc[2:3]
            st_ref[8 * (ROW_SUB + s) + r:8 * (ROW_SUB + s) + r + 1, :] = acc[1:2] + acc[3:4]
        if r == 7:
            for j in range(D_MODEL // 128):
                o_ref[pl.ds(t8, 8), j * 128:(j + 1) * 128] = st_ref[8 * j:8 * j + 8, :]

    _pipelined_tokens(tab_ref, idx_ref, (ga_ref, gb_ref), token)


def _peer_experts(h2, idx, gate, u_words, v_words):
    smem_idx = pl.BlockSpec((TP, N_PICK), lambda i: (i, 0), memory_space=pltpu.SMEM)
    rows = pl.BlockSpec((TP, D_MODEL), lambda i: (i, 0))
    pick = pl.BlockSpec((TP, N_PICK), lambda i: (i, 0))
    pair = pl.BlockSpec((TP, 2, 2 * N_PICK), lambda i: (i, 0, 0))
    table = pl.BlockSpec((N_EXPERTS * ROW_SUB, 128), lambda i: (0, 0), pipeline_mode=pl.Buffered(1))
    params = pltpu.CompilerParams(vmem_limit_bytes=VMEM_LIMIT)
    planes = pltpu.VMEM((ROW_SUB * PLANE_STRIDE, 128), jnp.int32)
    coef = pl.pallas_call(
        _peer_act_kernel,
        grid=(N_TOK // TP,),
        in_specs=[smem_idx, rows, pick, table],
        out_specs=pick,
        out_shape=jax.ShapeDtypeStruct((N_TOK, N_PICK), F32),
        scratch_shapes=[planes, planes],
        compiler_params=params,
        name="peer_act",
    )(idx, h2, gate, u_words)
    zero = jnp.zeros_like(coef)
    coef = jnp.stack([jnp.stack([coef, zero], axis=-1), jnp.stack([zero, coef], axis=-1)], axis=1)
    coef = coef.reshape(N_TOK, 2, 2 * N_PICK)
    return pl.pallas_call(
        _peer_out_kernel,
        grid=(N_TOK // TP,),
        in_specs=[smem_idx, pair, table],
        out_specs=rows,
        out_shape=jax.ShapeDtypeStruct((N_TOK, D_MODEL), F32),
        scratch_shapes=[planes, planes, pltpu.VMEM((8 * (D_MODEL // 128), 128), F32)],
        compiler_params=params,
        name="peer_out",
    )(idx, coef, v_words)


def _final_kernel(x_ref, p_ref, g2_ref, n_ref, o_ref):
    o_ref[...] = _rms(x_ref[...] + g2_ref[...] * p_ref[...], n_ref[...])


def _final(x, p, mod_l, final_norm):
    tok = pl.BlockSpec((TM, D_MODEL), lambda i: (i, 0))
    return pl.pallas_call(
        _final_kernel,
        grid=(N_TILES,),
        in_specs=[tok, tok, _mod_spec(5), _const_spec((1, D_MODEL))],
        out_specs=tok,
        out_shape=jax.ShapeDtypeStruct((N_TOK, D_MODEL), F32),
        name="final_norm",
    )(x, p, mod_l, final_norm.reshape(1, D_MODEL))


def _rope_tables():
    rows = DEC_SEQ // GRID_W
    row_ids = jnp.repeat(jnp.arange(rows), GRID_W).astype(F32)
    col_ids = jnp.tile(jnp.arange(GRID_W), rows).astype(F32)
    inv_freq = ROPE_BASE ** (-jnp.arange(ROPE_FREQS, dtype=F32) / ROPE_FREQS)
    ang_r = row_ids[:, None] * inv_freq
    ang_c = col_ids[:, None] * inv_freq
    cos = jnp.concatenate([jnp.cos(ang_r), jnp.cos(ang_r), jnp.cos(ang_c), jnp.cos(ang_c)], axis=1)
    sin = jnp.concatenate([-jnp.sin(ang_r), jnp.sin(ang_r), -jnp.sin(ang_c), jnp.sin(ang_c)], axis=1)
    reps = ATT_WIDTH // ATT_HD
    cos = jnp.concatenate([jnp.ones((TM, ATT_WIDTH), F32), jnp.tile(cos, (1, reps))], axis=0)
    sin = jnp.concatenate([jnp.zeros((TM, ATT_WIDTH), F32), jnp.tile(sin, (1, reps))], axis=0)
    return cos, sin


def kernel(x_prompt, x_sample, cache_k, cache_v, state_lru, c, c_ctx, w_mod, b_mod, norm1, norm2, w_in, a_norm, a_ws, a_bs, lru_conv_w, lru_conv_b, lru_w_r, lru_b_r, lru_w_i, lru_b_i, lru_lam, att_lam, att_subln, w_up_a, w_up_b, w_up_c, w_out, peer_wq, peer_keys, peer_u, peer_v, final_norm):
    x = jnp.concatenate([x_prompt.reshape(N_CTX_TOK, D_MODEL), x_sample.reshape(N_LAT_TOK, D_MODEL)], axis=0)
    cond = jnp.concatenate([c_ctx[None, :], c, jnp.zeros((N_COND - 1 - DEC_BATCH, D_MODEL), F32)], axis=0)
    mod = _modulation(cond, w_mod, b_mod).reshape(DEPTH, N_COND, 1, N_MOD * D_MODEL)
    cos_t, sin_t = _rope_tables()
    zero_h0 = jnp.zeros((BATCH, 2, LRU_WIDTH), F32)
    ks, vs, hs = [], [], []
    res = None
    for i in range(DEPTH):
        lam_init = 0.8 - 0.6 * math.exp(-0.3 * i)
        outs = _inproj(x, res, mod[i], norm1[i], w_in[i].astype(BF16), cos_t, sin_t)
        au, av, rx, rg, q, k, kr, v, br = outs[:9]
        if res is not None:
            x = outs[9]
        w_r = _blockdiag128(lru_w_r[i])
        w_i = _blockdiag128(lru_w_i[i])
        lru_args = (lru_conv_w[i], lru_conv_b[i], w_r, lru_b_r[i], w_i, lru_b_i[i], lru_lam[i])
        hs_ctx, hl_ctx = _lru(rx, 0, BATCH, SEQ, *lru_args, zero_h0)
        hs_lat, _ = _lru(rx, N_CTX_TOK, DEC_BATCH, DEC_SEQ, *lru_args, state_lru[:, i])
        o_ctx = _attention(q, kr, v, 0, BATCH, SEQ, None, att_lam[i], att_subln[i], lam_init)
        ctx = (cache_k[:, i].reshape(DEC_BATCH, PAST_LEN, ATT_WIDTH), cache_v[:, i].reshape(DEC_BATCH, PAST_LEN, ATT_WIDTH))
        o_lat = _attention(q, kr, v, N_CTX_TOK, DEC_BATCH, DEC_SEQ, ctx, att_lam[i], att_subln[i], lam_init)
        bias = jnp.repeat(a_bs[i].T, CHUNK, axis=1)
        x = _merge(x, au, av, hs_ctx, hs_lat, rg, o_ctx, o_lat, br, mod[i], a_norm[i], a_ws[i], bias, w_up_a[i].astype(BF16), w_up_b[i].astype(BF16),
                   w_up_c[i].astype(BF16), w_out[i].astype(BF16))
        h2, idx, gate = _route(x, mod[i], norm2[i], peer_wq[i].astype(BF16), peer_keys[i])
        p = _peer_experts(h2, idx.T, gate.T, _pack_table(peer_u, i), _pack_table(peer_v, i))
        res = (p, mod[i])
        ks.append(k[:N_CTX_TOK].reshape(BATCH, SEQ, ATT_HEADS, ATT_VD))
        vs.append(v[:N_CTX_TOK].reshape(BATCH, SEQ, ATT_HEADS, ATT_VD))
        hs.append(hl_ctx)
    y = _final(x, res[0], res[1], final_norm)
    return (y[:N_CTX_TOK].reshape(BATCH, SEQ, D_MODEL), y[N_CTX_TOK:].reshape(DEC_BATCH, DEC_SEQ, D_MODEL),
            jnp.stack(ks, axis=1), jnp.stack(vs, axis=1), jnp.stack(hs, axis=1))
```

```python
import functools
import math

import jax
import jax.numpy as jnp
from jax import lax
from jax.experimental import pallas as pl
from jax.experimental.pallas import tpu as pltpu

D_MODEL = 1024
BATCH = 16
SEQ = 256
DEPTH = 2
DEC_BATCH = 4
DEC_SEQ = 2048
PAST_LEN = 512
GRID_W = 64
EPS = 1e-6
N_MOD = 6
CHUNK = 128
A_GROUPS = 4
A_WIDTH = 512
LRU_BLOCKS = 8
LRU_BLOCK_W = 64
LRU_WIDTH = 512
CONV_W = 4
LRU_C = 8.0
ATT_HEADS = 4
ATT_HD = 64
ATT_VD = 128
ATT_WIDTH = 512
ROPE_BASE = 10000.0
ROPE_FREQS = 16
N_BRANCH = 3
IN_SPLITS = (512, 1024, 1536, 2048, 2560, 3072, 3584)
IN_WIDTH = 3584 + N_BRANCH * D_MODEL
PEER_HEADS = 8
N_KEYS = 128
N_EXPERTS = N_KEYS * N_KEYS
PEER_QDIM = 256
PEER_HALF = 128
PEER_TOPK = 16
N_PICK = PEER_HEADS * PEER_TOPK
CAND_COLS = tuple(PEER_TOPK // (a + 1) for a in range(PEER_TOPK))
CAND_START = tuple(sum(CAND_COLS[:a]) for a in range(PEER_TOPK))
N_CAND = sum(CAND_COLS)
CAND_PAD = -N_CAND % 8

N_CTX_TOK = BATCH * SEQ
N_LAT_TOK = DEC_BATCH * DEC_SEQ
N_TOK = N_CTX_TOK + N_LAT_TOK
TM = 256
N_TILES = N_TOK // TM
CTX_TILES = N_CTX_TOK // TM
LAT_TILES_PER_SEQ = DEC_SEQ // TM
N_COND = 8
TP = 128
TOKEN_UNROLL = 16
PACK_ROWS = 256
ROW_WORDS = D_MODEL // 2
ROW_SUB = ROW_WORDS // 128
PLANE_STRIDE = N_PICK + 8
VMEM_LIMIT = 56 * 1024 * 1024

F32 = jnp.float32
BF16 = jnp.bfloat16
HI = lax.Precision.HIGHEST


def _cond_row(i):
    return jnp.maximum(i - LAT_TILES_PER_SEQ, 0) // LAT_TILES_PER_SEQ


def _pos_block(i):
    return jnp.where(i < CTX_TILES, 0, 1 + i % LAT_TILES_PER_SEQ)


def _mod_spec(chunk):
    return pl.BlockSpec((None, 1, D_MODEL), lambda i: (_cond_row(i), 0, chunk))


def _const_spec(shape):
    nd = len(shape)
    return pl.BlockSpec(shape, lambda *_: (0,) * nd)


def _rms(x, gain):
    return x * lax.rsqrt(jnp.mean(x * x, axis=-1, keepdims=True) + EPS) * gain


def _dot(a, b):
    return jnp.dot(a.astype(BF16), b.astype(BF16), preferred_element_type=F32)


def _dot_nt(a, b):
    return lax.dot_general(a.astype(BF16), b.astype(BF16), (((1,), (1,)), ((), ())), preferred_element_type=F32)


def _mod_kernel(cond_ref, w_ref, b_ref, o_ref):
    cond = cond_ref[...]
    act = cond * jax.nn.sigmoid(cond)
    o_ref[...] = jnp.dot(act, w_ref[...], preferred_element_type=F32, precision=HI) + b_ref[...]


def _modulation(cond, w_mod, b_mod):
    nc = 4
    cw = N_MOD * D_MODEL // nc
    return pl.pallas_call(
        _mod_kernel,
        grid=(DEPTH, nc),
        in_specs=[pl.BlockSpec((N_COND, D_MODEL), lambda l, j: (0, 0)),
                  pl.BlockSpec((None, D_MODEL, cw), lambda l, j: (l, 0, j)),
                  pl.BlockSpec((None, 1, cw), lambda l, j: (l, 0, j))],
        out_specs=pl.BlockSpec((None, N_COND, cw), lambda l, j: (l, 0, j)),
        out_shape=jax.ShapeDtypeStruct((DEPTH, N_COND, N_MOD * D_MODEL), F32),
        compiler_params=pltpu.CompilerParams(vmem_limit_bytes=VMEM_LIMIT),
        name="modulation",
    )(cond, w_mod, b_mod.reshape(DEPTH, 1, N_MOD * D_MODEL))


def _inproj_kernel(has_res, *refs):
    if has_res:
        x_ref, p_ref, g2_ref = refs[:3]
        refs = refs[3:]
        x = x_ref[...] + g2_ref[...] * p_ref[...]
    else:
        x_ref = refs[0]
        refs = refs[1:]
        x = x_ref[...]
    (sh_ref, sc_ref, n1_ref, w_ref, cos_ref, sin_ref,
     au_ref, av_ref, rx_ref, rg_ref, q_ref, k_ref, kr_ref, v_ref, br_ref) = refs[:15]
    if has_res:
        refs[15][...] = x
    h = (_rms(x, n1_ref[...]) * (1.0 + sc_ref[...]) + sh_ref[...]).astype(BF16)

    def proj(lo, hi):
        return jnp.dot(h, w_ref[:, lo:hi], preferred_element_type=F32)

    au_ref[...] = proj(0, IN_SPLITS[0])
    av_ref[...] = proj(IN_SPLITS[0], IN_SPLITS[1])
    rx_ref[...] = proj(IN_SPLITS[1], IN_SPLITS[2])
    rg_ref[...] = proj(IN_SPLITS[2], IN_SPLITS[3])
    q = proj(IN_SPLITS[3], IN_SPLITS[4])
    k = proj(IN_SPLITS[4], IN_SPLITS[5])
    v_ref[...] = proj(IN_SPLITS[5], IN_SPLITS[6])
    for j in range(N_BRANCH):
        lo = IN_SPLITS[6] + j * D_MODEL
        br_ref[:, j * D_MODEL:(j + 1) * D_MODEL] = proj(lo, lo + D_MODEL)
    k_ref[...] = k
    lane = lax.broadcasted_iota(jnp.int32, (TM, ATT_WIDTH), 1)
    first = (lane % (2 * ROPE_FREQS)) < ROPE_FREQS
    cos = cos_ref[...]
    sin = sin_ref[...]

    def rot(t):
        partner = jnp.where(first, pltpu.roll(t, ATT_WIDTH - ROPE_FREQS, 1), pltpu.roll(t, ROPE_FREQS, 1))
        return t * cos + partner * sin

    q_ref[...] = rot(q) * (ATT_HD ** -0.5)
    kr_ref[...] = rot(k)


def _inproj(x, res, mod_l, norm1, w_in, cos_t, sin_t):
    has_res = res is not None
    tok = lambda w: pl.BlockSpec((TM, w), lambda i: (i, 0))
    in_specs = [tok(D_MODEL)]
    args = [x]
    if has_res:
        p, mod_prev = res
        in_specs += [tok(D_MODEL), _mod_spec(5)]
        args += [p, mod_prev]
    in_specs += [_mod_spec(0), _mod_spec(1), _const_spec((1, D_MODEL)),
                 pl.BlockSpec((D_MODEL, IN_WIDTH), lambda i: (0, 0), pipeline_mode=pl.Buffered(1)),
                 pl.BlockSpec((TM, ATT_WIDTH), lambda i: (_pos_block(i), 0)),
                 pl.BlockSpec((TM, ATT_WIDTH), lambda i: (_pos_block(i), 0))]
    args += [mod_l, mod_l, norm1.reshape(1, D_MODEL), w_in, cos_t, sin_t]
    widths = [512] * 8 + [N_BRANCH * D_MODEL]
    out_specs = [tok(w) for w in widths]
    out_shape = [jax.ShapeDtypeStruct((N_TOK, w), F32) for w in widths]
    if has_res:
        out_specs.append(tok(D_MODEL))
        out_shape.append(jax.ShapeDtypeStruct((N_TOK, D_MODEL), F32))
    return pl.pallas_call(
        functools.partial(_inproj_kernel, has_res),
        grid=(N_TILES,),
        in_specs=in_specs,
        out_specs=out_specs,
        out_shape=out_shape,
        compiler_params=pltpu.CompilerParams(vmem_limit_bytes=VMEM_LIMIT),
        name="inproj",
    )(*args)


def _lru_kernel(seq, x_ref, cw_ref, cb_ref, wr_ref, br_ref, wi_ref, bi_ref, lam_ref, h0_ref, hs_ref, hl_ref):
    x = x_ref[...]
    t = lax.broadcasted_iota(jnp.int32, (seq, 128), 0)

    def shifted(v, k, fill):
        r = pltpu.roll(v, k % seq, 0)
        ok = (t >= k) if k > 0 else (t < seq + k)
        return jnp.where(ok, r, fill)

    cw = cw_ref[...]
    xc = (shifted(x, 2, 0.0) * cw[0:1] + shifted(x, 1, 0.0) * cw[1:2] + x * cw[2:3]
          + shifted(x, -1, 0.0) * cw[3:4] + cb_ref[...])

    def direction(d):
        r = jax.nn.sigmoid(jnp.dot(xc, wr_ref[d, 0], preferred_element_type=F32, precision=HI) + br_ref[d:d + 1])
        g = jax.nn.sigmoid(jnp.dot(xc, wi_ref[d, 0], preferred_element_type=F32, precision=HI) + bi_ref[d:d + 1])
        z = -lam_ref[d:d + 1]
        softplus = jnp.maximum(z, 0.0) + jnp.log(1.0 + jnp.exp(-jnp.abs(z)))
        log_a = -LRU_C * r * softplus
        a = jnp.exp(log_a)
        b = jnp.sqrt(1.0 - jnp.exp(2.0 * log_a)) * (g * xc)
        sgn = 1 if d == 0 else -1
        k = 1
        while k < seq:
            a_prev = shifted(a, sgn * k, 1.0)
            b_prev = shifted(b, sgn * k, 0.0)
            b = a * b_prev + b
            a = a * a_prev
            k *= 2
        return a * h0_ref[0, d:d + 1] + b

    hf = direction(0)
    hb = direction(1)
    hs_ref[...] = hf + hb
    hl_ref[0] = jnp.concatenate([hf[seq - 1:seq], hb[0:1]], axis=0)


def _lru(rx, row0, nseq, seq, conv_w, conv_b, w_r, b_r, w_i, b_i, lam, h0):
    nb = LRU_WIDTH // 128
    blk0 = row0 // seq
    par = lambda shape: pl.BlockSpec(shape, lambda b, c: (0,) * (len(shape) - 1) + (c,))
    return pl.pallas_call(
        functools.partial(_lru_kernel, seq),
        grid=(nseq, nb),
        in_specs=[pl.BlockSpec((seq, 128), lambda b, c: (blk0 + b, c)),
                  par((CONV_W, 128)), par((1, 128)),
                  pl.BlockSpec((2, 1, 128, 128), lambda b, c: (0, c, 0, 0)), par((2, 128)),
                  pl.BlockSpec((2, 1, 128, 128), lambda b, c: (0, c, 0, 0)), par((2, 128)),
                  par((2, 128)),
                  pl.BlockSpec((1, 2, 128), lambda b, c: (b, 0, c))],
        out_specs=[pl.BlockSpec((seq, 128), lambda b, c: (b, c)),
                   pl.BlockSpec((1, 2, 128), lambda b, c: (b, 0, c))],
        out_shape=[jax.ShapeDtypeStruct((nseq * seq, LRU_WIDTH), F32),
                   jax.ShapeDtypeStruct((nseq, 2, LRU_WIDTH), F32)],
        compiler_params=pltpu.CompilerParams(vmem_limit_bytes=VMEM_LIMIT),
        name="rglru",
    )(rx, conv_w, conv_b.reshape(1, LRU_WIDTH), w_r, b_r, w_i, b_i, lam, h0)


def _blockdiag128(w):
    w = w.reshape(2, LRU_BLOCKS // 2, 2, LRU_BLOCK_W, LRU_BLOCK_W)
    z = jnp.zeros_like(w[:, :, 0])
    top = jnp.concatenate([w[:, :, 0], z], axis=-1)
    bot = jnp.concatenate([z, w[:, :, 1]], axis=-1)
    return jnp.concatenate([top, bot], axis=-2)


def _attn_kernel(has_ctx, lam_init, *refs):
    if has_ctx:
        q_ref, k_ref, v_ref, kc_ref, vc_ref, lp_ref, g_ref, o_ref = refs
    else:
        q_ref, k_ref, v_ref, lp_ref, g_ref, o_ref = refs
    lp = lp_ref[...]
    lam = (jnp.exp(jnp.sum(lp[0:1] * lp[1:2], axis=-1, keepdims=True))
           - jnp.exp(jnp.sum(lp[2:3] * lp[3:4], axis=-1, keepdims=True)) + lam_init)
    q = q_ref[...]
    lane = lax.broadcasted_iota(jnp.int32, q.shape, 1)
    halves = (jnp.where(lane < ATT_HD, q, 0.0), jnp.where(lane >= ATT_HD, q, 0.0))
    k = k_ref[...]
    v = v_ref[...]
    outs = []
    for qh in halves:
        s = _dot_nt(qh, k)
        m = jnp.max(s, axis=-1, keepdims=True)
        if has_ctx:
            sc = _dot_nt(qh, kc_ref[0])
            m = jnp.maximum(m, jnp.max(sc, axis=-1, keepdims=True))
        e = jnp.exp(s - m)
        den = jnp.sum(e, axis=-1, keepdims=True)
        num = _dot(e, v)
        if has_ctx:
            ec = jnp.exp(sc - m)
            den = den + jnp.sum(ec, axis=-1, keepdims=True)
            num = num + _dot(ec, vc_ref[0])
        outs.append(num / den)
    o = outs[0] - lam * outs[1]
    o_ref[...] = _rms(o, g_ref[...]) * (1.0 - lam_init)


def _attention(q, kr, v, row0, nseq, seq, ctx, att_lam, subln, lam_init):
    has_ctx = ctx is not None
    nq = seq // TM
    blk0 = row0 // seq
    in_specs = [pl.BlockSpec((TM, ATT_VD), lambda b, h, i: ((row0 // TM) + b * nq + i, h)),
                pl.BlockSpec((seq, ATT_VD), lambda b, h, i: (blk0 + b, h)),
                pl.BlockSpec((seq, ATT_VD), lambda b, h, i: (blk0 + b, h))]
    args = [q, kr, v]
    if has_ctx:
        in_specs += [pl.BlockSpec((1, PAST_LEN, ATT_VD), lambda b, h, i: (b, 0, h))] * 2
        args += list(ctx)
    in_specs += [_const_spec((4, ATT_HD)), _const_spec((1, ATT_VD))]
    args += [att_lam, subln.reshape(1, ATT_VD)]
    return pl.pallas_call(
        functools.partial(_attn_kernel, has_ctx, lam_init),
        grid=(nseq, ATT_HEADS, nq),
        in_specs=in_specs,
        out_specs=pl.BlockSpec((TM, ATT_VD), lambda b, h, i: (b * nq + i, h)),
        out_shape=jax.ShapeDtypeStruct((nseq * seq, ATT_WIDTH), F32),
        compiler_params=pltpu.CompilerParams(vmem_limit_bytes=VMEM_LIMIT),
        name="diffattn",
    )(*args)


def _merge_kernel(x_ref, au_ref, av_ref, hsc_ref, hsl_ref, rg_ref, oc_ref, ol_ref, br_ref, g1_ref, an_ref, ws_ref,
                  bs_ref, wa_ref, wb_ref, wc_ref, wo_ref, xo_ref):
    is_ctx = pl.program_id(0) < CTX_TILES
    hs = jnp.where(is_ctx, hsc_ref[...], hsl_ref[...])
    o = jnp.where(is_ctx, oc_ref[...], ol_ref[...])
    vn = _rms(av_ref[...], an_ref[...])
    rows = []
    for c in range(TM // CHUNK):
        cols = []
        for g in range(A_GROUPS):
            blk = vn[c * CHUNK:(c + 1) * CHUNK, g * 128:(g + 1) * 128]
            cols.append(_dot(ws_ref[g], blk))
        rows.append(jnp.concatenate(cols, axis=1) + bs_ref[...])
    y_a = au_ref[...] * jnp.concatenate(rows, axis=0)
    y_b = hs * jax.nn.gelu(rg_ref[...])
    merged = (jax.nn.sigmoid(br_ref[:, 0:D_MODEL]) * _dot(y_a, wa_ref[...])
              + jax.nn.sigmoid(br_ref[:, D_MODEL:2 * D_MODEL]) * _dot(y_b, wb_ref[...])
              + jax.nn.sigmoid(br_ref[:, 2 * D_MODEL:3 * D_MODEL]) * _dot(o, wc_ref[...]))
    xo_ref[...] = x_ref[...] + g1_ref[...] * _dot(merged, wo_ref[...])


def _merge(x, au, av, hs_ctx, hs_lat, rg, o_ctx, o_lat, br, mod_l, a_norm, a_ws, bias, wa, wb, wc, wo):
    tok = lambda w: pl.BlockSpec((TM, w), lambda i: (i, 0))
    ctx = pl.BlockSpec((TM, 512), lambda i: (jnp.minimum(i, CTX_TILES - 1), 0))
    lat = pl.BlockSpec((TM, 512), lambda i: (jnp.maximum(i - CTX_TILES, 0), 0))
    return pl.pallas_call(
        _merge_kernel,
        grid=(N_TILES,),
        in_specs=[tok(D_MODEL), tok(512), tok(512), ctx, lat, tok(512), ctx, lat, tok(N_BRANCH * D_MODEL),
                  _mod_spec(2), _const_spec((1, A_WIDTH)), _const_spec((A_GROUPS, CHUNK, CHUNK)),
                  _const_spec((CHUNK, A_WIDTH)),
                  _const_spec((A_WIDTH, D_MODEL)), _const_spec((LRU_WIDTH, D_MODEL)),
                  _const_spec((ATT_WIDTH, D_MODEL)), _const_spec((D_MODEL, D_MODEL))],
        out_specs=tok(D_MODEL),
        out_shape=jax.ShapeDtypeStruct((N_TOK, D_MODEL), F32),
        compiler_params=pltpu.CompilerParams(vmem_limit_bytes=VMEM_LIMIT),
        name="merge",
    )(x, au, av, hs_ctx, hs_lat, rg, o_ctx, o_lat, br, mod_l, a_norm.reshape(1, A_WIDTH), a_ws, bias, wa, wb, wc, wo)


def _top16(s, n):
    pos = lax.broadcasted_iota(jnp.int32, s.shape, 0).astype(F32)
    vals, idxs = [], []
    for _ in range(PEER_TOPK):
        m = jnp.max(s, axis=0, keepdims=True)
        am = jnp.min(jnp.where(s == m, pos, float(n)), axis=0, keepdims=True)
        vals.append(m)
        idxs.append(am)
        s = jnp.where(pos == am, -jnp.inf, s)
    return jnp.concatenate(vals, axis=0), jnp.concatenate(idxs, axis=0)


def _take16(table, sel):
    out = jnp.zeros_like(table)
    for a in range(PEER_TOPK):
        out = jnp.where(sel == float(a), table[a:a + 1], out)
    return out


def _route_kernel(x_ref, sh_ref, sc_ref, n2_ref, wq_ref, keys_ref, h2_ref, idx_ref, gate_ref):
    h2 = _rms(x_ref[...], n2_ref[...]) * (1.0 + sc_ref[...]) + sh_ref[...]
    h2_ref[...] = h2
    q = _dot(h2, wq_ref[...])
    for h in range(PEER_HEADS):
        base = h * PEER_QDIM
        v1, i1 = _top16(_dot_nt(keys_ref[0], q[:, base:base + PEER_HALF]), N_KEYS)
        v2, i2 = _top16(_dot_nt(keys_ref[1], q[:, base + PEER_HALF:base + PEER_QDIM]), N_KEYS)
        cand = jnp.concatenate([v1[a:a + 1] + v2[0:n] for a, n in enumerate(CAND_COLS)]
                               + [jnp.full((CAND_PAD, TM), -jnp.inf, F32)], axis=0)
        top_s, pos = _top16(cand, N_CAND + CAND_PAD)
        a_sel = jnp.zeros_like(pos)
        b_sel = pos
        for a in range(1, PEER_TOPK):
            later = pos >= float(CAND_START[a])
            a_sel = a_sel + jnp.where(later, 1.0, 0.0)
            b_sel = b_sel - jnp.where(later, float(CAND_COLS[a - 1]), 0.0)
        expert = _take16(i1, a_sel) * N_KEYS + _take16(i2, b_sel)
        e = jnp.exp(top_s - top_s[0:1])
        rows = slice(h * PEER_TOPK, (h + 1) * PEER_TOPK)
        gate_ref[rows, :] = e / jnp.sum(e, axis=0, keepdims=True)
        idx_ref[rows, :] = expert.astype(jnp.int32) * ROW_SUB


def _route(x, mod_l, norm2, wq, keys):
    tok = pl.BlockSpec((TM, D_MODEL), lambda i: (i, 0))
    pick = pl.BlockSpec((N_PICK, TM), lambda i: (0, i))
    return pl.pallas_call(
        _route_kernel,
        grid=(N_TILES,),
        in_specs=[tok, _mod_spec(3), _mod_spec(4), _const_spec((1, D_MODEL)),
                  _const_spec((D_MODEL, PEER_HEADS * PEER_QDIM)), _const_spec((2, N_KEYS, PEER_HALF))],
        out_specs=[tok, pick, pick],
        out_shape=[jax.ShapeDtypeStruct((N_TOK, D_MODEL), F32),
                   jax.ShapeDtypeStruct((N_PICK, N_TOK), jnp.int32),
                   jax.ShapeDtypeStruct((N_PICK, N_TOK), F32)],
        compiler_params=pltpu.CompilerParams(vmem_limit_bytes=VMEM_LIMIT),
        name="peer_route",
    )(x, mod_l, mod_l, norm2.reshape(1, D_MODEL), wq, keys)


def _pack_kernel(t_ref, o_ref):
    def rounded_bits(v):
        return lax.bitcast_convert_type(v.astype(BF16).astype(F32), jnp.int32)

    lo = lax.shift_right_logical(rounded_bits(t_ref[:, 0:ROW_WORDS]), 16)
    hi = rounded_bits(t_ref[:, ROW_WORDS:D_MODEL]) & jnp.int32(-65536)
    words = lo | hi
    for g in range(PACK_ROWS // 8):
        for s in range(ROW_SUB):
            o_ref[pl.ds(8 * g * ROW_SUB + s, 8, stride=ROW_SUB), :] = words[8 * g:8 * g + 8, s * 128:(s + 1) * 128]


def _pack_table(tabs, layer):
    return pl.pallas_call(
        _pack_kernel,
        grid=(N_EXPERTS // PACK_ROWS,),
        in_specs=[pl.BlockSpec((None, PACK_ROWS, D_MODEL), lambda i: (layer, i, 0))],
        out_specs=pl.BlockSpec((PACK_ROWS * ROW_SUB, 128), lambda i: (i, 0)),
        out_shape=jax.ShapeDtypeStruct((N_EXPERTS * ROW_SUB, 128), jnp.int32),
        name="pack_table",
    )(tabs)


def _unpack(words):
    lo = lax.bitcast_convert_type(words << 16, F32)
    hi = lax.bitcast_convert_type(words & jnp.int32(-65536), F32)
    return lo, hi


def _gather_planes(tab_ref, idx_ref, t, g_ref):
    for k in range(N_PICK):
        row = pl.multiple_of(idx_ref[t, k], ROW_SUB)
        g_ref[pl.ds(k, ROW_SUB, stride=PLANE_STRIDE), :] = tab_ref[pl.ds(row, ROW_SUB), :]


def _plane(g_ref, s):
    return g_ref[s * PLANE_STRIDE:s * PLANE_STRIDE + N_PICK, :]


def _split_bf16(v):
    hi = v.astype(BF16)
    return hi, (v - hi.astype(F32)).astype(BF16)


def _pipelined_tokens(tab_ref, idx_ref, bufs, compute):
    _gather_planes(tab_ref, idx_ref, 0, bufs[0])

    def group(j, carry):
        for p in range(TOKEN_UNROLL):
            t = TOKEN_UNROLL * j + p
            compute(t, pl.multiple_of(TOKEN_UNROLL * j + p // 8 * 8, 8), p % 8, bufs[p % 2])
            _gather_planes(tab_ref, idx_ref, jnp.minimum(t + 1, TP - 1), bufs[(p + 1) % 2])
        return carry

    lax.fori_loop(0, TP // TOKEN_UNROLL, group, 0)


def _peer_act_kernel(idx_ref, x_ref, gate_ref, tab_ref, o_ref, ga_ref, gb_ref):
    ones = jnp.ones((8, 2 * 128), BF16)

    def token(t, t8, r, g_ref):
        def x_row(j):
            return x_ref[pl.ds(t8, 8), j * 128:(j + 1) * 128][r:r + 1]

        acc = jnp.zeros((N_PICK, 128), F32)
        for s in range(ROW_SUB):
            lo, hi = _unpack(_plane(g_ref, s))
            acc = acc + lo * x_row(s) + hi * x_row(ROW_SUB + s)
        act = _dot_nt(ones, jnp.concatenate(_split_bf16(acc), axis=1))
        o_ref[pl.ds(t, 1), :] = jax.nn.gelu(act[0:1]) * gate_ref[pl.ds(t, 1), :]

    _pipelined_tokens(tab_ref, idx_ref, (ga_ref, gb_ref), token)


def _peer_out_kernel(idx_ref, coef_ref, tab_ref, o_ref, ga_ref, gb_ref, st_ref):
    def token(t, t8, r, g_ref):
        c_hi, c_lo = _split_bf16(coef_ref[t])
        lhs = jnp.concatenate([c_hi, c_lo, jnp.zeros((4, 2 * N_PICK), BF16)], axis=0)
        for s in range(ROW_SUB):
            w = pltpu.bitcast(_plane(g_ref, s), BF16)
            acc = jnp.dot(lhs, w, preferred_element_type=F32)
            st_ref[8 * s + r:8 * s + r + 1, :] = acc[0:1] + acc[2:3]
            st_ref[8 * (ROW_SUB + s) + r:8 * (ROW_SUB + s) + r + 1, :] = acc[1:2] + acc[3:4]
        if r == 7:
            for j in range(D_MODEL // 128):
                o_ref[pl.ds(t8, 8), j * 128:(j + 1) * 128] = st_ref[8 * j:8 * j + 8, :]

    _pipelined_tokens(tab_ref, idx_ref, (ga_ref, gb_ref), token)


def _peer_experts(h2, idx, gate, u_words, v_words):
    smem_idx = pl.BlockSpec((TP, N_PICK), lambda i: (i, 0), memory_space=pltpu.SMEM)
    rows = pl.BlockSpec((TP, D_MODEL), lambda i: (i, 0))
    pick = pl.BlockSpec((TP, N_PICK), lambda i: (i, 0))
    pair = pl.BlockSpec((TP, 2, 2 * N_PICK), lambda i: (i, 0, 0))
    table = pl.BlockSpec((N_EXPERTS * ROW_SUB, 128), lambda i: (0, 0), pipeline_mode=pl.Buffered(1))
    params = pltpu.CompilerParams(vmem_limit_bytes=VMEM_LIMIT)
    planes = pltpu.VMEM((ROW_SUB * PLANE_STRIDE, 128), jnp.int32)
    coef = pl.pallas_call(
        _peer_act_kernel,
        grid=(N_TOK // TP,),
        in_specs=[smem_idx, rows, pick, table],
        out_specs=pick,
        out_shape=jax.ShapeDtypeStruct((N_TOK, N_PICK), F32),
        scratch_shapes=[planes, planes],
        compiler_params=params,
        name="peer_act",
    )(idx, h2, gate, u_words)
    zero = jnp.zeros_like(coef)
    coef = jnp.stack([jnp.stack([coef, zero], axis=-1), jnp.stack([zero, coef], axis=-1)], axis=1)
    coef = coef.reshape(N_TOK, 2, 2 * N_PICK)
    return pl.pallas_call(
        _peer_out_kernel,
        grid=(N_TOK // TP,),
        in_specs=[smem_idx, pair, table],
        out_specs=rows,
        out_shape=jax.ShapeDtypeStruct((N_TOK, D_MODEL), F32),
        scratch_shapes=[planes, planes, pltpu.VMEM((8 * (D_MODEL // 128), 128), F32)],
        compiler_params=params,
        name="peer_out",
    )(idx, coef, v_words)


def _final_kernel(x_ref, p_ref, g2_ref, n_ref, o_ref):
    o_ref[...] = _rms(x_ref[...] + g2_ref[...] * p_ref[...], n_ref[...])


def _final(x, p, mod_l, final_norm):
    tok = pl.BlockSpec((TM, D_MODEL), lambda i: (i, 0))
    return pl.pallas_call(
        _final_kernel,
        grid=(N_TILES,),
        in_specs=[tok, tok, _mod_spec(5), _const_spec((1, D_MODEL))],
        out_specs=tok,
        out_shape=jax.ShapeDtypeStruct((N_TOK, D_MODEL), F32),
        name="final_norm",
    )(x, p, mod_l, final_norm.reshape(1, D_MODEL))


def _rope_tables():
    rows = DEC_SEQ // GRID_W
    row_ids = jnp.repeat(jnp.arange(rows), GRID_W).astype(F32)
    col_ids = jnp.tile(jnp.arange(GRID_W), rows).astype(F32)
    inv_freq = ROPE_BASE ** (-jnp.arange(ROPE_FREQS, dtype=F32) / ROPE_FREQS)
    ang_r = row_ids[:, None] * inv_freq
    ang_c = col_ids[:, None] * inv_freq
    cos = jnp.concatenate([jnp.cos(ang_r), jnp.cos(ang_r), jnp.cos(ang_c), jnp.cos(ang_c)], axis=1)
    sin = jnp.concatenate([-jnp.sin(ang_r), jnp.sin(ang_r), -jnp.sin(ang_c), jnp.sin(ang_c)], axis=1)
    reps = ATT_WIDTH // ATT_HD
    cos = jnp.concatenate([jnp.ones((TM, ATT_WIDTH), F32), jnp.tile(cos, (1, reps))], axis=0)
    sin = jnp.concatenate([jnp.zeros((TM, ATT_WIDTH), F32), jnp.tile(sin, (1, reps))], axis=0)
    return cos, sin


def kernel(x_prompt, x_sample, cache_k, cache_v, state_lru, c, c_ctx, w_mod, b_mod, norm1, norm2, w_in, a_norm, a_ws, a_bs, lru_conv_w, lru_conv_b, lru_w_r, lru_b_r, lru_w_i, lru_b_i, lru_lam, att_lam, att_subln, w_up_a, w_up_b, w_up_c, w_out, peer_wq, peer_keys, peer_u, peer_v, final_norm):
    x = jnp.concatenate([x_prompt.reshape(N_CTX_TOK, D_MODEL), x_sample.reshape(N_LAT_TOK, D_MODEL)], axis=0)
    cond = jnp.concatenate([c_ctx[None, :], c, jnp.zeros((N_COND - 1 - DEC_BATCH, D_MODEL), F32)], axis=0)
    mod = _modulation(cond, w_mod, b_mod).reshape(DEPTH, N_COND, 1, N_MOD * D_MODEL)
    cos_t, sin_t = _rope_tables()
    zero_h0 = jnp.zeros((BATCH, 2, LRU_WIDTH), F32)
    ks, vs, hs = [], [], []
    res = None
    for i in range(DEPTH):
        lam_init = 0.8 - 0.6 * math.exp(-0.3 * i)
        outs = _inproj(x, res, mod[i], norm1[i], w_in[i].astype(BF16), cos_t, sin_t)
        au, av, rx, rg, q, k, kr, v, br = outs[:9]
        if res is not None:
            x = outs[9]
        w_r = _blockdiag128(lru_w_r[i])
        w_i = _blockdiag128(lru_w_i[i])
        lru_args = (lru_conv_w[i], lru_conv_b[i], w_r, lru_b_r[i], w_i, lru_b_i[i], lru_lam[i])
        hs_ctx, hl_ctx = _lru(rx, 0, BATCH, SEQ, *lru_args, zero_h0)
        hs_lat, _ = _lru(rx, N_CTX_TOK, DEC_BATCH, DEC_SEQ, *lru_args, state_lru[:, i])
        o_ctx = _attention(q, kr, v, 0, BATCH, SEQ, None, att_lam[i], att_subln[i], lam_init)
        ctx = (cache_k[:, i].reshape(DEC_BATCH, PAST_LEN, ATT_WIDTH), cache_v[:, i].reshape(DEC_BATCH, PAST_LEN, ATT_WIDTH))
        o_lat = _attention(q, kr, v, N_CTX_TOK, DEC_BATCH, DEC_SEQ, ctx, att_lam[i], att_subln[i], lam_init)
        bias = jnp.repeat(a_bs[i].T, CHUNK, axis=1)
        x = _merge(x, au, av, hs_ctx, hs_lat, rg, o_ctx, o_lat, br, mod[i], a_norm[i], a_ws[i], bias, w_up_a[i].astype(BF16), w_up_b[i].astype(BF16),
                   w_up_c[i].astype(BF16), w_out[i].astype(BF16))
        h2, idx, gate = _route(x, mod[i], norm2[i], peer_wq[i].astype(BF16), peer_keys[i])
        p = _peer_experts(h2, idx.T, gate.T, _pack_table(peer_u, i), _pack_table(peer_v, i))
        res = (p, mod[i])
        ks.append(k[:N_CTX_TOK].reshape(BATCH, SEQ, ATT_HEADS, ATT_VD))
        vs.append(v[:N_CTX_TOK].reshape(BATCH, SEQ, ATT_HEADS, ATT_VD))
        hs.append(hl_ctx)
    y = _final(x, res[0], res[1], final_norm)
    return (y[:N_CTX_TOK].reshape(BATCH, SEQ, D_MODEL), y[N_CTX_TOK:].reshape(DEC_BATCH, DEC_SEQ, D_MODEL),
            jnp.stack(ks, axis=1), jnp.stack(vs, axis=1), jnp.stack(hs, axis=1))
```

```python
import functools
import math

import jax
import jax.numpy as jnp
from jax import lax
from jax.experimental import pallas as pl
from jax.experimental.pallas import tpu as pltpu

D_MODEL = 1024
BATCH = 16
SEQ = 256
DEPTH = 2
DEC_BATCH = 4
DEC_SEQ = 2048
PAST_LEN = 512
GRID_W = 64
EPS = 1e-6
N_MOD = 6
CHUNK = 128
A_GROUPS = 4
A_WIDTH = 512
LRU_BLOCKS = 8
LRU_BLOCK_W = 64
LRU_WIDTH = 512
CONV_W = 4
LRU_C = 8.0
ATT_HEADS = 4
ATT_HD = 64
ATT_VD = 128
ATT_WIDTH = 512
ROPE_BASE = 10000.0
ROPE_FREQS = 16
N_BRANCH = 3
IN_SPLITS = (512, 1024, 1536, 2048, 2560, 3072, 3584)
IN_WIDTH = 3584 + N_BRANCH * D_MODEL
PEER_HEADS = 8
N_KEYS = 128
N_EXPERTS = N_KEYS * N_KEYS
PEER_QDIM = 256
PEER_HALF = 128
PEER_TOPK = 16
N_PICK = PEER_HEADS * PEER_TOPK
CAND_COLS = tuple(PEER_TOPK // (a + 1) for a in range(PEER_TOPK))
CAND_START = tuple(sum(CAND_COLS[:a]) for a in range(PEER_TOPK))
N_CAND = sum(CAND_COLS)
CAND_PAD = -N_CAND % 8

N_CTX_TOK = BATCH * SEQ
N_LAT_TOK = DEC_BATCH * DEC_SEQ
N_TOK = N_CTX_TOK + N_LAT_TOK
TM = 256
N_TILES = N_TOK // TM
N_THIRD = N_TOK // 3
THIRD_TILES = N_THIRD // TM
CTX_TILES = N_CTX_TOK // TM
LAT_TILES_PER_SEQ = DEC_SEQ // TM
N_COND = 8
TP = 128
TOKEN_UNROLL = 16
PACK_ROWS = 256
ROW_WORDS = D_MODEL // 2
ROW_SUB = ROW_WORDS // 128
PLANE_STRIDE = N_PICK + 8
VMEM_LIMIT = 56 * 1024 * 1024

F32 = jnp.float32
BF16 = jnp.bfloat16
HI = lax.Precision.HIGHEST


def _cond_row(i):
    return jnp.maximum(i - LAT_TILES_PER_SEQ, 0) // LAT_TILES_PER_SEQ


def _pos_block(i):
    return jnp.where(i < CTX_TILES, 0, 1 + i % LAT_TILES_PER_SEQ)


def _mod_spec(chunk):
    return pl.BlockSpec((None, 1, D_MODEL), lambda i: (_cond_row(i), 0, chunk))


def _third_specs(width):
    return [pl.BlockSpec((TM, width), lambda i, k=k: (jnp.clip(i - k * THIRD_TILES, 0, THIRD_TILES - 1), 0))
            for k in range(3)]


def _pick_third(refs):
    i = pl.program_id(0)
    return jnp.where(i < THIRD_TILES, refs[0][...], jnp.where(i < 2 * THIRD_TILES, refs[1][...], refs[2][...]))


def _const_spec(shape):
    nd = len(shape)
    return pl.BlockSpec(shape, lambda *_: (0,) * nd)


def _rms(x, gain):
    return x * lax.rsqrt(jnp.mean(x * x, axis=-1, keepdims=True) + EPS) * gain


def _dot(a, b):
    return jnp.dot(a.astype(BF16), b.astype(BF16), preferred_element_type=F32)


def _dot_nt(a, b):
    return lax.dot_general(a.astype(BF16), b.astype(BF16), (((1,), (1,)), ((), ())), preferred_element_type=F32)


def _mod_kernel(cond_ref, w_ref, b_ref, o_ref):
    cond = cond_ref[...]
    act = cond * jax.nn.sigmoid(cond)
    o_ref[...] = jnp.dot(act, w_ref[...], preferred_element_type=F32, precision=HI) + b_ref[...]


def _modulation(cond, w_mod, b_mod):
    nc = 4
    cw = N_MOD * D_MODEL // nc
    return pl.pallas_call(
        _mod_kernel,
        grid=(DEPTH, nc),
        in_specs=[pl.BlockSpec((N_COND, D_MODEL), lambda l, j: (0, 0)),
                  pl.BlockSpec((None, D_MODEL, cw), lambda l, j: (l, 0, j)),
                  pl.BlockSpec((None, 1, cw), lambda l, j: (l, 0, j))],
        out_specs=pl.BlockSpec((None, N_COND, cw), lambda l, j: (l, 0, j)),
        out_shape=jax.ShapeDtypeStruct((DEPTH, N_COND, N_MOD * D_MODEL), F32),
        compiler_params=pltpu.CompilerParams(vmem_limit_bytes=VMEM_LIMIT),
        name="modulation",
    )(cond, w_mod, b_mod.reshape(DEPTH, 1, N_MOD * D_MODEL))


def _inproj_kernel(has_res, *refs):
    if has_res:
        x_ref, g2_ref = refs[0], refs[4]
        x = x_ref[...] + g2_ref[...] * _pick_third(refs[1:4])
        refs = refs[5:]
    else:
        x_ref = refs[0]
        refs = refs[1:]
        x = x_ref[...]
    (sh_ref, sc_ref, n1_ref, w_ref, cos_ref, sin_ref,
     au_ref, av_ref, rx_ref, rg_ref, q_ref, k_ref, kr_ref, v_ref, br_ref) = refs[:15]
    if has_res:
        refs[15][...] = x
    h = (_rms(x, n1_ref[...]) * (1.0 + sc_ref[...]) + sh_ref[...]).astype(BF16)

    def proj(lo, hi):
        return jnp.dot(h, w_ref[:, lo:hi], preferred_element_type=F32)

    au_ref[...] = proj(0, IN_SPLITS[0])
    av_ref[...] = proj(IN_SPLITS[0], IN_SPLITS[1])
    rx_ref[...] = proj(IN_SPLITS[1], IN_SPLITS[2])
    rg_ref[...] = proj(IN_SPLITS[2], IN_SPLITS[3])
    q = proj(IN_SPLITS[3], IN_SPLITS[4])
    k = proj(IN_SPLITS[4], IN_SPLITS[5])
    v_ref[...] = proj(IN_SPLITS[5], IN_SPLITS[6])
    for j in range(N_BRANCH):
        lo = IN_SPLITS[6] + j * D_MODEL
        br_ref[:, j * D_MODEL:(j + 1) * D_MODEL] = proj(lo, lo + D_MODEL)
    k_ref[...] = k
    lane = lax.broadcasted_iota(jnp.int32, (TM, ATT_WIDTH), 1)
    first = (lane % (2 * ROPE_FREQS)) < ROPE_FREQS
    cos = cos_ref[...]
    sin = sin_ref[...]

    def rot(t):
        partner = jnp.where(first, pltpu.roll(t, ATT_WIDTH - ROPE_FREQS, 1), pltpu.roll(t, ROPE_FREQS, 1))
        return t * cos + partner * sin

    q_ref[...] = rot(q) * (ATT_HD ** -0.5)
    kr_ref[...] = rot(k)


def _inproj(x, res, mod_l, norm1, w_in, cos_t, sin_t):
    has_res = res is not None
    tok = lambda w: pl.BlockSpec((TM, w), lambda i: (i, 0))
    in_specs = [tok(D_MODEL)]
    args = [x]
    if has_res:
        p, mod_prev = res
        in_specs += _third_specs(D_MODEL) + [_mod_spec(5)]
        args += [*p, mod_prev]
    in_specs += [_mod_spec(0), _mod_spec(1), _const_spec((1, D_MODEL)),
                 pl.BlockSpec((D_MODEL, IN_WIDTH), lambda i: (0, 0), pipeline_mode=pl.Buffered(1)),
                 pl.BlockSpec((TM, ATT_WIDTH), lambda i: (_pos_block(i), 0)),
                 pl.BlockSpec((TM, ATT_WIDTH), lambda i: (_pos_block(i), 0))]
    args += [mod_l, mod_l, norm1.reshape(1, D_MODEL), w_in, cos_t, sin_t]
    widths = [512] * 8 + [N_BRANCH * D_MODEL]
    out_specs = [tok(w) for w in widths]
    out_shape = [jax.ShapeDtypeStruct((N_TOK, w), F32) for w in widths]
    if has_res:
        out_specs.append(tok(D_MODEL))
        out_shape.append(jax.ShapeDtypeStruct((N_TOK, D_MODEL), F32))
    return pl.pallas_call(
        functools.partial(_inproj_kernel, has_res),
        grid=(N_TILES,),
        in_specs=in_specs,
        out_specs=out_specs,
        out_shape=out_shape,
        compiler_params=pltpu.CompilerParams(vmem_limit_bytes=VMEM_LIMIT),
        name="inproj",
    )(*args)


def _lru_kernel(seq, x_ref, cw_ref, cb_ref, wr_ref, br_ref, wi_ref, bi_ref, lam_ref, h0_ref, hs_ref, hl_ref):
    x = x_ref[...]
    t = lax.broadcasted_iota(jnp.int32, (seq, 128), 0)

    def shifted(v, k, fill):
        r = pltpu.roll(v, k % seq, 0)
        ok = (t >= k) if k > 0 else (t < seq + k)
        return jnp.where(ok, r, fill)

    cw = cw_ref[...]
    xc = (shifted(x, 2, 0.0) * cw[0:1] + shifted(x, 1, 0.0) * cw[1:2] + x * cw[2:3]
          + shifted(x, -1, 0.0) * cw[3:4] + cb_ref[...])

    def direction(d):
        r = jax.nn.sigmoid(jnp.dot(xc, wr_ref[d, 0], preferred_element_type=F32, precision=HI) + br_ref[d:d + 1])
        g = jax.nn.sigmoid(jnp.dot(xc, wi_ref[d, 0], preferred_element_type=F32, precision=HI) + bi_ref[d:d + 1])
        z = -lam_ref[d:d + 1]
        softplus = jnp.maximum(z, 0.0) + jnp.log(1.0 + jnp.exp(-jnp.abs(z)))
        log_a = -LRU_C * r * softplus
        a = jnp.exp(log_a)
        b = jnp.sqrt(1.0 - jnp.exp(2.0 * log_a)) * (g * xc)
        sgn = 1 if d == 0 else -1
        k = 1
        while k < seq:
            a_prev = shifted(a, sgn * k, 1.0)
            b_prev = shifted(b, sgn * k, 0.0)
            b = a * b_prev + b
            a = a * a_prev
            k *= 2
        return a * h0_ref[0, d:d + 1] + b

    hf = direction(0)
    hb = direction(1)
    hs_ref[...] = hf + hb
    hl_ref[0] = jnp.concatenate([hf[seq - 1:seq], hb[0:1]], axis=0)


def _lru(rx, row0, nseq, seq, conv_w, conv_b, w_r, b_r, w_i, b_i, lam, h0):
    nb = LRU_WIDTH // 128
    blk0 = row0 // seq
    par = lambda shape: pl.BlockSpec(shape, lambda b, c: (0,) * (len(shape) - 1) + (c,))
    return pl.pallas_call(
        functools.partial(_lru_kernel, seq),
        grid=(nseq, nb),
        in_specs=[pl.BlockSpec((seq, 128), lambda b, c: (blk0 + b, c)),
                  par((CONV_W, 128)), par((1, 128)),
                  pl.BlockSpec((2, 1, 128, 128), lambda b, c: (0, c, 0, 0)), par((2, 128)),
                  pl.BlockSpec((2, 1, 128, 128), lambda b, c: (0, c, 0, 0)), par((2, 128)),
                  par((2, 128)),
                  pl.BlockSpec((1, 2, 128), lambda b, c: (b, 0, c))],
        out_specs=[pl.BlockSpec((seq, 128), lambda b, c: (b, c)),
                   pl.BlockSpec((1, 2, 128), lambda b, c: (b, 0, c))],
        out_shape=[jax.ShapeDtypeStruct((nseq * seq, LRU_WIDTH), F32),
                   jax.ShapeDtypeStruct((nseq, 2, LRU_WIDTH), F32)],
        compiler_params=pltpu.CompilerParams(vmem_limit_bytes=VMEM_LIMIT),
        name="rglru",
    )(rx, conv_w, conv_b.reshape(1, LRU_WIDTH), w_r, b_r, w_i, b_i, lam, h0)


def _blockdiag128(w):
    w = w.reshape(2, LRU_BLOCKS // 2, 2, LRU_BLOCK_W, LRU_BLOCK_W)
    z = jnp.zeros_like(w[:, :, 0])
    top = jnp.concatenate([w[:, :, 0], z], axis=-1)
    bot = jnp.concatenate([z, w[:, :, 1]], axis=-1)
    return jnp.concatenate([top, bot], axis=-2)


def _attn_kernel(has_ctx, lam_init, *refs):
    if has_ctx:
        q_ref, k_ref, v_ref, kc_ref, vc_ref, lp_ref, g_ref, o_ref = refs
    else:
        q_ref, k_ref, v_ref, lp_ref, g_ref, o_ref = refs
    lp = lp_ref[...]
    lam = (jnp.exp(jnp.sum(lp[0:1] * lp[1:2], axis=-1, keepdims=True))
           - jnp.exp(jnp.sum(lp[2:3] * lp[3:4], axis=-1, keepdims=True)) + lam_init)
    q = q_ref[...]
    lane = lax.broadcasted_iota(jnp.int32, q.shape, 1)
    halves = (jnp.where(lane < ATT_HD, q, 0.0), jnp.where(lane >= ATT_HD, q, 0.0))
    k = k_ref[...]
    v = v_ref[...]
    w = []
    for qh in halves:
        s = _dot_nt(qh, k)
        m = jnp.max(s, axis=-1, keepdims=True)
        if has_ctx:
            sc = _dot_nt(qh, kc_ref[0])
            m = jnp.maximum(m, jnp.max(sc, axis=-1, keepdims=True))
            ec = jnp.exp(sc - m)
        e = jnp.exp(s - m)
        den = jnp.sum(e, axis=-1, keepdims=True)
        if has_ctx:
            den = den + jnp.sum(ec, axis=-1, keepdims=True)
            w.append((e / den, ec / den))
        else:
            w.append((e / den,))
    o = _dot(w[0][0] - lam * w[1][0], v)
    if has_ctx:
        o = o + _dot(w[0][1] - lam * w[1][1], vc_ref[0])
    o_ref[...] = _rms(o, g_ref[...]) * (1.0 - lam_init)


def _attention(q, kr, v, row0, nseq, seq, ctx, att_lam, subln, lam_init):
    has_ctx = ctx is not None
    nq = seq // TM
    blk0 = row0 // seq
    in_specs = [pl.BlockSpec((TM, ATT_VD), lambda b, h, i: ((row0 // TM) + b * nq + i, h)),
                pl.BlockSpec((seq, ATT_VD), lambda b, h, i: (blk0 + b, h)),
                pl.BlockSpec((seq, ATT_VD), lambda b, h, i: (blk0 + b, h))]
    args = [q, kr, v]
    if has_ctx:
        in_specs += [pl.BlockSpec((1, PAST_LEN, ATT_VD), lambda b, h, i: (b, 0, h))] * 2
        args += list(ctx)
    in_specs += [_const_spec((4, ATT_HD)), _const_spec((1, ATT_VD))]
    args += [att_lam, subln.reshape(1, ATT_VD)]
    return pl.pallas_call(
        functools.partial(_attn_kernel, has_ctx, lam_init),
        grid=(nseq, ATT_HEADS, nq),
        in_specs=in_specs,
        out_specs=pl.BlockSpec((TM, ATT_VD), lambda b, h, i: (b * nq + i, h)),
        out_shape=jax.ShapeDtypeStruct((nseq * seq, ATT_WIDTH), F32),
        compiler_params=pltpu.CompilerParams(vmem_limit_bytes=VMEM_LIMIT),
        name="diffattn",
    )(*args)


def _merge_kernel(x_ref, au_ref, av_ref, hsc_ref, hsl_ref, rg_ref, oc_ref, ol_ref, br_ref, g1_ref, an_ref, ws_ref,
                  bs_ref, wa_ref, wb_ref, wc_ref, wo_ref, xo_ref):
    is_ctx = pl.program_id(0) < CTX_TILES
    hs = jnp.where(is_ctx, hsc_ref[...], hsl_ref[...])
    o = jnp.where(is_ctx, oc_ref[...], ol_ref[...])
    vn = _rms(av_ref[...], an_ref[...])
    rows = []
    for c in range(TM // CHUNK):
        cols = []
        for g in range(A_GROUPS):
            blk = vn[c * CHUNK:(c + 1) * CHUNK, g * 128:(g + 1) * 128]
            cols.append(_dot(ws_ref[g], blk))
        rows.append(jnp.concatenate(cols, axis=1) + bs_ref[...])
    y_a = au_ref[...] * jnp.concatenate(rows, axis=0)
    y_b = hs * jax.nn.gelu(rg_ref[...])
    merged = (jax.nn.sigmoid(br_ref[:, 0:D_MODEL]) * _dot(y_a, wa_ref[...])
              + jax.nn.sigmoid(br_ref[:, D_MODEL:2 * D_MODEL]) * _dot(y_b, wb_ref[...])
              + jax.nn.sigmoid(br_ref[:, 2 * D_MODEL:3 * D_MODEL]) * _dot(o, wc_ref[...]))
    xo_ref[...] = x_ref[...] + g1_ref[...] * _dot(merged, wo_ref[...])


def _merge(x, au, av, hs_ctx, hs_lat, rg, o_ctx, o_lat, br, mod_l, a_norm, a_ws, bias, wa, wb, wc, wo):
    tok = lambda w: pl.BlockSpec((TM, w), lambda i: (i, 0))
    ctx = pl.BlockSpec((TM, 512), lambda i: (jnp.minimum(i, CTX_TILES - 1), 0))
    lat = pl.BlockSpec((TM, 512), lambda i: (jnp.maximum(i - CTX_TILES, 0), 0))
    return pl.pallas_call(
        _merge_kernel,
        grid=(N_TILES,),
        in_specs=[tok(D_MODEL), tok(512), tok(512), ctx, lat, tok(512), ctx, lat, tok(N_BRANCH * D_MODEL),
                  _mod_spec(2), _const_spec((1, A_WIDTH)), _const_spec((A_GROUPS, CHUNK, CHUNK)),
                  _const_spec((CHUNK, A_WIDTH)),
                  _const_spec((A_WIDTH, D_MODEL)), _const_spec((LRU_WIDTH, D_MODEL)),
                  _const_spec((ATT_WIDTH, D_MODEL)), _const_spec((D_MODEL, D_MODEL))],
        out_specs=tok(D_MODEL),
        out_shape=jax.ShapeDtypeStruct((N_TOK, D_MODEL), F32),
        compiler_params=pltpu.CompilerParams(vmem_limit_bytes=VMEM_LIMIT),
        name="merge",
    )(x, au, av, hs_ctx, hs_lat, rg, o_ctx, o_lat, br, mod_l, a_norm.reshape(1, A_WIDTH), a_ws, bias, wa, wb, wc, wo)


def _top16(s, n):
    pos = lax.broadcasted_iota(jnp.int32, s.shape, 0).astype(F32)
    vals, idxs = [], []
    for _ in range(PEER_TOPK):
        m = jnp.max(s, axis=0, keepdims=True)
        am = jnp.min(jnp.where(s == m, pos, float(n)), axis=0, keepdims=True)
        vals.append(m)
        idxs.append(am)
        s = jnp.where(pos == am, -jnp.inf, s)
    return jnp.concatenate(vals, axis=0), jnp.concatenate(idxs, axis=0)


def _take16(table, sel):
    out = jnp.zeros_like(table)
    for a in range(PEER_TOPK):
        out = jnp.where(sel == float(a), table[a:a + 1], out)
    return out


def _route_hidden(x_ref, sh_ref, sc_ref, n2_ref):
    return _rms(x_ref[...], n2_ref[...]) * (1.0 + sc_ref[...]) + sh_ref[...]


def _route_head(h2b, wq_h, keys_ref, idx_ref, gate_ref, rows):
    tn = h2b.shape[0]
    q = jnp.dot(h2b, wq_h, preferred_element_type=F32)
    v1, i1 = _top16(_dot_nt(keys_ref[0], q[:, 0:PEER_HALF]), N_KEYS)
    v2, i2 = _top16(_dot_nt(keys_ref[1], q[:, PEER_HALF:PEER_QDIM]), N_KEYS)
    cand = jnp.concatenate([v1[a:a + 1] + v2[0:n] for a, n in enumerate(CAND_COLS)]
                           + [jnp.full((CAND_PAD, tn), -jnp.inf, F32)], axis=0)
    top_s, pos = _top16(cand, N_CAND + CAND_PAD)
    a_sel = jnp.zeros_like(pos)
    b_sel = pos
    for a in range(1, PEER_TOPK):
        later = pos >= float(CAND_START[a])
        a_sel = a_sel + jnp.where(later, 1.0, 0.0)
        b_sel = b_sel - jnp.where(later, float(CAND_COLS[a - 1]), 0.0)
    expert = _take16(i1, a_sel) * N_KEYS + _take16(i2, b_sel)
    e = jnp.exp(top_s - top_s[0:1])
    gate_ref[rows, :] = e / jnp.sum(e, axis=0, keepdims=True)
    idx_ref[rows, :] = expert.astype(jnp.int32) * ROW_SUB


def _route_kernel(x_ref, sh_ref, sc_ref, n2_ref, wq_ref, keys_ref, h2_ref, idx_ref, gate_ref):
    h2 = _route_hidden(x_ref, sh_ref, sc_ref, n2_ref)
    h2_ref[...] = h2
    h2b = h2.astype(BF16)
    for h in range(PEER_HEADS):
        _route_head(h2b, wq_ref[h], keys_ref, idx_ref, gate_ref, slice(h * PEER_TOPK, (h + 1) * PEER_TOPK))


def _route_out_shape():
    return [jax.ShapeDtypeStruct((N_THIRD, D_MODEL), F32),
            jax.ShapeDtypeStruct((N_PICK, N_THIRD), jnp.int32),
            jax.ShapeDtypeStruct((N_PICK, N_THIRD), F32)]


def _route_first(x, mod_l, norm2, wq, keys):
    tok = pl.BlockSpec((TM, D_MODEL), lambda i: (i, 0))
    pick = pl.BlockSpec((N_PICK, TM), lambda i: (0, i))
    return pl.pallas_call(
        _route_kernel,
        grid=(N_THIRD // TM,),
        in_specs=[tok, _mod_spec(3), _mod_spec(4), _const_spec((1, D_MODEL)),
                  _const_spec((PEER_HEADS, D_MODEL, PEER_QDIM)), _const_spec((2, N_KEYS, PEER_HALF))],
        out_specs=[tok, pick, pick],
        out_shape=_route_out_shape(),
        compiler_params=pltpu.CompilerParams(vmem_limit_bytes=VMEM_LIMIT),
        name="peer_route",
    )(x, mod_l, mod_l, norm2.reshape(1, D_MODEL), wq, keys)


def _route_side_specs(third):
    off = third * (N_THIRD // TP)

    def mod(chunk):
        return pl.BlockSpec((None, 1, D_MODEL), lambda i: (_cond_row((off + i) * TP // TM), 0, chunk))

    in_specs = [pl.BlockSpec((TP, D_MODEL), lambda i: (off + i, 0)), mod(3), mod(4), _const_spec((1, D_MODEL)),
                pl.BlockSpec((PEER_HEADS, D_MODEL, PEER_QDIM), lambda i: (0, 0, 0), pipeline_mode=pl.Buffered(1)),
                _const_spec((2, N_KEYS, PEER_HALF))]
    pick = pl.BlockSpec((N_PICK, TP), lambda i: (0, i))
    out_specs = [pl.BlockSpec((TP, D_MODEL), lambda i: (i, 0)), pick, pick]
    return in_specs, out_specs


def _route_side(in_refs, out_refs, h2b_ref):
    x_ref, sh_ref, sc_ref, n2_ref, wq_ref, keys_ref = in_refs
    h2_ref, idx_ref, gate_ref = out_refs
    h2 = _route_hidden(x_ref, sh_ref, sc_ref, n2_ref)
    h2_ref[...] = h2
    h2b_ref[...] = h2.astype(BF16)

    def side(h):
        rows = pl.ds(pl.multiple_of(h * PEER_TOPK, PEER_TOPK), PEER_TOPK)
        _route_head(h2b_ref[...], wq_ref[h], keys_ref, idx_ref, gate_ref, rows)

    return side


def _pack_kernel(t_ref, o_ref):
    def rounded_bits(v):
        return lax.bitcast_convert_type(v.astype(BF16).astype(F32), jnp.int32)

    lo = lax.shift_right_logical(rounded_bits(t_ref[:, 0:ROW_WORDS]), 16)
    hi = rounded_bits(t_ref[:, ROW_WORDS:D_MODEL]) & jnp.int32(-65536)
    words = lo | hi
    for g in range(PACK_ROWS // 8):
        for s in range(ROW_SUB):
            o_ref[pl.ds(8 * g * ROW_SUB + s, 8, stride=ROW_SUB), :] = words[8 * g:8 * g + 8, s * 128:(s + 1) * 128]


def _pack_table(tabs, layer):
    return pl.pallas_call(
        _pack_kernel,
        grid=(N_EXPERTS // PACK_ROWS,),
        in_specs=[pl.BlockSpec((None, PACK_ROWS, D_MODEL), lambda i: (layer, i, 0))],
        out_specs=pl.BlockSpec((PACK_ROWS * ROW_SUB, 128), lambda i: (i, 0)),
        out_shape=jax.ShapeDtypeStruct((N_EXPERTS * ROW_SUB, 128), jnp.int32),
        name="pack_table",
    )(tabs)


def _unpack(words):
    lo = lax.bitcast_convert_type(words << 16, F32)
    hi = lax.bitcast_convert_type(words & jnp.int32(-65536), F32)
    return lo, hi


def _gather_planes(tab_ref, idx_ref, t, g_ref):
    for k in range(N_PICK):
        row = pl.multiple_of(idx_ref[t, k], ROW_SUB)
        g_ref[pl.ds(k, ROW_SUB, stride=PLANE_STRIDE), :] = tab_ref[pl.ds(row, ROW_SUB), :]


def _plane(g_ref, s):
    return g_ref[s * PLANE_STRIDE:s * PLANE_STRIDE + N_PICK, :]


def _split_bf16(v):
    hi = v.astype(BF16)
    return hi, (v - hi.astype(F32)).astype(BF16)


def _pipelined_tokens(tab_ref, idx_ref, bufs, compute, side):
    _gather_planes(tab_ref, idx_ref, 0, bufs[0])

    def group(j, carry):
        if side is not None:
            side(j)
        for p in range(TOKEN_UNROLL):
            t = TOKEN_UNROLL * j + p
            compute(t, pl.multiple_of(TOKEN_UNROLL * j + p // 8 * 8, 8), p % 8, bufs[p % 2])
            _gather_planes(tab_ref, idx_ref, jnp.minimum(t + 1, TP - 1), bufs[(p + 1) % 2])
        return carry

    lax.fori_loop(0, TP // TOKEN_UNROLL, group, 0)


def _split_refs(with_route, refs, n_in, n_scratch):
    n_rin, n_rout = (6, 3) if with_route else (0, 0)
    cuts = [n_in, n_rin, 1, n_rout, n_scratch]
    parts, at = [], 0
    for n in cuts:
        parts.append(refs[at:at + n])
        at += n
    parts.append(refs[at:])
    return parts


def _peer_act_kernel(with_route, *refs):
    (idx_ref, x_ref, gate_ref, tab_ref), rin, (o_ref,), rout, (ga_ref, gb_ref), rscratch = _split_refs(
        with_route, refs, 4, 2)
    side = _route_side(rin, rout, rscratch[0]) if with_route else None
    ones = jnp.ones((8, 2 * 128), BF16)

    def token(t, t8, r, g_ref):
        def x_row(j):
            return x_ref[pl.ds(t8, 8), j * 128:(j + 1) * 128][r:r + 1]

        acc = jnp.zeros((N_PICK, 128), F32)
        for s in range(ROW_SUB):
            lo, hi = _unpack(_plane(g_ref, s))
            acc = acc + lo * x_row(s) + hi * x_row(ROW_SUB + s)
        act = _dot_nt(ones, jnp.concatenate(_split_bf16(acc), axis=1))
        o_ref[pl.ds(t, 1), :] = jax.nn.gelu(act[0:1]) * gate_ref[pl.ds(t, 1), :]

    _pipelined_tokens(tab_ref, idx_ref, (ga_ref, gb_ref), token, side)


def _peer_out_kernel(with_route, *refs):
    (idx_ref, coef_ref, tab_ref), rin, (o_ref,), rout, (ga_ref, gb_ref, st_ref), rscratch = _split_refs(
        with_route, refs, 3, 3)
    side = _route_side(rin, rout, rscratch[0]) if with_route else None

    def token(t, t8, r, g_ref):
        c_hi, c_lo = _split_bf16(coef_ref[t])
        lhs = jnp.concatenate([c_hi, c_lo, jnp.zeros((4, 2 * N_PICK), BF16)], axis=0)
        for s in range(ROW_SUB):
            w = pltpu.bitcast(_plane(g_ref, s), BF16)
            acc = jnp.dot(lhs, w, preferred_element_type=F32)
            st_ref[8 * s + r:8 * s + r + 1, :] = acc[0:1] + acc[2:3]
            st_ref[8 * (ROW_SUB + s) + r:8 * (ROW_SUB + s) + r + 1, :] = acc[1:2] + acc[3:4]
        if r == 7:
            for j in range(D_MODEL // 128):
                o_ref[pl.ds(t8, 8), j * 128:(j + 1) * 128] = st_ref[8 * j:8 * j + 8, :]

    _pipelined_tokens(tab_ref, idx_ref, (ga_ref, gb_ref), token, side)


def _expert_call(body, name, in_specs, args, out_spec, out_width, scratch, side):
    out_specs, out_shape = [out_spec], [jax.ShapeDtypeStruct((N_THIRD, out_width), F32)]
    if side is not None:
        assert TP // TOKEN_UNROLL == PEER_HEADS
        third, x, mod_l, norm2, wq, keys = side
        rin, rout = _route_side_specs(third)
        in_specs = in_specs + rin
        args = args + (x, mod_l, mod_l, norm2.reshape(1, D_MODEL), wq, keys)
        out_specs = out_specs + rout
        out_shape = out_shape + _route_out_shape()
        scratch = scratch + [pltpu.VMEM((TP, D_MODEL), BF16)]
    outs = pl.pallas_call(
        functools.partial(body, side is not None),
        grid=(N_THIRD // TP,),
        in_specs=in_specs,
        out_specs=out_specs,
        out_shape=out_shape,
        scratch_shapes=scratch,
        compiler_params=pltpu.CompilerParams(vmem_limit_bytes=VMEM_LIMIT),
        name=name,
    )(*args)
    return outs[0], tuple(outs[1:])


_SMEM_IDX = pl.BlockSpec((TP, N_PICK), lambda i: (i, 0), memory_space=pltpu.SMEM)
_TABLE = pl.BlockSpec((N_EXPERTS * ROW_SUB, 128), lambda i: (0, 0), pipeline_mode=pl.Buffered(1))
_PLANES = pltpu.VMEM((ROW_SUB * PLANE_STRIDE, 128), jnp.int32)


def _peer_act(idx, h2, gate, u_words, side=None):
    rows = pl.BlockSpec((TP, D_MODEL), lambda i: (i, 0))
    pick = pl.BlockSpec((TP, N_PICK), lambda i: (i, 0))
    coef, routed = _expert_call(_peer_act_kernel, "peer_act", [_SMEM_IDX, rows, pick, _TABLE],
                                (idx, h2, gate, u_words), pick, N_PICK, [_PLANES, _PLANES], side)
    zero = jnp.zeros_like(coef)
    coef = jnp.stack([jnp.stack([coef, zero], axis=-1), jnp.stack([zero, coef], axis=-1)], axis=1)
    return coef.reshape(N_THIRD, 2, 2 * N_PICK), routed


def _peer_out(idx, coef, v_words, side=None):
    rows = pl.BlockSpec((TP, D_MODEL), lambda i: (i, 0))
    pair = pl.BlockSpec((TP, 2, 2 * N_PICK), lambda i: (i, 0, 0))
    stage = pltpu.VMEM((8 * (D_MODEL // 128), 128), F32)
    return _expert_call(_peer_out_kernel, "peer_out", [_SMEM_IDX, pair, _TABLE], (idx, coef, v_words),
                        rows, D_MODEL, [_PLANES, _PLANES, stage], side)


def _peer(x, mod_l, norm2, wq, keys, u_words, v_words):
    route_args = (x, mod_l, norm2, wq, keys)
    h2_a, idx_a, gate_a = _route_first(*route_args)
    coef_a, (h2_b, idx_b, gate_b) = _peer_act(idx_a.T, h2_a, gate_a.T, u_words, (1,) + route_args)
    p_a, (h2_c, idx_c, gate_c) = _peer_out(idx_a.T, coef_a, v_words, (2,) + route_args)
    p_b, _ = _peer_out(idx_b.T, _peer_act(idx_b.T, h2_b, gate_b.T, u_words)[0], v_words)
    p_c, _ = _peer_out(idx_c.T, _peer_act(idx_c.T, h2_c, gate_c.T, u_words)[0], v_words)
    return p_a, p_b, p_c


def _final_kernel(x_ref, pa_ref, pb_ref, pc_ref, g2_ref, n_ref, o_ref):
    o_ref[...] = _rms(x_ref[...] + g2_ref[...] * _pick_third((pa_ref, pb_ref, pc_ref)), n_ref[...])


def _final(x, p, mod_l, final_norm):
    tok = pl.BlockSpec((TM, D_MODEL), lambda i: (i, 0))
    return pl.pallas_call(
        _final_kernel,
        grid=(N_TILES,),
        in_specs=[tok] + _third_specs(D_MODEL) + [_mod_spec(5), _const_spec((1, D_MODEL))],
        out_specs=tok,
        out_shape=jax.ShapeDtypeStruct((N_TOK, D_MODEL), F32),
        name="final_norm",
    )(x, *p, mod_l, final_norm.reshape(1, D_MODEL))


def _rope_tables():
    rows = DEC_SEQ // GRID_W
    row_ids = jnp.repeat(jnp.arange(rows), GRID_W).astype(F32)
    col_ids = jnp.tile(jnp.arange(GRID_W), rows).astype(F32)
    inv_freq = ROPE_BASE ** (-jnp.arange(ROPE_FREQS, dtype=F32) / ROPE_FREQS)
    ang_r = row_ids[:, None] * inv_freq
    ang_c = col_ids[:, None] * inv_freq
    cos = jnp.concatenate([jnp.cos(ang_r), jnp.cos(ang_r), jnp.cos(ang_c), jnp.cos(ang_c)], axis=1)
    sin = jnp.concatenate([-jnp.sin(ang_r), jnp.sin(ang_r), -jnp.sin(ang_c), jnp.sin(ang_c)], axis=1)
    reps = ATT_WIDTH // ATT_HD
    cos = jnp.concatenate([jnp.ones((TM, ATT_WIDTH), F32), jnp.tile(cos, (1, reps))], axis=0)
    sin = jnp.concatenate([jnp.zeros((TM, ATT_WIDTH), F32), jnp.tile(sin, (1, reps))], axis=0)
    return cos, sin


def kernel(x_prompt, x_sample, cache_k, cache_v, state_lru, c, c_ctx, w_mod, b_mod, norm1, norm2, w_in, a_norm, a_ws, a_bs, lru_conv_w, lru_conv_b, lru_w_r, lru_b_r, lru_w_i, lru_b_i, lru_lam, att_lam, att_subln, w_up_a, w_up_b, w_up_c, w_out, peer_wq, peer_keys, peer_u, peer_v, final_norm):
    x = jnp.concatenate([x_prompt.reshape(N_CTX_TOK, D_MODEL), x_sample.reshape(N_LAT_TOK, D_MODEL)], axis=0)
    cond = jnp.concatenate([c_ctx[None, :], c, jnp.zeros((N_COND - 1 - DEC_BATCH, D_MODEL), F32)], axis=0)
    mod = _modulation(cond, w_mod, b_mod).reshape(DEPTH, N_COND, 1, N_MOD * D_MODEL)
    cos_t, sin_t = _rope_tables()
    zero_h0 = jnp.zeros((BATCH, 2, LRU_WIDTH), F32)
    ks, vs, hs = [], [], []
    res = None
    for i in range(DEPTH):
        lam_init = 0.8 - 0.6 * math.exp(-0.3 * i)
        outs = _inproj(x, res, mod[i], norm1[i], w_in[i].astype(BF16), cos_t, sin_t)
        au, av, rx, rg, q, k, kr, v, br = outs[:9]
        if res is not None:
            x = outs[9]
        w_r = _blockdiag128(lru_w_r[i])
        w_i = _blockdiag128(lru_w_i[i])
        lru_args = (lru_conv_w[i], lru_conv_b[i], w_r, lru_b_r[i], w_i, lru_b_i[i], lru_lam[i])
        hs_ctx, hl_ctx = _lru(rx, 0, BATCH, SEQ, *lru_args, zero_h0)
        hs_lat, _ = _lru(rx, N_CTX_TOK, DEC_BATCH, DEC_SEQ, *lru_args, state_lru[:, i])
        o_ctx = _attention(q, kr, v, 0, BATCH, SEQ, None, att_lam[i], att_subln[i], lam_init)
        ctx = (cache_k[:, i].reshape(DEC_BATCH, PAST_LEN, ATT_WIDTH), cache_v[:, i].reshape(DEC_BATCH, PAST_LEN, ATT_WIDTH))
        o_lat = _attention(q, kr, v, N_CTX_TOK, DEC_BATCH, DEC_SEQ, ctx, att_lam[i], att_subln[i], lam_init)
        bias = jnp.repeat(a_bs[i].T, CHUNK, axis=1)
        x = _merge(x, au, av, hs_ctx, hs_lat, rg, o_ctx, o_lat, br, mod[i], a_norm[i], a_ws[i], bias, w_up_a[i].astype(BF16), w_up_b[i].astype(BF16),
                   w_up_c[i].astype(BF16), w_out[i].astype(BF16))
        wq = peer_wq[i].astype(BF16).reshape(D_MODEL, PEER_HEADS, PEER_QDIM).transpose(1, 0, 2)
        p = _peer(x, mod[i], norm2[i], wq, peer_keys[i], _pack_table(peer_u, i), _pack_table(peer_v, i))
        res = (p, mod[i])
        ks.append(k[:N_CTX_TOK].reshape(BATCH, SEQ, ATT_HEADS, ATT_VD))
        vs.append(v[:N_CTX_TOK].reshape(BATCH, SEQ, ATT_HEADS, ATT_VD))
        hs.append(hl_ctx)
    y = _final(x, res[0], res[1], final_norm)
    return (y[:N_CTX_TOK].reshape(BATCH, SEQ, D_MODEL), y[N_CTX_TOK:].reshape(DEC_BATCH, DEC_SEQ, D_MODEL),
            jnp.stack(ks, axis=1), jnp.stack(vs, axis=1), jnp.stack(hs, axis=1))
```

```python
import functools
import math

import jax
import jax.numpy as jnp
from jax import lax
from jax.experimental import pallas as pl
from jax.experimental.pallas import tpu as pltpu

D_MODEL = 1024
BATCH = 16
SEQ = 256
DEPTH = 2
DEC_BATCH = 4
DEC_SEQ = 2048
PAST_LEN = 512
GRID_W = 64
EPS = 1e-6
N_MOD = 6
CHUNK = 128
A_GROUPS = 4
A_WIDTH = 512
LRU_BLOCKS = 8
LRU_BLOCK_W = 64
LRU_WIDTH = 512
CONV_W = 4
LRU_C = 8.0
ATT_HEADS = 4
ATT_HD = 64
ATT_VD = 128
ATT_WIDTH = 512
ROPE_BASE = 10000.0
ROPE_FREQS = 16
N_BRANCH = 3
IN_SPLITS = (512, 1024, 1536, 2048, 2560, 3072, 3584)
IN_WIDTH = 3584 + N_BRANCH * D_MODEL
PEER_HEADS = 8
N_KEYS = 128
N_EXPERTS = N_KEYS * N_KEYS
PEER_QDIM = 256
PEER_HALF = 128
PEER_TOPK = 16
N_PICK = PEER_HEADS * PEER_TOPK
CAND_COLS = tuple(PEER_TOPK // (a + 1) for a in range(PEER_TOPK))
CAND_START = tuple(sum(CAND_COLS[:a]) for a in range(PEER_TOPK))
N_CAND = sum(CAND_COLS)
CAND_PAD = -N_CAND % 8

N_CTX_TOK = BATCH * SEQ
N_LAT_TOK = DEC_BATCH * DEC_SEQ
N_TOK = N_CTX_TOK + N_LAT_TOK
TM = 256
N_TILES = N_TOK // TM
N_THIRD = N_TOK // 3
THIRD_TILES = N_THIRD // TM
CTX_TILES = N_CTX_TOK // TM
LAT_TILES_PER_SEQ = DEC_SEQ // TM
N_COND = 8
TP = 128
TOKEN_UNROLL = 16
PACK_ROWS = 256
ROW_WORDS = D_MODEL // 2
ROW_SUB = ROW_WORDS // 128
PLANE_STRIDE = N_PICK + 8
VMEM_LIMIT = 56 * 1024 * 1024

F32 = jnp.float32
BF16 = jnp.bfloat16
HI = lax.Precision.HIGHEST


def _cond_row(i):
    return jnp.maximum(i - LAT_TILES_PER_SEQ, 0) // LAT_TILES_PER_SEQ


def _pos_block(i):
    return jnp.where(i < CTX_TILES, 0, 1 + i % LAT_TILES_PER_SEQ)


def _mod_spec(chunk):
    return pl.BlockSpec((None, 1, D_MODEL), lambda i: (_cond_row(i), 0, chunk))


def _third_specs(width):
    return [pl.BlockSpec((TM, width), lambda i, k=k: (jnp.clip(i - k * THIRD_TILES, 0, THIRD_TILES - 1), 0))
            for k in range(3)]


def _pick_third(refs):
    i = pl.program_id(0)
    return jnp.where(i < THIRD_TILES, refs[0][...], jnp.where(i < 2 * THIRD_TILES, refs[1][...], refs[2][...]))


def _const_spec(shape):
    nd = len(shape)
    return pl.BlockSpec(shape, lambda *_: (0,) * nd)


def _rms(x, gain):
    return x * lax.rsqrt(jnp.mean(x * x, axis=-1, keepdims=True) + EPS) * gain


def _dot(a, b):
    return jnp.dot(a.astype(BF16), b.astype(BF16), preferred_element_type=F32)


def _dot_nt(a, b):
    return lax.dot_general(a.astype(BF16), b.astype(BF16), (((1,), (1,)), ((), ())), preferred_element_type=F32)


def _mod_kernel(cond_ref, w_ref, b_ref, o_ref):
    cond = cond_ref[...]
    act = cond * jax.nn.sigmoid(cond)
    o_ref[...] = jnp.dot(act, w_ref[...], preferred_element_type=F32, precision=HI) + b_ref[...]


def _modulation(cond, w_mod, b_mod):
    nc = 4
    cw = N_MOD * D_MODEL // nc
    return pl.pallas_call(
        _mod_kernel,
        grid=(DEPTH, nc),
        in_specs=[pl.BlockSpec((N_COND, D_MODEL), lambda l, j: (0, 0)),
                  pl.BlockSpec((None, D_MODEL, cw), lambda l, j: (l, 0, j)),
                  pl.BlockSpec((None, 1, cw), lambda l, j: (l, 0, j))],
        out_specs=pl.BlockSpec((None, N_COND, cw), lambda l, j: (l, 0, j)),
        out_shape=jax.ShapeDtypeStruct((DEPTH, N_COND, N_MOD * D_MODEL), F32),
        compiler_params=pltpu.CompilerParams(vmem_limit_bytes=VMEM_LIMIT),
        name="modulation",
    )(cond, w_mod, b_mod.reshape(DEPTH, 1, N_MOD * D_MODEL))


def _inproj_kernel(has_res, *refs):
    if has_res:
        x_ref, g2_ref = refs[0], refs[4]
        x = x_ref[...] + g2_ref[...] * _pick_third(refs[1:4])
        refs = refs[5:]
    else:
        x_ref = refs[0]
        refs = refs[1:]
        x = x_ref[...]
    (sh_ref, sc_ref, n1_ref, w_ref, cos_ref, sin_ref,
     au_ref, av_ref, rx_ref, rg_ref, q_ref, k_ref, kr_ref, v_ref, br_ref) = refs[:15]
    if has_res:
        refs[15][...] = x
    h = (_rms(x, n1_ref[...]) * (1.0 + sc_ref[...]) + sh_ref[...]).astype(BF16)

    def proj(lo, hi):
        return jnp.dot(h, w_ref[:, lo:hi], preferred_element_type=F32)

    au_ref[...] = proj(0, IN_SPLITS[0])
    av_ref[...] = proj(IN_SPLITS[0], IN_SPLITS[1])
    rx_ref[...] = proj(IN_SPLITS[1], IN_SPLITS[2])
    rg_ref[...] = proj(IN_SPLITS[2], IN_SPLITS[3])
    q = proj(IN_SPLITS[3], IN_SPLITS[4])
    k = proj(IN_SPLITS[4], IN_SPLITS[5])
    v_ref[...] = proj(IN_SPLITS[5], IN_SPLITS[6])
    for j in range(N_BRANCH):
        lo = IN_SPLITS[6] + j * D_MODEL
        br_ref[:, j * D_MODEL:(j + 1) * D_MODEL] = proj(lo, lo + D_MODEL)
    k_ref[...] = k
    lane = lax.broadcasted_iota(jnp.int32, (TM, ATT_WIDTH), 1)
    first = (lane % (2 * ROPE_FREQS)) < ROPE_FREQS
    cos = cos_ref[...]
    sin = sin_ref[...]

    def rot(t):
        partner = jnp.where(first, pltpu.roll(t, ATT_WIDTH - ROPE_FREQS, 1), pltpu.roll(t, ROPE_FREQS, 1))
        return t * cos + partner * sin

    q_ref[...] = rot(q) * (ATT_HD ** -0.5)
    kr_ref[...] = rot(k)


def _inproj(x, res, mod_l, norm1, w_in, cos_t, sin_t):
    has_res = res is not None
    tok = lambda w: pl.BlockSpec((TM, w), lambda i: (i, 0))
    in_specs = [tok(D_MODEL)]
    args = [x]
    if has_res:
        p, mod_prev = res
        in_specs += _third_specs(D_MODEL) + [_mod_spec(5)]
        args += [*p, mod_prev]
    in_specs += [_mod_spec(0), _mod_spec(1), _const_spec((1, D_MODEL)),
                 pl.BlockSpec((D_MODEL, IN_WIDTH), lambda i: (0, 0), pipeline_mode=pl.Buffered(1)),
                 pl.BlockSpec((TM, ATT_WIDTH), lambda i: (_pos_block(i), 0)),
                 pl.BlockSpec((TM, ATT_WIDTH), lambda i: (_pos_block(i), 0))]
    args += [mod_l, mod_l, norm1.reshape(1, D_MODEL), w_in, cos_t, sin_t]
    widths = [512] * 8 + [N_BRANCH * D_MODEL]
    out_specs = [tok(w) for w in widths]
    out_shape = [jax.ShapeDtypeStruct((N_TOK, w), F32) for w in widths]
    if has_res:
        out_specs.append(tok(D_MODEL))
        out_shape.append(jax.ShapeDtypeStruct((N_TOK, D_MODEL), F32))
    return pl.pallas_call(
        functools.partial(_inproj_kernel, has_res),
        grid=(N_TILES,),
        in_specs=in_specs,
        out_specs=out_specs,
        out_shape=out_shape,
        compiler_params=pltpu.CompilerParams(vmem_limit_bytes=VMEM_LIMIT),
        name="inproj",
    )(*args)


def _lru_kernel(seq, x_ref, cw_ref, cb_ref, wr_ref, br_ref, wi_ref, bi_ref, lam_ref, h0_ref, hs_ref, hl_ref):
    x = x_ref[...]
    t = lax.broadcasted_iota(jnp.int32, (seq, 128), 0)

    def shifted(v, k, fill):
        r = pltpu.roll(v, k % seq, 0)
        ok = (t >= k) if k > 0 else (t < seq + k)
        return jnp.where(ok, r, fill)

    cw = cw_ref[...]
    xc = (shifted(x, 2, 0.0) * cw[0:1] + shifted(x, 1, 0.0) * cw[1:2] + x * cw[2:3]
          + shifted(x, -1, 0.0) * cw[3:4] + cb_ref[...])

    def direction(d):
        r = jax.nn.sigmoid(jnp.dot(xc, wr_ref[d, 0], preferred_element_type=F32, precision=HI) + br_ref[d:d + 1])
        g = jax.nn.sigmoid(jnp.dot(xc, wi_ref[d, 0], preferred_element_type=F32, precision=HI) + bi_ref[d:d + 1])
        z = -lam_ref[d:d + 1]
        softplus = jnp.maximum(z, 0.0) + jnp.log(1.0 + jnp.exp(-jnp.abs(z)))
        log_a = -LRU_C * r * softplus
        a = jnp.exp(log_a)
        b = jnp.sqrt(1.0 - jnp.exp(2.0 * log_a)) * (g * xc)
        sgn = 1 if d == 0 else -1
        k = 1
        while k < seq:
            a_prev = shifted(a, sgn * k, 1.0)
            b_prev = shifted(b, sgn * k, 0.0)
            b = a * b_prev + b
            a = a * a_prev
            k *= 2
        return a * h0_ref[0, d:d + 1] + b

    hf = direction(0)
    hb = direction(1)
    hs_ref[...] = hf + hb
    hl_ref[0] = jnp.concatenate([hf[seq - 1:seq], hb[0:1]], axis=0)


def _lru(rx, row0, nseq, seq, conv_w, conv_b, w_r, b_r, w_i, b_i, lam, h0):
    nb = LRU_WIDTH // 128
    blk0 = row0 // seq
    par = lambda shape: pl.BlockSpec(shape, lambda b, c: (0,) * (len(shape) - 1) + (c,))
    return pl.pallas_call(
        functools.partial(_lru_kernel, seq),
        grid=(nseq, nb),
        in_specs=[pl.BlockSpec((seq, 128), lambda b, c: (blk0 + b, c)),
                  par((CONV_W, 128)), par((1, 128)),
                  pl.BlockSpec((2, 1, 128, 128), lambda b, c: (0, c, 0, 0)), par((2, 128)),
                  pl.BlockSpec((2, 1, 128, 128), lambda b, c: (0, c, 0, 0)), par((2, 128)),
                  par((2, 128)),
                  pl.BlockSpec((1, 2, 128), lambda b, c: (b, 0, c))],
        out_specs=[pl.BlockSpec((seq, 128), lambda b, c: (b, c)),
                   pl.BlockSpec((1, 2, 128), lambda b, c: (b, 0, c))],
        out_shape=[jax.ShapeDtypeStruct((nseq * seq, LRU_WIDTH), F32),
                   jax.ShapeDtypeStruct((nseq, 2, LRU_WIDTH), F32)],
        compiler_params=pltpu.CompilerParams(vmem_limit_bytes=VMEM_LIMIT),
        name="rglru",
    )(rx, conv_w, conv_b.reshape(1, LRU_WIDTH), w_r, b_r, w_i, b_i, lam, h0)


def _blockdiag128(w):
    w = w.reshape(2, LRU_BLOCKS // 2, 2, LRU_BLOCK_W, LRU_BLOCK_W)
    z = jnp.zeros_like(w[:, :, 0])
    top = jnp.concatenate([w[:, :, 0], z], axis=-1)
    bot = jnp.concatenate([z, w[:, :, 1]], axis=-1)
    return jnp.concatenate([top, bot], axis=-2)


def _attn_kernel(has_ctx, lam_init, *refs):
    if has_ctx:
        q_ref, k_ref, v_ref, kc_ref, vc_ref, lp_ref, g_ref, o_ref = refs
    else:
        q_ref, k_ref, v_ref, lp_ref, g_ref, o_ref = refs
    lp = lp_ref[...]
    lam = (jnp.exp(jnp.sum(lp[0:1] * lp[1:2], axis=-1, keepdims=True))
           - jnp.exp(jnp.sum(lp[2:3] * lp[3:4], axis=-1, keepdims=True)) + lam_init)
    q = q_ref[...]
    lane = lax.broadcasted_iota(jnp.int32, q.shape, 1)
    halves = (jnp.where(lane < ATT_HD, q, 0.0), jnp.where(lane >= ATT_HD, q, 0.0))
    k = k_ref[...]
    v = v_ref[...]
    w = []
    for qh in halves:
        s = _dot_nt(qh, k)
        m = jnp.max(s, axis=-1, keepdims=True)
        if has_ctx:
            sc = _dot_nt(qh, kc_ref[0])
            m = jnp.maximum(m, jnp.max(sc, axis=-1, keepdims=True))
            ec = jnp.exp(sc - m)
        e = jnp.exp(s - m)
        den = jnp.sum(e, axis=-1, keepdims=True)
        if has_ctx:
            den = den + jnp.sum(ec, axis=-1, keepdims=True)
            w.append((e / den, ec / den))
        else:
            w.append((e / den,))
    o = _dot(w[0][0] - lam * w[1][0], v)
    if has_ctx:
        o = o + _dot(w[0][1] - lam * w[1][1], vc_ref[0])
    o_ref[...] = _rms(o, g_ref[...]) * (1.0 - lam_init)


def _attention(q, kr, v, row0, nseq, seq, ctx, att_lam, subln, lam_init):
    has_ctx = ctx is not None
    nq = seq // TM
    blk0 = row0 // seq
    in_specs = [pl.BlockSpec((TM, ATT_VD), lambda b, h, i: ((row0 // TM) + b * nq + i, h)),
                pl.BlockSpec((seq, ATT_VD), lambda b, h, i: (blk0 + b, h)),
                pl.BlockSpec((seq, ATT_VD), lambda b, h, i: (blk0 + b, h))]
    args = [q, kr, v]
    if has_ctx:
        in_specs += [pl.BlockSpec((1, PAST_LEN, ATT_VD), lambda b, h, i: (b, 0, h))] * 2
        args += list(ctx)
    in_specs += [_const_spec((4, ATT_HD)), _const_spec((1, ATT_VD))]
    args += [att_lam, subln.reshape(1, ATT_VD)]
    return pl.pallas_call(
        functools.partial(_attn_kernel, has_ctx, lam_init),
        grid=(nseq, ATT_HEADS, nq),
        in_specs=in_specs,
        out_specs=pl.BlockSpec((TM, ATT_VD), lambda b, h, i: (b * nq + i, h)),
        out_shape=jax.ShapeDtypeStruct((nseq * seq, ATT_WIDTH), F32),
        compiler_params=pltpu.CompilerParams(vmem_limit_bytes=VMEM_LIMIT),
        name="diffattn",
    )(*args)


def _merge_kernel(x_ref, au_ref, av_ref, hsc_ref, hsl_ref, rg_ref, oc_ref, ol_ref, br_ref, g1_ref, an_ref, ws_ref,
                  bs_ref, wa_ref, wb_ref, wc_ref, wo_ref, xo_ref):
    is_ctx = pl.program_id(0) < CTX_TILES
    hs = jnp.where(is_ctx, hsc_ref[...], hsl_ref[...])
    o = jnp.where(is_ctx, oc_ref[...], ol_ref[...])
    vn = _rms(av_ref[...], an_ref[...])
    rows = []
    for c in range(TM // CHUNK):
        cols = []
        for g in range(A_GROUPS):
            blk = vn[c * CHUNK:(c + 1) * CHUNK, g * 128:(g + 1) * 128]
            cols.append(_dot(ws_ref[g], blk))
        rows.append(jnp.concatenate(cols, axis=1) + bs_ref[...])
    y_a = au_ref[...] * jnp.concatenate(rows, axis=0)
    y_b = hs * jax.nn.gelu(rg_ref[...])
    merged = (jax.nn.sigmoid(br_ref[:, 0:D_MODEL]) * _dot(y_a, wa_ref[...])
              + jax.nn.sigmoid(br_ref[:, D_MODEL:2 * D_MODEL]) * _dot(y_b, wb_ref[...])
              + jax.nn.sigmoid(br_ref[:, 2 * D_MODEL:3 * D_MODEL]) * _dot(o, wc_ref[...]))
    xo_ref[...] = x_ref[...] + g1_ref[...] * _dot(merged, wo_ref[...])


def _merge(x, au, av, hs_ctx, hs_lat, rg, o_ctx, o_lat, br, mod_l, a_norm, a_ws, bias, wa, wb, wc, wo):
    tok = lambda w: pl.BlockSpec((TM, w), lambda i: (i, 0))
    ctx = pl.BlockSpec((TM, 512), lambda i: (jnp.minimum(i, CTX_TILES - 1), 0))
    lat = pl.BlockSpec((TM, 512), lambda i: (jnp.maximum(i - CTX_TILES, 0), 0))
    return pl.pallas_call(
        _merge_kernel,
        grid=(N_TILES,),
        in_specs=[tok(D_MODEL), tok(512), tok(512), ctx, lat, tok(512), ctx, lat, tok(N_BRANCH * D_MODEL),
                  _mod_spec(2), _const_spec((1, A_WIDTH)), _const_spec((A_GROUPS, CHUNK, CHUNK)),
                  _const_spec((CHUNK, A_WIDTH)),
                  _const_spec((A_WIDTH, D_MODEL)), _const_spec((LRU_WIDTH, D_MODEL)),
                  _const_spec((ATT_WIDTH, D_MODEL)), _const_spec((D_MODEL, D_MODEL))],
        out_specs=tok(D_MODEL),
        out_shape=jax.ShapeDtypeStruct((N_TOK, D_MODEL), F32),
        compiler_params=pltpu.CompilerParams(vmem_limit_bytes=VMEM_LIMIT),
        name="merge",
    )(x, au, av, hs_ctx, hs_lat, rg, o_ctx, o_lat, br, mod_l, a_norm.reshape(1, A_WIDTH), a_ws, bias, wa, wb, wc, wo)


def _top16(s, n):
    pos = lax.broadcasted_iota(jnp.int32, s.shape, 0).astype(F32)
    vals, idxs = [], []
    for _ in range(PEER_TOPK):
        m = jnp.max(s, axis=0, keepdims=True)
        am = jnp.min(jnp.where(s == m, pos, float(n)), axis=0, keepdims=True)
        vals.append(m)
        idxs.append(am)
        s = jnp.where(pos == am, -jnp.inf, s)
    return jnp.concatenate(vals, axis=0), jnp.concatenate(idxs, axis=0)


def _take16(table, sel):
    out = jnp.zeros_like(table)
    for a in range(PEER_TOPK):
        out = jnp.where(sel == float(a), table[a:a + 1], out)
    return out


def _route_hidden(x_ref, sh_ref, sc_ref, n2_ref):
    return _rms(x_ref[...], n2_ref[...]) * (1.0 + sc_ref[...]) + sh_ref[...]


def _route_head(h2b, wq_h, keys_ref, idx_ref, gate_ref, rows):
    tn = h2b.shape[0]
    q = jnp.dot(h2b, wq_h, preferred_element_type=F32)
    v1, i1 = _top16(_dot_nt(keys_ref[0], q[:, 0:PEER_HALF]), N_KEYS)
    v2, i2 = _top16(_dot_nt(keys_ref[1], q[:, PEER_HALF:PEER_QDIM]), N_KEYS)
    cand = jnp.concatenate([v1[a:a + 1] + v2[0:n] for a, n in enumerate(CAND_COLS)]
                           + [jnp.full((CAND_PAD, tn), -jnp.inf, F32)], axis=0)
    top_s, pos = _top16(cand, N_CAND + CAND_PAD)
    a_sel = jnp.zeros_like(pos)
    b_sel = pos
    for a in range(1, PEER_TOPK):
        later = pos >= float(CAND_START[a])
        a_sel = a_sel + jnp.where(later, 1.0, 0.0)
        b_sel = b_sel - jnp.where(later, float(CAND_COLS[a - 1]), 0.0)
    expert = _take16(i1, a_sel) * N_KEYS + _take16(i2, b_sel)
    e = jnp.exp(top_s - top_s[0:1])
    gate_ref[rows, :] = e / jnp.sum(e, axis=0, keepdims=True)
    idx_ref[rows, :] = expert.astype(jnp.int32) * ROW_SUB


def _route_kernel(x_ref, sh_ref, sc_ref, n2_ref, wq_ref, keys_ref, h2_ref, idx_ref, gate_ref):
    h2 = _route_hidden(x_ref, sh_ref, sc_ref, n2_ref)
    h2_ref[...] = h2
    h2b = h2.astype(BF16)
    for h in range(PEER_HEADS):
        _route_head(h2b, wq_ref[h], keys_ref, idx_ref, gate_ref, slice(h * PEER_TOPK, (h + 1) * PEER_TOPK))


def _route_out_shape():
    return [jax.ShapeDtypeStruct((N_THIRD, D_MODEL), F32),
            jax.ShapeDtypeStruct((N_PICK, N_THIRD), jnp.int32),
            jax.ShapeDtypeStruct((N_PICK, N_THIRD), F32)]


def _route_first(x, mod_l, norm2, wq, keys):
    tok = pl.BlockSpec((TM, D_MODEL), lambda i: (i, 0))
    pick = pl.BlockSpec((N_PICK, TM), lambda i: (0, i))
    return pl.pallas_call(
        _route_kernel,
        grid=(N_THIRD // TM,),
        in_specs=[tok, _mod_spec(3), _mod_spec(4), _const_spec((1, D_MODEL)),
                  _const_spec((PEER_HEADS, D_MODEL, PEER_QDIM)), _const_spec((2, N_KEYS, PEER_HALF))],
        out_specs=[tok, pick, pick],
        out_shape=_route_out_shape(),
        compiler_params=pltpu.CompilerParams(vmem_limit_bytes=VMEM_LIMIT),
        name="peer_route",
    )(x, mod_l, mod_l, norm2.reshape(1, D_MODEL), wq, keys)


def _route_side_specs(third):
    off = third * (N_THIRD // TP)

    def mod(chunk):
        return pl.BlockSpec((None, 1, D_MODEL), lambda i: (_cond_row((off + i) * TP // TM), 0, chunk))

    in_specs = [pl.BlockSpec((TP, D_MODEL), lambda i: (off + i, 0)), mod(3), mod(4), _const_spec((1, D_MODEL)),
                pl.BlockSpec((PEER_HEADS, D_MODEL, PEER_QDIM), lambda i: (0, 0, 0), pipeline_mode=pl.Buffered(1)),
                _const_spec((2, N_KEYS, PEER_HALF))]
    pick = pl.BlockSpec((N_PICK, TP), lambda i: (0, i))
    out_specs = [pl.BlockSpec((TP, D_MODEL), lambda i: (i, 0)), pick, pick]
    return in_specs, out_specs


def _route_side(in_refs, out_refs, h2b_ref):
    x_ref, sh_ref, sc_ref, n2_ref, wq_ref, keys_ref = in_refs
    h2_ref, idx_ref, gate_ref = out_refs
    h2 = _route_hidden(x_ref, sh_ref, sc_ref, n2_ref)
    h2_ref[...] = h2
    h2b_ref[...] = h2.astype(BF16)

    def side(h):
        rows = pl.ds(pl.multiple_of(h * PEER_TOPK, PEER_TOPK), PEER_TOPK)
        _route_head(h2b_ref[...], wq_ref[h], keys_ref, idx_ref, gate_ref, rows)

    return side


def _pack_kernel(t_ref, o_ref):
    def rounded_bits(v):
        return lax.bitcast_convert_type(v.astype(BF16).astype(F32), jnp.int32)

    lo = lax.shift_right_logical(rounded_bits(t_ref[:, 0:ROW_WORDS]), 16)
    hi = rounded_bits(t_ref[:, ROW_WORDS:D_MODEL]) & jnp.int32(-65536)
    words = lo | hi
    for g in range(PACK_ROWS // 8):
        for s in range(ROW_SUB):
            o_ref[pl.ds(8 * g * ROW_SUB + s, 8, stride=ROW_SUB), :] = words[8 * g:8 * g + 8, s * 128:(s + 1) * 128]


def _pack_table(tabs, layer):
    return pl.pallas_call(
        _pack_kernel,
        grid=(N_EXPERTS // PACK_ROWS,),
        in_specs=[pl.BlockSpec((None, PACK_ROWS, D_MODEL), lambda i: (layer, i, 0))],
        out_specs=pl.BlockSpec((PACK_ROWS * ROW_SUB, 128), lambda i: (i, 0)),
        out_shape=jax.ShapeDtypeStruct((N_EXPERTS * ROW_SUB, 128), jnp.int32),
        name="pack_table",
    )(tabs)


def _unpack(words):
    lo = lax.bitcast_convert_type(words << 16, F32)
    hi = lax.bitcast_convert_type(words & jnp.int32(-65536), F32)
    return lo, hi


def _gather_planes(tab_ref, idx_ref, r, g_ref):
    for k in range(N_PICK):
        row = pl.multiple_of(idx_ref[r, k], ROW_SUB)
        g_ref[pl.ds(k, ROW_SUB, stride=PLANE_STRIDE), :] = tab_ref[pl.ds(row, ROW_SUB), :]


def _plane(g_ref, s):
    return g_ref[s * PLANE_STRIDE:s * PLANE_STRIDE + N_PICK, :]


def _split_bf16(v):
    hi = v.astype(BF16)
    return hi, (v - hi.astype(F32)).astype(BF16)


def _pipelined_tokens(tab_ref, idx_hbm, idx_bufs, sem, bufs, compute, side):
    step = pl.program_id(0)
    groups = TP // TOKEN_UNROLL
    last = pl.num_programs(0) * groups - 1

    def fetch(g, slot):
        rows = pl.ds(jnp.minimum(g, last) * TOKEN_UNROLL, TOKEN_UNROLL)
        return pltpu.make_async_copy(idx_hbm.at[rows], idx_bufs[slot], sem.at[slot])

    @pl.when(step == 0)
    def _():
        fetch(0, 0).start()
        fetch(1, 1).start()
        fetch(0, 0).wait()
        _gather_planes(tab_ref, idx_bufs[0], 0, bufs[0])

    def group_pair(jj, carry):
        for slot in range(2):
            j = 2 * jj + slot
            g = step * groups + j
            if side is not None:
                side(j)
            for p in range(TOKEN_UNROLL):
                t = TOKEN_UNROLL * j + p
                compute(t, pl.multiple_of(TOKEN_UNROLL * j + p // 8 * 8, 8), p % 8, bufs[p % 2])
                if p + 1 < TOKEN_UNROLL:
                    _gather_planes(tab_ref, idx_bufs[slot], p + 1, bufs[(p + 1) % 2])
                else:
                    fetch(g + 1, 1 - slot).wait()
                    _gather_planes(tab_ref, idx_bufs[1 - slot], 0, bufs[0])
            fetch(g + 2, slot).start()
        return carry

    lax.fori_loop(0, groups // 2, group_pair, 0)

    @pl.when(step == pl.num_programs(0) - 1)
    def _():
        fetch(last, 1).wait()


def _split_refs(with_route, refs, n_in, n_scratch):
    n_rin, n_rout = (6, 3) if with_route else (0, 0)
    cuts = [n_in, n_rin, 1, n_rout, n_scratch]
    parts, at = [], 0
    for n in cuts:
        parts.append(refs[at:at + n])
        at += n
    parts.append(refs[at:])
    return parts


def _peer_act_kernel(with_route, *refs):
    (idx_hbm, x_ref, gate_ref, tab_ref), rin, (o_ref,), rout, (ga_ref, gb_ref, ia_ref, ib_ref, sem), rscratch = (
        _split_refs(with_route, refs, 4, 5))
    side = _route_side(rin, rout, rscratch[0]) if with_route else None
    ones = jnp.ones((8, 2 * 128), BF16)

    def token(t, t8, r, g_ref):
        def x_row(j):
            return x_ref[pl.ds(t8, 8), j * 128:(j + 1) * 128][r:r + 1]

        acc = jnp.zeros((N_PICK, 128), F32)
        for s in range(ROW_SUB):
            lo, hi = _unpack(_plane(g_ref, s))
            acc = acc + lo * x_row(s) + hi * x_row(ROW_SUB + s)
        act = _dot_nt(ones, jnp.concatenate(_split_bf16(acc), axis=1))
        o_ref[pl.ds(t, 1), :] = jax.nn.gelu(act[0:1]) * gate_ref[pl.ds(t, 1), :]

    _pipelined_tokens(tab_ref, idx_hbm, (ia_ref, ib_ref), sem, (ga_ref, gb_ref), token, side)


def _peer_out_kernel(with_route, *refs):
    (idx_hbm, coef_ref, tab_ref), rin, (o_ref,), rout, (ga_ref, gb_ref, st_ref, ia_ref, ib_ref, sem), rscratch = (
        _split_refs(with_route, refs, 3, 6))
    side = _route_side(rin, rout, rscratch[0]) if with_route else None

    def token(t, t8, r, g_ref):
        c_hi, c_lo = _split_bf16(coef_ref[t])
        lhs = jnp.concatenate([c_hi, c_lo, jnp.zeros((4, 2 * N_PICK), BF16)], axis=0)
        for s in range(ROW_SUB):
            w = pltpu.bitcast(_plane(g_ref, s), BF16)
            acc = jnp.dot(lhs, w, preferred_element_type=F32)
            st_ref[8 * s + r:8 * s + r + 1, :] = acc[0:1] + acc[2:3]
            st_ref[8 * (ROW_SUB + s) + r:8 * (ROW_SUB + s) + r + 1, :] = acc[1:2] + acc[3:4]
        if r == 7:
            for j in range(D_MODEL // 128):
                o_ref[pl.ds(t8, 8), j * 128:(j + 1) * 128] = st_ref[8 * j:8 * j + 8, :]

    _pipelined_tokens(tab_ref, idx_hbm, (ia_ref, ib_ref), sem, (ga_ref, gb_ref), token, side)


def _expert_call(body, name, in_specs, args, out_spec, out_width, scratch, side):
    out_specs, out_shape = [out_spec], [jax.ShapeDtypeStruct((N_THIRD, out_width), F32)]
    if side is not None:
        assert TP // TOKEN_UNROLL == PEER_HEADS
        third, x, mod_l, norm2, wq, keys = side
        rin, rout = _route_side_specs(third)
        in_specs = in_specs + rin
        args = args + (x, mod_l, mod_l, norm2.reshape(1, D_MODEL), wq, keys)
        out_specs = out_specs + rout
        out_shape = out_shape + _route_out_shape()
        scratch = scratch + [pltpu.VMEM((TP, D_MODEL), BF16)]
    outs = pl.pallas_call(
        functools.partial(body, side is not None),
        grid=(N_THIRD // TP,),
        in_specs=in_specs,
        out_specs=out_specs,
        out_shape=out_shape,
        scratch_shapes=scratch,
        compiler_params=pltpu.CompilerParams(vmem_limit_bytes=VMEM_LIMIT),
        name=name,
    )(*args)
    return outs[0], tuple(outs[1:])


_PICKS_HBM = pl.BlockSpec(memory_space=pl.ANY)
_TABLE = pl.BlockSpec((N_EXPERTS * ROW_SUB, 128), lambda i: (0, 0), pipeline_mode=pl.Buffered(1))
_PLANES = pltpu.VMEM((ROW_SUB * PLANE_STRIDE, 128), jnp.int32)
_PICK_BUF = pltpu.SMEM((TOKEN_UNROLL, N_PICK), jnp.int32)
_PICK_SCRATCH = [_PICK_BUF, _PICK_BUF, pltpu.SemaphoreType.DMA((2,))]


def _peer_act(idx, h2, gate, u_words, side=None):
    rows = pl.BlockSpec((TP, D_MODEL), lambda i: (i, 0))
    pick = pl.BlockSpec((TP, N_PICK), lambda i: (i, 0))
    coef, routed = _expert_call(_peer_act_kernel, "peer_act", [_PICKS_HBM, rows, pick, _TABLE],
                                (idx, h2, gate, u_words), pick, N_PICK, [_PLANES, _PLANES] + _PICK_SCRATCH, side)
    zero = jnp.zeros_like(coef)
    coef = jnp.stack([jnp.stack([coef, zero], axis=-1), jnp.stack([zero, coef], axis=-1)], axis=1)
    return coef.reshape(N_THIRD, 2, 2 * N_PICK), routed


def _peer_out(idx, coef, v_words, side=None):
    rows = pl.BlockSpec((TP, D_MODEL), lambda i: (i, 0))
    pair = pl.BlockSpec((TP, 2, 2 * N_PICK), lambda i: (i, 0, 0))
    stage = pltpu.VMEM((8 * (D_MODEL // 128), 128), F32)
    return _expert_call(_peer_out_kernel, "peer_out", [_PICKS_HBM, pair, _TABLE], (idx, coef, v_words),
                        rows, D_MODEL, [_PLANES, _PLANES, stage] + _PICK_SCRATCH, side)


def _peer(x, mod_l, norm2, wq, keys, u_words, v_words):
    route_args = (x, mod_l, norm2, wq, keys)
    h2, idx, gate = _route_first(*route_args)
    outs = []
    for third in range(3):
        coef, _ = _peer_act(idx.T, h2, gate.T, u_words)
        side = (third + 1,) + route_args if third < 2 else None
        p, routed = _peer_out(idx.T, coef, v_words, side)
        outs.append(p)
        if side is not None:
            h2, idx, gate = routed
    return tuple(outs)


def _final_kernel(x_ref, pa_ref, pb_ref, pc_ref, g2_ref, n_ref, o_ref):
    o_ref[...] = _rms(x_ref[...] + g2_ref[...] * _pick_third((pa_ref, pb_ref, pc_ref)), n_ref[...])


def _final(x, p, mod_l, final_norm):
    tok = pl.BlockSpec((TM, D_MODEL), lambda i: (i, 0))
    return pl.pallas_call(
        _final_kernel,
        grid=(N_TILES,),
        in_specs=[tok] + _third_specs(D_MODEL) + [_mod_spec(5), _const_spec((1, D_MODEL))],
        out_specs=tok,
        out_shape=jax.ShapeDtypeStruct((N_TOK, D_MODEL), F32),
        name="final_norm",
    )(x, *p, mod_l, final_norm.reshape(1, D_MODEL))


def _rope_tables():
    rows = DEC_SEQ // GRID_W
    row_ids = jnp.repeat(jnp.arange(rows), GRID_W).astype(F32)
    col_ids = jnp.tile(jnp.arange(GRID_W), rows).astype(F32)
    inv_freq = ROPE_BASE ** (-jnp.arange(ROPE_FREQS, dtype=F32) / ROPE_FREQS)
    ang_r = row_ids[:, None] * inv_freq
    ang_c = col_ids[:, None] * inv_freq
    cos = jnp.concatenate([jnp.cos(ang_r), jnp.cos(ang_r), jnp.cos(ang_c), jnp.cos(ang_c)], axis=1)
    sin = jnp.concatenate([-jnp.sin(ang_r), jnp.sin(ang_r), -jnp.sin(ang_c), jnp.sin(ang_c)], axis=1)
    reps = ATT_WIDTH // ATT_HD
    cos = jnp.concatenate([jnp.ones((TM, ATT_WIDTH), F32), jnp.tile(cos, (1, reps))], axis=0)
    sin = jnp.concatenate([jnp.zeros((TM, ATT_WIDTH), F32), jnp.tile(sin, (1, reps))], axis=0)
    return cos, sin


def kernel(x_prompt, x_sample, cache_k, cache_v, state_lru, c, c_ctx, w_mod, b_mod, norm1, norm2, w_in, a_norm, a_ws, a_bs, lru_conv_w, lru_conv_b, lru_w_r, lru_b_r, lru_w_i, lru_b_i, lru_lam, att_lam, att_subln, w_up_a, w_up_b, w_up_c, w_out, peer_wq, peer_keys, peer_u, peer_v, final_norm):
    x = jnp.concatenate([x_prompt.reshape(N_CTX_TOK, D_MODEL), x_sample.reshape(N_LAT_TOK, D_MODEL)], axis=0)
    cond = jnp.concatenate([c_ctx[None, :], c, jnp.zeros((N_COND - 1 - DEC_BATCH, D_MODEL), F32)], axis=0)
    mod = _modulation(cond, w_mod, b_mod).reshape(DEPTH, N_COND, 1, N_MOD * D_MODEL)
    cos_t, sin_t = _rope_tables()
    zero_h0 = jnp.zeros((BATCH, 2, LRU_WIDTH), F32)
    ks, vs, hs = [], [], []
    res = None
    for i in range(DEPTH):
        lam_init = 0.8 - 0.6 * math.exp(-0.3 * i)
        outs = _inproj(x, res, mod[i], norm1[i], w_in[i].astype(BF16), cos_t, sin_t)
        au, av, rx, rg, q, k, kr, v, br = outs[:9]
        if res is not None:
            x = outs[9]
        w_r = _blockdiag128(lru_w_r[i])
        w_i = _blockdiag128(lru_w_i[i])
        lru_args = (lru_conv_w[i], lru_conv_b[i], w_r, lru_b_r[i], w_i, lru_b_i[i], lru_lam[i])
        hs_ctx, hl_ctx = _lru(rx, 0, BATCH, SEQ, *lru_args, zero_h0)
        hs_lat, _ = _lru(rx, N_CTX_TOK, DEC_BATCH, DEC_SEQ, *lru_args, state_lru[:, i])
        o_ctx = _attention(q, kr, v, 0, BATCH, SEQ, None, att_lam[i], att_subln[i], lam_init)
        ctx = (cache_k[:, i].reshape(DEC_BATCH, PAST_LEN, ATT_WIDTH), cache_v[:, i].reshape(DEC_BATCH, PAST_LEN, ATT_WIDTH))
        o_lat = _attention(q, kr, v, N_CTX_TOK, DEC_BATCH, DEC_SEQ, ctx, att_lam[i], att_subln[i], lam_init)
        bias = jnp.repeat(a_bs[i].T, CHUNK, axis=1)
        x = _merge(x, au, av, hs_ctx, hs_lat, rg, o_ctx, o_lat, br, mod[i], a_norm[i], a_ws[i], bias, w_up_a[i].astype(BF16), w_up_b[i].astype(BF16),
                   w_up_c[i].astype(BF16), w_out[i].astype(BF16))
        wq = peer_wq[i].astype(BF16).reshape(D_MODEL, PEER_HEADS, PEER_QDIM).transpose(1, 0, 2)
        p = _peer(x, mod[i], norm2[i], wq, peer_keys[i], _pack_table(peer_u, i), _pack_table(peer_v, i))
        res = (p, mod[i])
        ks.append(k[:N_CTX_TOK].reshape(BATCH, SEQ, ATT_HEADS, ATT_VD))
        vs.append(v[:N_CTX_TOK].reshape(BATCH, SEQ, ATT_HEADS, ATT_VD))
        hs.append(hl_ctx)
    y = _final(x, res[0], res[1], final_norm)
    return (y[:N_CTX_TOK].reshape(BATCH, SEQ, D_MODEL), y[N_CTX_TOK:].reshape(DEC_BATCH, DEC_SEQ, D_MODEL),
            jnp.stack(ks, axis=1), jnp.stack(vs, axis=1), jnp.stack(hs, axis=1))
```

```python
import functools
import math

import jax
import jax.numpy as jnp
from jax import lax
from jax.experimental import pallas as pl
from jax.experimental.pallas import tpu as pltpu

D_MODEL = 1024
BATCH = 16
SEQ = 256
DEPTH = 2
DEC_BATCH = 4
DEC_SEQ = 2048
PAST_LEN = 512
GRID_W = 64
EPS = 1e-6
N_MOD = 6
CHUNK = 128
A_GROUPS = 4
A_WIDTH = 512
LRU_BLOCKS = 8
LRU_BLOCK_W = 64
LRU_WIDTH = 512
CONV_W = 4
LRU_C = 8.0
ATT_HEADS = 4
ATT_HD = 64
ATT_VD = 128
ATT_WIDTH = 512
ROPE_BASE = 10000.0
ROPE_FREQS = 16
N_BRANCH = 3
IN_SPLITS = (512, 1024, 1536, 2048, 2560, 3072, 3584)
IN_WIDTH = 3584 + N_BRANCH * D_MODEL
PEER_HEADS = 8
N_KEYS = 128
N_EXPERTS = N_KEYS * N_KEYS
PEER_QDIM = 256
PEER_HALF = 128
PEER_TOPK = 16
N_PICK = PEER_HEADS * PEER_TOPK
CAND_COLS = tuple(PEER_TOPK // (a + 1) for a in range(PEER_TOPK))
CAND_START = tuple(sum(CAND_COLS[:a]) for a in range(PEER_TOPK))
N_CAND = sum(CAND_COLS)
CAND_PAD = -N_CAND % 8

N_CTX_TOK = BATCH * SEQ
N_LAT_TOK = DEC_BATCH * DEC_SEQ
N_TOK = N_CTX_TOK + N_LAT_TOK
TM = 256
N_TILES = N_TOK // TM
N_THIRD = N_TOK // 3
THIRD_TILES = N_THIRD // TM
CTX_TILES = N_CTX_TOK // TM
LAT_TILES_PER_SEQ = DEC_SEQ // TM
N_COND = 8
TP = 128
TOKEN_UNROLL = 16
PACK_ROWS = 256
ROW_WORDS = D_MODEL // 2
ROW_SUB = ROW_WORDS // 128
PLANE_STRIDE = N_PICK + 8
VMEM_LIMIT = 56 * 1024 * 1024

F32 = jnp.float32
BF16 = jnp.bfloat16
HI = lax.Precision.HIGHEST


def _cond_row(i):
    return jnp.maximum(i - LAT_TILES_PER_SEQ, 0) // LAT_TILES_PER_SEQ


def _pos_block(i):
    return jnp.where(i < CTX_TILES, 0, 1 + i % LAT_TILES_PER_SEQ)


def _mod_spec(chunk):
    return pl.BlockSpec((None, 1, D_MODEL), lambda i: (_cond_row(i), 0, chunk))


def _third_specs(width):
    return [pl.BlockSpec((TM, width), lambda i, k=k: (jnp.clip(i - k * THIRD_TILES, 0, THIRD_TILES - 1), 0))
            for k in range(3)]


def _pick_third(refs):
    i = pl.program_id(0)
    return jnp.where(i < THIRD_TILES, refs[0][...], jnp.where(i < 2 * THIRD_TILES, refs[1][...], refs[2][...]))


def _const_spec(shape):
    nd = len(shape)
    return pl.BlockSpec(shape, lambda *_: (0,) * nd)


def _rms(x, gain):
    return x * lax.rsqrt(jnp.mean(x * x, axis=-1, keepdims=True) + EPS) * gain


def _dot(a, b):
    return jnp.dot(a.astype(BF16), b.astype(BF16), preferred_element_type=F32)


def _dot_nt(a, b):
    return lax.dot_general(a.astype(BF16), b.astype(BF16), (((1,), (1,)), ((), ())), preferred_element_type=F32)


def _mod_kernel(cond_ref, w_ref, b_ref, o_ref):
    cond = cond_ref[...]
    act = cond * jax.nn.sigmoid(cond)
    o_ref[...] = jnp.dot(act, w_ref[...], preferred_element_type=F32, precision=HI) + b_ref[...]


def _modulation(cond, w_mod, b_mod):
    nc = 4
    cw = N_MOD * D_MODEL // nc
    return pl.pallas_call(
        _mod_kernel,
        grid=(DEPTH, nc),
        in_specs=[pl.BlockSpec((N_COND, D_MODEL), lambda l, j: (0, 0)),
                  pl.BlockSpec((None, D_MODEL, cw), lambda l, j: (l, 0, j)),
                  pl.BlockSpec((None, 1, cw), lambda l, j: (l, 0, j))],
        out_specs=pl.BlockSpec((None, N_COND, cw), lambda l, j: (l, 0, j)),
        out_shape=jax.ShapeDtypeStruct((DEPTH, N_COND, N_MOD * D_MODEL), F32),
        compiler_params=pltpu.CompilerParams(vmem_limit_bytes=VMEM_LIMIT),
        name="modulation",
    )(cond, w_mod, b_mod.reshape(DEPTH, 1, N_MOD * D_MODEL))


def _inproj_kernel(has_res, *refs):
    if has_res:
        x_ref, g2_ref = refs[0], refs[4]
        x = x_ref[...] + g2_ref[...] * _pick_third(refs[1:4])
        refs = refs[5:]
    else:
        x_ref = refs[0]
        refs = refs[1:]
        x = x_ref[...]
    (sh_ref, sc_ref, n1_ref, w_ref, cos_ref, sin_ref,
     au_ref, av_ref, rx_ref, rg_ref, q_ref, k_ref, kr_ref, v_ref, br_ref) = refs[:15]
    if has_res:
        refs[15][...] = x
    h = (_rms(x, n1_ref[...]) * (1.0 + sc_ref[...]) + sh_ref[...]).astype(BF16)

    def proj(lo, hi):
        return jnp.dot(h, w_ref[:, lo:hi], preferred_element_type=F32)

    au_ref[...] = proj(0, IN_SPLITS[0])
    av_ref[...] = proj(IN_SPLITS[0], IN_SPLITS[1])
    rx_ref[...] = proj(IN_SPLITS[1], IN_SPLITS[2])
    rg_ref[...] = proj(IN_SPLITS[2], IN_SPLITS[3])
    q = proj(IN_SPLITS[3], IN_SPLITS[4])
    k = proj(IN_SPLITS[4], IN_SPLITS[5])
    v_ref[...] = proj(IN_SPLITS[5], IN_SPLITS[6])
    for j in range(N_BRANCH):
        lo = IN_SPLITS[6] + j * D_MODEL
        br_ref[:, j * D_MODEL:(j + 1) * D_MODEL] = proj(lo, lo + D_MODEL)
    k_ref[...] = k
    lane = lax.broadcasted_iota(jnp.int32, (TM, ATT_WIDTH), 1)
    first = (lane % (2 * ROPE_FREQS)) < ROPE_FREQS
    cos = cos_ref[...]
    sin = sin_ref[...]

    def rot(t):
        partner = jnp.where(first, pltpu.roll(t, ATT_WIDTH - ROPE_FREQS, 1), pltpu.roll(t, ROPE_FREQS, 1))
        return t * cos + partner * sin

    q_ref[...] = rot(q) * (ATT_HD ** -0.5)
    kr_ref[...] = rot(k)


def _inproj(x, res, mod_l, norm1, w_in, cos_t, sin_t):
    has_res = res is not None
    tok = lambda w: pl.BlockSpec((TM, w), lambda i: (i, 0))
    in_specs = [tok(D_MODEL)]
    args = [x]
    if has_res:
        p, mod_prev = res
        in_specs += _third_specs(D_MODEL) + [_mod_spec(5)]
        args += [*p, mod_prev]
    in_specs += [_mod_spec(0), _mod_spec(1), _const_spec((1, D_MODEL)),
                 pl.BlockSpec((D_MODEL, IN_WIDTH), lambda i: (0, 0), pipeline_mode=pl.Buffered(1)),
                 pl.BlockSpec((TM, ATT_WIDTH), lambda i: (_pos_block(i), 0)),
                 pl.BlockSpec((TM, ATT_WIDTH), lambda i: (_pos_block(i), 0))]
    args += [mod_l, mod_l, norm1.reshape(1, D_MODEL), w_in, cos_t, sin_t]
    widths = [512] * 8 + [N_BRANCH * D_MODEL]
    out_specs = [tok(w) for w in widths]
    out_shape = [jax.ShapeDtypeStruct((N_TOK, w), F32) for w in widths]
    if has_res:
        out_specs.append(tok(D_MODEL))
        out_shape.append(jax.ShapeDtypeStruct((N_TOK, D_MODEL), F32))
    return pl.pallas_call(
        functools.partial(_inproj_kernel, has_res),
        grid=(N_TILES,),
        in_specs=in_specs,
        out_specs=out_specs,
        out_shape=out_shape,
        compiler_params=pltpu.CompilerParams(vmem_limit_bytes=VMEM_LIMIT),
        name="inproj",
    )(*args)


def _lru_kernel(seq, x_ref, cw_ref, cb_ref, wr_ref, br_ref, wi_ref, bi_ref, lam_ref, h0_ref, hs_ref, hl_ref):
    x = x_ref[...]
    t = lax.broadcasted_iota(jnp.int32, (seq, 128), 0)

    def shifted(v, k, fill):
        r = pltpu.roll(v, k % seq, 0)
        ok = (t >= k) if k > 0 else (t < seq + k)
        return jnp.where(ok, r, fill)

    cw = cw_ref[...]
    xc = (shifted(x, 2, 0.0) * cw[0:1] + shifted(x, 1, 0.0) * cw[1:2] + x * cw[2:3]
          + shifted(x, -1, 0.0) * cw[3:4] + cb_ref[...])

    xc_hi, xc_lo = _split_bf16(xc)

    def gate_dot(w):
        w_hi, w_lo = _split_bf16(w)
        return (jnp.dot(xc_hi, w_hi, preferred_element_type=F32) + jnp.dot(xc_lo, w_hi, preferred_element_type=F32)
                + jnp.dot(xc_hi, w_lo, preferred_element_type=F32))

    def direction(d):
        r = jax.nn.sigmoid(gate_dot(wr_ref[d, 0]) + br_ref[d:d + 1])
        g = jax.nn.sigmoid(gate_dot(wi_ref[d, 0]) + bi_ref[d:d + 1])
        z = -lam_ref[d:d + 1]
        softplus = jnp.maximum(z, 0.0) + jnp.log(1.0 + jnp.exp(-jnp.abs(z)))
        log_a = -LRU_C * r * softplus
        a = jnp.exp(log_a)
        b = jnp.sqrt(1.0 - jnp.exp(2.0 * log_a)) * (g * xc)
        sgn = 1 if d == 0 else -1
        local = t % 8
        for k in (1, 2, 4):
            inside = (local >= k) if d == 0 else (local < 8 - k)
            a_prev = jnp.where(inside, pltpu.roll(a, (sgn * k) % seq, 0), 1.0)
            b_prev = jnp.where(inside, pltpu.roll(b, (sgn * k) % seq, 0), 0.0)
            b = a * b_prev + b
            a = a * a_prev
        edge = 7 if d == 0 else 0
        h = h0_ref[0, d:d + 1]
        tiles = range(seq // 8) if d == 0 else reversed(range(seq // 8))
        for j in tiles:
            rows = slice(8 * j, 8 * j + 8)
            h_tile = a[rows] * h + b[rows]
            if d == 0:
                hs_ref[rows, :] = h_tile
            else:
                hs_ref[rows, :] += h_tile
            h = h_tile[edge:edge + 1]
        return h

    h_f = direction(0)
    h_b = direction(1)
    hl_ref[0] = jnp.concatenate([h_f, h_b], axis=0)


def _lru(rx, row0, nseq, seq, conv_w, conv_b, w_r, b_r, w_i, b_i, lam, h0):
    nb = LRU_WIDTH // 128
    blk0 = row0 // seq
    par = lambda shape: pl.BlockSpec(shape, lambda b, c: (0,) * (len(shape) - 1) + (c,))
    return pl.pallas_call(
        functools.partial(_lru_kernel, seq),
        grid=(nseq, nb),
        in_specs=[pl.BlockSpec((seq, 128), lambda b, c: (blk0 + b, c)),
                  par((CONV_W, 128)), par((1, 128)),
                  pl.BlockSpec((2, 1, 128, 128), lambda b, c: (0, c, 0, 0)), par((2, 128)),
                  pl.BlockSpec((2, 1, 128, 128), lambda b, c: (0, c, 0, 0)), par((2, 128)),
                  par((2, 128)),
                  pl.BlockSpec((1, 2, 128), lambda b, c: (b, 0, c))],
        out_specs=[pl.BlockSpec((seq, 128), lambda b, c: (b, c)),
                   pl.BlockSpec((1, 2, 128), lambda b, c: (b, 0, c))],
        out_shape=[jax.ShapeDtypeStruct((nseq * seq, LRU_WIDTH), F32),
                   jax.ShapeDtypeStruct((nseq, 2, LRU_WIDTH), F32)],
        compiler_params=pltpu.CompilerParams(vmem_limit_bytes=VMEM_LIMIT),
        name="rglru",
    )(rx, conv_w, conv_b.reshape(1, LRU_WIDTH), w_r, b_r, w_i, b_i, lam, h0)


def _blockdiag128(w):
    w = w.reshape(2, LRU_BLOCKS // 2, 2, LRU_BLOCK_W, LRU_BLOCK_W)
    z = jnp.zeros_like(w[:, :, 0])
    top = jnp.concatenate([w[:, :, 0], z], axis=-1)
    bot = jnp.concatenate([z, w[:, :, 1]], axis=-1)
    return jnp.concatenate([top, bot], axis=-2)


def _attn_kernel(has_ctx, lam_init, *refs):
    if has_ctx:
        q_ref, k_ref, v_ref, kc_ref, vc_ref, lp_ref, g_ref, o_ref = refs
    else:
        q_ref, k_ref, v_ref, lp_ref, g_ref, o_ref = refs
    lp = lp_ref[...]
    lam = (jnp.exp(jnp.sum(lp[0:1] * lp[1:2], axis=-1, keepdims=True))
           - jnp.exp(jnp.sum(lp[2:3] * lp[3:4], axis=-1, keepdims=True)) + lam_init)
    q = q_ref[...]
    lane = lax.broadcasted_iota(jnp.int32, q.shape, 1)
    halves = (jnp.where(lane < ATT_HD, q, 0.0), jnp.where(lane >= ATT_HD, q, 0.0))
    k = k_ref[...]
    v = v_ref[...]
    w = []
    for qh in halves:
        s = _dot_nt(qh, k)
        m = jnp.max(s, axis=-1, keepdims=True)
        if has_ctx:
            sc = _dot_nt(qh, kc_ref[0])
            m = jnp.maximum(m, jnp.max(sc, axis=-1, keepdims=True))
            ec = jnp.exp(sc - m)
        e = jnp.exp(s - m)
        den = jnp.sum(e, axis=-1, keepdims=True)
        if has_ctx:
            den = den + jnp.sum(ec, axis=-1, keepdims=True)
            w.append((e / den, ec / den))
        else:
            w.append((e / den,))
    o = _dot(w[0][0] - lam * w[1][0], v)
    if has_ctx:
        o = o + _dot(w[0][1] - lam * w[1][1], vc_ref[0])
    o_ref[...] = _rms(o, g_ref[...]) * (1.0 - lam_init)


def _attention(q, kr, v, row0, nseq, seq, ctx, att_lam, subln, lam_init):
    has_ctx = ctx is not None
    nq = seq // TM
    blk0 = row0 // seq
    in_specs = [pl.BlockSpec((TM, ATT_VD), lambda b, h, i: ((row0 // TM) + b * nq + i, h)),
                pl.BlockSpec((seq, ATT_VD), lambda b, h, i: (blk0 + b, h)),
                pl.BlockSpec((seq, ATT_VD), lambda b, h, i: (blk0 + b, h))]
    args = [q, kr, v]
    if has_ctx:
        in_specs += [pl.BlockSpec((1, PAST_LEN, ATT_VD), lambda b, h, i: (b, 0, h))] * 2
        args += list(ctx)
    in_specs += [_const_spec((4, ATT_HD)), _const_spec((1, ATT_VD))]
    args += [att_lam, subln.reshape(1, ATT_VD)]
    return pl.pallas_call(
        functools.partial(_attn_kernel, has_ctx, lam_init),
        grid=(nseq, ATT_HEADS, nq),
        in_specs=in_specs,
        out_specs=pl.BlockSpec((TM, ATT_VD), lambda b, h, i: (b * nq + i, h)),
        out_shape=jax.ShapeDtypeStruct((nseq * seq, ATT_WIDTH), F32),
        compiler_params=pltpu.CompilerParams(vmem_limit_bytes=VMEM_LIMIT),
        name="diffattn",
    )(*args)


def _merge_kernel(x_ref, au_ref, av_ref, hsc_ref, hsl_ref, rg_ref, oc_ref, ol_ref, br_ref, g1_ref, an_ref, ws_ref,
                  bs_ref, wa_ref, wb_ref, wc_ref, wo_ref, xo_ref):
    is_ctx = pl.program_id(0) < CTX_TILES
    hs = jnp.where(is_ctx, hsc_ref[...], hsl_ref[...])
    o = jnp.where(is_ctx, oc_ref[...], ol_ref[...])
    vn = _rms(av_ref[...], an_ref[...])
    rows = []
    for c in range(TM // CHUNK):
        cols = []
        for g in range(A_GROUPS):
            blk = vn[c * CHUNK:(c + 1) * CHUNK, g * 128:(g + 1) * 128]
            cols.append(_dot(ws_ref[g], blk))
        rows.append(jnp.concatenate(cols, axis=1) + bs_ref[...])
    y_a = au_ref[...] * jnp.concatenate(rows, axis=0)
    y_b = hs * jax.nn.gelu(rg_ref[...])
    merged = (jax.nn.sigmoid(br_ref[:, 0:D_MODEL]) * _dot(y_a, wa_ref[...])
              + jax.nn.sigmoid(br_ref[:, D_MODEL:2 * D_MODEL]) * _dot(y_b, wb_ref[...])
              + jax.nn.sigmoid(br_ref[:, 2 * D_MODEL:3 * D_MODEL]) * _dot(o, wc_ref[...]))
    xo_ref[...] = x_ref[...] + g1_ref[...] * _dot(merged, wo_ref[...])


def _merge(x, au, av, hs_ctx, hs_lat, rg, o_ctx, o_lat, br, mod_l, a_norm, a_ws, bias, wa, wb, wc, wo):
    tok = lambda w: pl.BlockSpec((TM, w), lambda i: (i, 0))
    ctx = pl.BlockSpec((TM, 512), lambda i: (jnp.minimum(i, CTX_TILES - 1), 0))
    lat = pl.BlockSpec((TM, 512), lambda i: (jnp.maximum(i - CTX_TILES, 0), 0))
    return pl.pallas_call(
        _merge_kernel,
        grid=(N_TILES,),
        in_specs=[tok(D_MODEL), tok(512), tok(512), ctx, lat, tok(512), ctx, lat, tok(N_BRANCH * D_MODEL),
                  _mod_spec(2), _const_spec((1, A_WIDTH)), _const_spec((A_GROUPS, CHUNK, CHUNK)),
                  _const_spec((CHUNK, A_WIDTH)),
                  _const_spec((A_WIDTH, D_MODEL)), _const_spec((LRU_WIDTH, D_MODEL)),
                  _const_spec((ATT_WIDTH, D_MODEL)), _const_spec((D_MODEL, D_MODEL))],
        out_specs=tok(D_MODEL),
        out_shape=jax.ShapeDtypeStruct((N_TOK, D_MODEL), F32),
        compiler_params=pltpu.CompilerParams(vmem_limit_bytes=VMEM_LIMIT),
        name="merge",
    )(x, au, av, hs_ctx, hs_lat, rg, o_ctx, o_lat, br, mod_l, a_norm.reshape(1, A_WIDTH), a_ws, bias, wa, wb, wc, wo)


def _top16(s, n):
    pos = lax.broadcasted_iota(jnp.int32, s.shape, 0).astype(F32)
    vals, idxs = [], []
    for _ in range(PEER_TOPK):
        m = jnp.max(s, axis=0, keepdims=True)
        am = jnp.min(jnp.where(s == m, pos, float(n)), axis=0, keepdims=True)
        vals.append(m)
        idxs.append(am)
        s = jnp.where(pos == am, -jnp.inf, s)
    return jnp.concatenate(vals, axis=0), jnp.concatenate(idxs, axis=0)


def _take16(table, sel):
    out = jnp.zeros_like(table)
    for a in range(PEER_TOPK):
        out = jnp.where(sel == float(a), table[a:a + 1], out)
    return out


def _route_hidden(x_ref, sh_ref, sc_ref, n2_ref):
    return _rms(x_ref[...], n2_ref[...]) * (1.0 + sc_ref[...]) + sh_ref[...]


def _route_head(h2b, wq_h, keys_ref, idx_ref, gate_ref, rows):
    tn = h2b.shape[0]
    q = jnp.dot(h2b, wq_h, preferred_element_type=F32)
    v1, i1 = _top16(_dot_nt(keys_ref[0], q[:, 0:PEER_HALF]), N_KEYS)
    v2, i2 = _top16(_dot_nt(keys_ref[1], q[:, PEER_HALF:PEER_QDIM]), N_KEYS)
    cand = jnp.concatenate([v1[a:a + 1] + v2[0:n] for a, n in enumerate(CAND_COLS)]
                           + [jnp.full((CAND_PAD, tn), -jnp.inf, F32)], axis=0)
    top_s, pos = _top16(cand, N_CAND + CAND_PAD)
    a_sel = jnp.zeros_like(pos)
    b_sel = pos
    for a in range(1, PEER_TOPK):
        later = pos >= float(CAND_START[a])
        a_sel = a_sel + jnp.where(later, 1.0, 0.0)
        b_sel = b_sel - jnp.where(later, float(CAND_COLS[a - 1]), 0.0)
    expert = _take16(i1, a_sel) * N_KEYS + _take16(i2, b_sel)
    e = jnp.exp(top_s - top_s[0:1])
    gate_ref[rows, :] = e / jnp.sum(e, axis=0, keepdims=True)
    idx_ref[rows, :] = expert.astype(jnp.int32) * ROW_SUB


def _route_kernel(x_ref, sh_ref, sc_ref, n2_ref, wq_ref, keys_ref, h2_ref, idx_ref, gate_ref):
    h2 = _route_hidden(x_ref, sh_ref, sc_ref, n2_ref)
    h2_ref[...] = h2
    h2b = h2.astype(BF16)
    for h in range(PEER_HEADS):
        _route_head(h2b, wq_ref[h], keys_ref, idx_ref, gate_ref, slice(h * PEER_TOPK, (h + 1) * PEER_TOPK))


def _route_out_shape():
    return [jax.ShapeDtypeStruct((N_THIRD, D_MODEL), F32),
            jax.ShapeDtypeStruct((N_PICK, N_THIRD), jnp.int32),
            jax.ShapeDtypeStruct((N_PICK, N_THIRD), F32)]


def _route_first(x, mod_l, norm2, wq, keys):
    tok = pl.BlockSpec((TM, D_MODEL), lambda i: (i, 0))
    pick = pl.BlockSpec((N_PICK, TM), lambda i: (0, i))
    return pl.pallas_call(
        _route_kernel,
        grid=(N_THIRD // TM,),
        in_specs=[tok, _mod_spec(3), _mod_spec(4), _const_spec((1, D_MODEL)),
                  _const_spec((PEER_HEADS, D_MODEL, PEER_QDIM)), _const_spec((2, N_KEYS, PEER_HALF))],
        out_specs=[tok, pick, pick],
        out_shape=_route_out_shape(),
        compiler_params=pltpu.CompilerParams(vmem_limit_bytes=VMEM_LIMIT),
        name="peer_route",
    )(x, mod_l, mod_l, norm2.reshape(1, D_MODEL), wq, keys)


def _route_side_specs(third):
    off = third * (N_THIRD // TP)

    def mod(chunk):
        return pl.BlockSpec((None, 1, D_MODEL), lambda i: (_cond_row((off + i) * TP // TM), 0, chunk))

    in_specs = [pl.BlockSpec((TP, D_MODEL), lambda i: (off + i, 0)), mod(3), mod(4), _const_spec((1, D_MODEL)),
                pl.BlockSpec((PEER_HEADS, D_MODEL, PEER_QDIM), lambda i: (0, 0, 0), pipeline_mode=pl.Buffered(1)),
                _const_spec((2, N_KEYS, PEER_HALF))]
    pick = pl.BlockSpec((N_PICK, TP), lambda i: (0, i))
    out_specs = [pl.BlockSpec((TP, D_MODEL), lambda i: (i, 0)), pick, pick]
    return in_specs, out_specs


def _route_side(in_refs, out_refs, h2b_ref):
    x_ref, sh_ref, sc_ref, n2_ref, wq_ref, keys_ref = in_refs
    h2_ref, idx_ref, gate_ref = out_refs
    h2 = _route_hidden(x_ref, sh_ref, sc_ref, n2_ref)
    h2_ref[...] = h2
    h2b_ref[...] = h2.astype(BF16)

    def side(h):
        rows = pl.ds(pl.multiple_of(h * PEER_TOPK, PEER_TOPK), PEER_TOPK)
        _route_head(h2b_ref[...], wq_ref[h], keys_ref, idx_ref, gate_ref, rows)

    return side


def _pack_kernel(t_ref, o_ref):
    def rounded_bits(v):
        return lax.bitcast_convert_type(v.astype(BF16).astype(F32), jnp.int32)

    lo = lax.shift_right_logical(rounded_bits(t_ref[:, 0:ROW_WORDS]), 16)
    hi = rounded_bits(t_ref[:, ROW_WORDS:D_MODEL]) & jnp.int32(-65536)
    words = lo | hi
    for g in range(PACK_ROWS // 8):
        for s in range(ROW_SUB):
            o_ref[pl.ds(8 * g * ROW_SUB + s, 8, stride=ROW_SUB), :] = words[8 * g:8 * g + 8, s * 128:(s + 1) * 128]


def _pack_table(tabs, layer):
    return pl.pallas_call(
        _pack_kernel,
        grid=(N_EXPERTS // PACK_ROWS,),
        in_specs=[pl.BlockSpec((None, PACK_ROWS, D_MODEL), lambda i: (layer, i, 0))],
        out_specs=pl.BlockSpec((PACK_ROWS * ROW_SUB, 128), lambda i: (i, 0)),
        out_shape=jax.ShapeDtypeStruct((N_EXPERTS * ROW_SUB, 128), jnp.int32),
        name="pack_table",
    )(tabs)


def _unpack(words):
    lo = lax.bitcast_convert_type(words << 16, F32)
    hi = lax.bitcast_convert_type(words & jnp.int32(-65536), F32)
    return lo, hi


def _gather_planes(tab_ref, idx_ref, t, g_ref):
    for k in range(N_PICK):
        row = pl.multiple_of(idx_ref[t, k], ROW_SUB)
        g_ref[pl.ds(k, ROW_SUB, stride=PLANE_STRIDE), :] = tab_ref[pl.ds(row, ROW_SUB), :]


def _plane(g_ref, s):
    return g_ref[s * PLANE_STRIDE:s * PLANE_STRIDE + N_PICK, :]


def _split_bf16(v):
    hi = v.astype(BF16)
    return hi, (v - hi.astype(F32)).astype(BF16)


def _pipelined_tokens(tab_ref, idx_ref, bufs, compute, side):
    _gather_planes(tab_ref, idx_ref, 0, bufs[0])

    def group(j, carry):
        if side is not None:
            side(j)
        for p in range(TOKEN_UNROLL):
            t = TOKEN_UNROLL * j + p
            compute(t, pl.multiple_of(TOKEN_UNROLL * j + p // 8 * 8, 8), p % 8, bufs[p % 2])
            _gather_planes(tab_ref, idx_ref, jnp.minimum(t + 1, TP - 1), bufs[(p + 1) % 2])
        return carry

    lax.fori_loop(0, TP // TOKEN_UNROLL, group, 0)


def _split_refs(with_route, refs, n_in, n_scratch):
    n_rin, n_rout = (6, 3) if with_route else (0, 0)
    cuts = [n_in, n_rin, 1, n_rout, n_scratch]
    parts, at = [], 0
    for n in cuts:
        parts.append(refs[at:at + n])
        at += n
    parts.append(refs[at:])
    return parts


def _peer_act_kernel(with_route, *refs):
    (idx_ref, x_ref, gate_ref, tab_ref), rin, (o_ref,), rout, (ga_ref, gb_ref), rscratch = _split_refs(
        with_route, refs, 4, 2)
    side = _route_side(rin, rout, rscratch[0]) if with_route else None
    ones = jnp.ones((8, 2 * 128), BF16)

    def token(t, t8, r, g_ref):
        def x_row(j):
            return x_ref[pl.ds(t8, 8), j * 128:(j + 1) * 128][r:r + 1]

        acc = jnp.zeros((N_PICK, 128), F32)
        for s in range(ROW_SUB):
            lo, hi = _unpack(_plane(g_ref, s))
            acc = acc + lo * x_row(s) + hi * x_row(ROW_SUB + s)
        act = _dot_nt(ones, jnp.concatenate(_split_bf16(acc), axis=1))
        o_ref[pl.ds(t, 1), :] = jax.nn.gelu(act[0:1]) * gate_ref[pl.ds(t, 1), :]

    _pipelined_tokens(tab_ref, idx_ref, (ga_ref, gb_ref), token, side)


def _peer_out_kernel(with_route, *refs):
    (idx_ref, coef_ref, tab_ref), rin, (o_ref,), rout, (ga_ref, gb_ref, st_ref), rscratch = _split_refs(
        with_route, refs, 3, 3)
    side = _route_side(rin, rout, rscratch[0]) if with_route else None

    def token(t, t8, r, g_ref):
        c_hi, c_lo = _split_bf16(coef_ref[t])
        lhs = jnp.concatenate([c_hi, c_lo, jnp.zeros((4, 2 * N_PICK), BF16)], axis=0)
        for s in range(ROW_SUB):
            w = pltpu.bitcast(_plane(g_ref, s), BF16)
            acc = jnp.dot(lhs, w, preferred_element_type=F32)
            st_ref[8 * s + r:8 * s + r + 1, :] = acc[0:1] + acc[2:3]
            st_ref[8 * (ROW_SUB + s) + r:8 * (ROW_SUB + s) + r + 1, :] = acc[1:2] + acc[3:4]
        if r == 7:
            for j in range(D_MODEL // 128):
                o_ref[pl.ds(t8, 8), j * 128:(j + 1) * 128] = st_ref[8 * j:8 * j + 8, :]

    _pipelined_tokens(tab_ref, idx_ref, (ga_ref, gb_ref), token, side)


def _expert_call(body, name, in_specs, args, out_spec, out_width, scratch, side):
    out_specs, out_shape = [out_spec], [jax.ShapeDtypeStruct((N_THIRD, out_width), F32)]
    if side is not None:
        assert TP // TOKEN_UNROLL == PEER_HEADS
        third, x, mod_l, norm2, wq, keys = side
        rin, rout = _route_side_specs(third)
        in_specs = in_specs + rin
        args = args + (x, mod_l, mod_l, norm2.reshape(1, D_MODEL), wq, keys)
        out_specs = out_specs + rout
        out_shape = out_shape + _route_out_shape()
        scratch = scratch + [pltpu.VMEM((TP, D_MODEL), BF16)]
    outs = pl.pallas_call(
        functools.partial(body, side is not None),
        grid=(N_THIRD // TP,),
        in_specs=in_specs,
        out_specs=out_specs,
        out_shape=out_shape,
        scratch_shapes=scratch,
        compiler_params=pltpu.CompilerParams(vmem_limit_bytes=VMEM_LIMIT),
        name=name,
    )(*args)
    return outs[0], tuple(outs[1:])


_SMEM_IDX = pl.BlockSpec((TP, N_PICK), lambda i: (i, 0), memory_space=pltpu.SMEM)
_TABLE = pl.BlockSpec((N_EXPERTS * ROW_SUB, 128), lambda i: (0, 0), pipeline_mode=pl.Buffered(1))
_PLANES = pltpu.VMEM((ROW_SUB * PLANE_STRIDE, 128), jnp.int32)


def _peer_act(idx, h2, gate, u_words, side=None):
    rows = pl.BlockSpec((TP, D_MODEL), lambda i: (i, 0))
    pick = pl.BlockSpec((TP, N_PICK), lambda i: (i, 0))
    coef, routed = _expert_call(_peer_act_kernel, "peer_act", [_SMEM_IDX, rows, pick, _TABLE],
                                (idx, h2, gate, u_words), pick, N_PICK, [_PLANES, _PLANES], side)
    zero = jnp.zeros_like(coef)
    coef = jnp.stack([jnp.stack([coef, zero], axis=-1), jnp.stack([zero, coef], axis=-1)], axis=1)
    return coef.reshape(N_THIRD, 2, 2 * N_PICK), routed


def _peer_out(idx, coef, v_words, side=None):
    rows = pl.BlockSpec((TP, D_MODEL), lambda i: (i, 0))
    pair = pl.BlockSpec((TP, 2, 2 * N_PICK), lambda i: (i, 0, 0))
    stage = pltpu.VMEM((8 * (D_MODEL // 128), 128), F32)
    return _expert_call(_peer_out_kernel, "peer_out", [_SMEM_IDX, pair, _TABLE], (idx, coef, v_words),
                        rows, D_MODEL, [_PLANES, _PLANES, stage], side)


def _peer(x, mod_l, norm2, wq, keys, u_words, v_words):
    route_args = (x, mod_l, norm2, wq, keys)
    h2, idx, gate = _route_first(*route_args)
    outs = []
    for third in range(3):
        coef, _ = _peer_act(idx.T, h2, gate.T, u_words)
        side = (third + 1,) + route_args if third < 2 else None
        p, routed = _peer_out(idx.T, coef, v_words, side)
        outs.append(p)
        if side is not None:
            h2, idx, gate = routed
    return tuple(outs)


def _final_kernel(x_ref, pa_ref, pb_ref, pc_ref, g2_ref, n_ref, o_ref):
    o_ref[...] = _rms(x_ref[...] + g2_ref[...] * _pick_third((pa_ref, pb_ref, pc_ref)), n_ref[...])


def _final(x, p, mod_l, final_norm):
    tok = pl.BlockSpec((TM, D_MODEL), lambda i: (i, 0))
    return pl.pallas_call(
        _final_kernel,
        grid=(N_TILES,),
        in_specs=[tok] + _third_specs(D_MODEL) + [_mod_spec(5), _const_spec((1, D_MODEL))],
        out_specs=tok,
        out_shape=jax.ShapeDtypeStruct((N_TOK, D_MODEL), F32),
        name="final_norm",
    )(x, *p, mod_l, final_norm.reshape(1, D_MODEL))


def _rope_tables():
    rows = DEC_SEQ // GRID_W
    row_ids = jnp.repeat(jnp.arange(rows), GRID_W).astype(F32)
    col_ids = jnp.tile(jnp.arange(GRID_W), rows).astype(F32)
    inv_freq = ROPE_BASE ** (-jnp.arange(ROPE_FREQS, dtype=F32) / ROPE_FREQS)
    ang_r = row_ids[:, None] * inv_freq
    ang_c = col_ids[:, None] * inv_freq
    cos = jnp.concatenate([jnp.cos(ang_r), jnp.cos(ang_r), jnp.cos(ang_c), jnp.cos(ang_c)], axis=1)
    sin = jnp.concatenate([-jnp.sin(ang_r), jnp.sin(ang_r), -jnp.sin(ang_c), jnp.sin(ang_c)], axis=1)
    reps = ATT_WIDTH // ATT_HD
    cos = jnp.concatenate([jnp.ones((TM, ATT_WIDTH), F32), jnp.tile(cos, (1, reps))], axis=0)
    sin = jnp.concatenate([jnp.zeros((TM, ATT_WIDTH), F32), jnp.tile(sin, (1, reps))], axis=0)
    return cos, sin


def kernel(x_prompt, x_sample, cache_k, cache_v, state_lru, c, c_ctx, w_mod, b_mod, norm1, norm2, w_in, a_norm, a_ws, a_bs, lru_conv_w, lru_conv_b, lru_w_r, lru_b_r, lru_w_i, lru_b_i, lru_lam, att_lam, att_subln, w_up_a, w_up_b, w_up_c, w_out, peer_wq, peer_keys, peer_u, peer_v, final_norm):
    x = jnp.concatenate([x_prompt.reshape(N_CTX_TOK, D_MODEL), x_sample.reshape(N_LAT_TOK, D_MODEL)], axis=0)
    cond = jnp.concatenate([c_ctx[None, :], c, jnp.zeros((N_COND - 1 - DEC_BATCH, D_MODEL), F32)], axis=0)
    mod = _modulation(cond, w_mod, b_mod).reshape(DEPTH, N_COND, 1, N_MOD * D_MODEL)
    cos_t, sin_t = _rope_tables()
    zero_h0 = jnp.zeros((BATCH, 2, LRU_WIDTH), F32)
    ks, vs, hs = [], [], []
    res = None
    for i in range(DEPTH):
        lam_init = 0.8 - 0.6 * math.exp(-0.3 * i)
        outs = _inproj(x, res, mod[i], norm1[i], w_in[i].astype(BF16), cos_t, sin_t)
        au, av, rx, rg, q, k, kr, v, br = outs[:9]
        if res is not None:
            x = outs[9]
        w_r = _blockdiag128(lru_w_r[i])
        w_i = _blockdiag128(lru_w_i[i])
        lru_args = (lru_conv_w[i], lru_conv_b[i], w_r, lru_b_r[i], w_i, lru_b_i[i], lru_lam[i])
        hs_ctx, hl_ctx = _lru(rx, 0, BATCH, SEQ, *lru_args, zero_h0)
        hs_lat, _ = _lru(rx, N_CTX_TOK, DEC_BATCH, DEC_SEQ, *lru_args, state_lru[:, i])
        o_ctx = _attention(q, kr, v, 0, BATCH, SEQ, None, att_lam[i], att_subln[i], lam_init)
        ctx = (cache_k[:, i].reshape(DEC_BATCH, PAST_LEN, ATT_WIDTH), cache_v[:, i].reshape(DEC_BATCH, PAST_LEN, ATT_WIDTH))
        o_lat = _attention(q, kr, v, N_CTX_TOK, DEC_BATCH, DEC_SEQ, ctx, att_lam[i], att_subln[i], lam_init)
        bias = jnp.repeat(a_bs[i].T, CHUNK, axis=1)
        x = _merge(x, au, av, hs_ctx, hs_lat, rg, o_ctx, o_lat, br, mod[i], a_norm[i], a_ws[i], bias, w_up_a[i].astype(BF16), w_up_b[i].astype(BF16),
                   w_up_c[i].astype(BF16), w_out[i].astype(BF16))
        wq = peer_wq[i].astype(BF16).reshape(D_MODEL, PEER_HEADS, PEER_QDIM).transpose(1, 0, 2)
        p = _peer(x, mod[i], norm2[i], wq, peer_keys[i], _pack_table(peer_u, i), _pack_table(peer_v, i))
        res = (p, mod[i])
        ks.append(k[:N_CTX_TOK].reshape(BATCH, SEQ, ATT_HEADS, ATT_VD))
        vs.append(v[:N_CTX_TOK].reshape(BATCH, SEQ, ATT_HEADS, ATT_VD))
        hs.append(hl_ctx)
    y = _final(x, res[0], res[1], final_norm)
    return (y[:N_CTX_TOK].reshape(BATCH, SEQ, D_MODEL), y[N_CTX_TOK:].reshape(DEC_BATCH, DEC_SEQ, D_MODEL),
            jnp.stack(ks, axis=1), jnp.stack(vs, axis=1), jnp.stack(hs, axis=1))
```

```python
import functools
import math

import jax
import jax.numpy as jnp
from jax import lax
from jax.experimental import pallas as pl
from jax.experimental.pallas import tpu as pltpu

D_MODEL = 1024
BATCH = 16
SEQ = 256
DEPTH = 2
DEC_BATCH = 4
DEC_SEQ = 2048
PAST_LEN = 512
GRID_W = 64
EPS = 1e-6
N_MOD = 6
CHUNK = 128
A_GROUPS = 4
A_WIDTH = 512
LRU_BLOCKS = 8
LRU_BLOCK_W = 64
LRU_WIDTH = 512
CONV_W = 4
LRU_C = 8.0
ATT_HEADS = 4
ATT_HD = 64
ATT_VD = 128
ATT_WIDTH = 512
ROPE_BASE = 10000.0
ROPE_FREQS = 16
N_BRANCH = 3
IN_SPLITS = (512, 1024, 1536, 2048, 2560, 3072, 3584)
IN_WIDTH = 3584 + N_BRANCH * D_MODEL
PEER_HEADS = 8
N_KEYS = 128
N_EXPERTS = N_KEYS * N_KEYS
PEER_QDIM = 256
PEER_HALF = 128
PEER_TOPK = 16
N_PICK = PEER_HEADS * PEER_TOPK
CAND_COLS = tuple(PEER_TOPK // (a + 1) for a in range(PEER_TOPK))
CAND_START = tuple(sum(CAND_COLS[:a]) for a in range(PEER_TOPK))
N_CAND = sum(CAND_COLS)
CAND_PAD = -N_CAND % 8

N_CTX_TOK = BATCH * SEQ
N_LAT_TOK = DEC_BATCH * DEC_SEQ
N_TOK = N_CTX_TOK + N_LAT_TOK
TM = 256
N_TILES = N_TOK // TM
N_THIRD = N_TOK // 3
THIRD_TILES = N_THIRD // TM
CTX_TILES = N_CTX_TOK // TM
LAT_TILES_PER_SEQ = DEC_SEQ // TM
N_COND = 8
TP = 128
TOKEN_UNROLL = 16
PACK_ROWS = 256
ROW_WORDS = D_MODEL // 2
ROW_SUB = ROW_WORDS // 128
PLANE_STRIDE = N_PICK + 8
VMEM_LIMIT = 56 * 1024 * 1024

F32 = jnp.float32
BF16 = jnp.bfloat16
HI = lax.Precision.HIGHEST


def _cond_row(i):
    return jnp.maximum(i - LAT_TILES_PER_SEQ, 0) // LAT_TILES_PER_SEQ


def _pos_block(i):
    return jnp.where(i < CTX_TILES, 0, 1 + i % LAT_TILES_PER_SEQ)


def _mod_spec(chunk):
    return pl.BlockSpec((None, 1, D_MODEL), lambda i: (_cond_row(i), 0, chunk))


def _third_specs(width):
    return [pl.BlockSpec((TM, width), lambda i, k=k: (jnp.clip(i - k * THIRD_TILES, 0, THIRD_TILES - 1), 0))
            for k in range(3)]


def _pick_third(refs):
    i = pl.program_id(0)
    return jnp.where(i < THIRD_TILES, refs[0][...], jnp.where(i < 2 * THIRD_TILES, refs[1][...], refs[2][...]))


def _const_spec(shape):
    nd = len(shape)
    return pl.BlockSpec(shape, lambda *_: (0,) * nd)


def _rms(x, gain):
    return x * lax.rsqrt(jnp.mean(x * x, axis=-1, keepdims=True) + EPS) * gain


def _dot(a, b):
    return jnp.dot(a.astype(BF16), b.astype(BF16), preferred_element_type=F32)


def _dot_nt(a, b):
    return lax.dot_general(a.astype(BF16), b.astype(BF16), (((1,), (1,)), ((), ())), preferred_element_type=F32)


def _mod_kernel(cond_ref, w_ref, b_ref, o_ref):
    cond = cond_ref[...]
    act = cond * jax.nn.sigmoid(cond)
    o_ref[...] = jnp.dot(act, w_ref[...], preferred_element_type=F32, precision=HI) + b_ref[...]


def _modulation(cond, w_mod, b_mod):
    nc = 4
    cw = N_MOD * D_MODEL // nc
    return pl.pallas_call(
        _mod_kernel,
        grid=(DEPTH, nc),
        in_specs=[pl.BlockSpec((N_COND, D_MODEL), lambda l, j: (0, 0)),
                  pl.BlockSpec((None, D_MODEL, cw), lambda l, j: (l, 0, j)),
                  pl.BlockSpec((None, 1, cw), lambda l, j: (l, 0, j))],
        out_specs=pl.BlockSpec((None, N_COND, cw), lambda l, j: (l, 0, j)),
        out_shape=jax.ShapeDtypeStruct((DEPTH, N_COND, N_MOD * D_MODEL), F32),
        compiler_params=pltpu.CompilerParams(vmem_limit_bytes=VMEM_LIMIT),
        name="modulation",
    )(cond, w_mod, b_mod.reshape(DEPTH, 1, N_MOD * D_MODEL))


def _inproj_kernel(has_res, *refs):
    if has_res:
        x_ref, g2_ref = refs[0], refs[4]
        x = x_ref[...] + g2_ref[...] * _pick_third(refs[1:4])
        refs = refs[5:]
    else:
        x_ref = refs[0]
        refs = refs[1:]
        x = x_ref[...]
    (sh_ref, sc_ref, n1_ref, w_ref, cos_ref, sin_ref,
     au_ref, av_ref, rx_ref, rg_ref, q_ref, k_ref, kr_ref, v_ref, br_ref) = refs[:15]
    if has_res:
        refs[15][...] = x
    h = (_rms(x, n1_ref[...]) * (1.0 + sc_ref[...]) + sh_ref[...]).astype(BF16)

    def proj(lo, hi):
        return jnp.dot(h, w_ref[:, lo:hi], preferred_element_type=F32)

    au_ref[...] = proj(0, IN_SPLITS[0])
    av_ref[...] = proj(IN_SPLITS[0], IN_SPLITS[1])
    rx_ref[...] = proj(IN_SPLITS[1], IN_SPLITS[2])
    rg_ref[...] = proj(IN_SPLITS[2], IN_SPLITS[3])
    q = proj(IN_SPLITS[3], IN_SPLITS[4])
    k = proj(IN_SPLITS[4], IN_SPLITS[5])
    v_ref[...] = proj(IN_SPLITS[5], IN_SPLITS[6])
    for j in range(N_BRANCH):
        lo = IN_SPLITS[6] + j * D_MODEL
        br_ref[:, j * D_MODEL:(j + 1) * D_MODEL] = proj(lo, lo + D_MODEL)
    k_ref[...] = k
    lane = lax.broadcasted_iota(jnp.int32, (TM, ATT_WIDTH), 1)
    first = (lane % (2 * ROPE_FREQS)) < ROPE_FREQS
    cos = cos_ref[...]
    sin = sin_ref[...]

    def rot(t):
        partner = jnp.where(first, pltpu.roll(t, ATT_WIDTH - ROPE_FREQS, 1), pltpu.roll(t, ROPE_FREQS, 1))
        return t * cos + partner * sin

    q_ref[...] = rot(q) * (ATT_HD ** -0.5)
    kr_ref[...] = rot(k)


def _inproj(x, res, mod_l, norm1, w_in, cos_t, sin_t):
    has_res = res is not None
    tok = lambda w: pl.BlockSpec((TM, w), lambda i: (i, 0))
    in_specs = [tok(D_MODEL)]
    args = [x]
    if has_res:
        p, mod_prev = res
        in_specs += _third_specs(D_MODEL) + [_mod_spec(5)]
        args += [*p, mod_prev]
    in_specs += [_mod_spec(0), _mod_spec(1), _const_spec((1, D_MODEL)),
                 pl.BlockSpec((D_MODEL, IN_WIDTH), lambda i: (0, 0), pipeline_mode=pl.Buffered(1)),
                 pl.BlockSpec((TM, ATT_WIDTH), lambda i: (_pos_block(i), 0)),
                 pl.BlockSpec((TM, ATT_WIDTH), lambda i: (_pos_block(i), 0))]
    args += [mod_l, mod_l, norm1.reshape(1, D_MODEL), w_in, cos_t, sin_t]
    widths = [512] * 8 + [N_BRANCH * D_MODEL]
    out_specs = [tok(w) for w in widths]
    out_shape = [jax.ShapeDtypeStruct((N_TOK, w), F32) for w in widths]
    if has_res:
        out_specs.append(tok(D_MODEL))
        out_shape.append(jax.ShapeDtypeStruct((N_TOK, D_MODEL), F32))
    return pl.pallas_call(
        functools.partial(_inproj_kernel, has_res),
        grid=(N_TILES,),
        in_specs=in_specs,
        out_specs=out_specs,
        out_shape=out_shape,
        compiler_params=pltpu.CompilerParams(vmem_limit_bytes=VMEM_LIMIT),
        name="inproj",
    )(*args)


def _lru_kernel(seq, x_ref, cw_ref, cb_ref, wr_ref, br_ref, wi_ref, bi_ref, lam_ref, h0_ref, hs_ref, hl_ref):
    x = x_ref[...]
    t = lax.broadcasted_iota(jnp.int32, (seq, 128), 0)

    def shifted(v, k, fill):
        r = pltpu.roll(v, k % seq, 0)
        ok = (t >= k) if k > 0 else (t < seq + k)
        return jnp.where(ok, r, fill)

    cw = cw_ref[...]
    xc = (shifted(x, 2, 0.0) * cw[0:1] + shifted(x, 1, 0.0) * cw[1:2] + x * cw[2:3]
          + shifted(x, -1, 0.0) * cw[3:4] + cb_ref[...])

    xc_hi, xc_lo = _split_bf16(xc)

    def gate_dot(w):
        w_hi, w_lo = _split_bf16(w)
        return (jnp.dot(xc_hi, w_hi, preferred_element_type=F32) + jnp.dot(xc_lo, w_hi, preferred_element_type=F32)
                + jnp.dot(xc_hi, w_lo, preferred_element_type=F32))

    def direction(d):
        r = jax.nn.sigmoid(gate_dot(wr_ref[d, 0]) + br_ref[d:d + 1])
        g = jax.nn.sigmoid(gate_dot(wi_ref[d, 0]) + bi_ref[d:d + 1])
        z = -lam_ref[d:d + 1]
        softplus = jnp.maximum(z, 0.0) + jnp.log(1.0 + jnp.exp(-jnp.abs(z)))
        log_a = -LRU_C * r * softplus
        a = jnp.exp(log_a)
        b = jnp.sqrt(1.0 - jnp.exp(2.0 * log_a)) * (g * xc)
        sgn = 1 if d == 0 else -1
        local = t % 8
        for k in (1, 2, 4):
            inside = (local >= k) if d == 0 else (local < 8 - k)
            a_prev = jnp.where(inside, pltpu.roll(a, (sgn * k) % seq, 0), 1.0)
            b_prev = jnp.where(inside, pltpu.roll(b, (sgn * k) % seq, 0), 0.0)
            b = a * b_prev + b
            a = a * a_prev
        edge = 7 if d == 0 else 0
        h = h0_ref[0, d:d + 1]
        tiles = range(seq // 8) if d == 0 else reversed(range(seq // 8))
        for j in tiles:
            rows = slice(8 * j, 8 * j + 8)
            h_tile = a[rows] * h + b[rows]
            if d == 0:
                hs_ref[rows, :] = h_tile
            else:
                hs_ref[rows, :] += h_tile
            h = h_tile[edge:edge + 1]
        return h

    h_f = direction(0)
    h_b = direction(1)
    hl_ref[0] = jnp.concatenate([h_f, h_b], axis=0)


def _lru(rx, row0, nseq, seq, conv_w, conv_b, w_r, b_r, w_i, b_i, lam, h0):
    nb = LRU_WIDTH // 128
    blk0 = row0 // seq
    par = lambda shape: pl.BlockSpec(shape, lambda b, c: (0,) * (len(shape) - 1) + (c,))
    return pl.pallas_call(
        functools.partial(_lru_kernel, seq),
        grid=(nseq, nb),
        in_specs=[pl.BlockSpec((seq, 128), lambda b, c: (blk0 + b, c)),
                  par((CONV_W, 128)), par((1, 128)),
                  pl.BlockSpec((2, 1, 128, 128), lambda b, c: (0, c, 0, 0)), par((2, 128)),
                  pl.BlockSpec((2, 1, 128, 128), lambda b, c: (0, c, 0, 0)), par((2, 128)),
                  par((2, 128)),
                  pl.BlockSpec((1, 2, 128), lambda b, c: (b, 0, c))],
        out_specs=[pl.BlockSpec((seq, 128), lambda b, c: (b, c)),
                   pl.BlockSpec((1, 2, 128), lambda b, c: (b, 0, c))],
        out_shape=[jax.ShapeDtypeStruct((nseq * seq, LRU_WIDTH), F32),
                   jax.ShapeDtypeStruct((nseq, 2, LRU_WIDTH), F32)],
        compiler_params=pltpu.CompilerParams(vmem_limit_bytes=VMEM_LIMIT),
        name="rglru",
    )(rx, conv_w, conv_b.reshape(1, LRU_WIDTH), w_r, b_r, w_i, b_i, lam, h0)


def _blockdiag128(w):
    w = w.reshape(2, LRU_BLOCKS // 2, 2, LRU_BLOCK_W, LRU_BLOCK_W)
    z = jnp.zeros_like(w[:, :, 0])
    top = jnp.concatenate([w[:, :, 0], z], axis=-1)
    bot = jnp.concatenate([z, w[:, :, 1]], axis=-1)
    return jnp.concatenate([top, bot], axis=-2)


def _attn_kernel(has_ctx, lam_init, *refs):
    if has_ctx:
        q_ref, k_ref, v_ref, kc_ref, vc_ref, lp_ref, g_ref, o_ref = refs
    else:
        q_ref, k_ref, v_ref, lp_ref, g_ref, o_ref = refs
    lp = lp_ref[...]
    lam = (jnp.exp(jnp.sum(lp[0:1] * lp[1:2], axis=-1, keepdims=True))
           - jnp.exp(jnp.sum(lp[2:3] * lp[3:4], axis=-1, keepdims=True)) + lam_init)
    q = q_ref[...]
    lane = lax.broadcasted_iota(jnp.int32, q.shape, 1)
    halves = (jnp.where(lane < ATT_HD, q, 0.0), jnp.where(lane >= ATT_HD, q, 0.0))
    k = k_ref[...]
    v = v_ref[...]
    w = []
    for qh in halves:
        s = _dot_nt(qh, k)
        m = jnp.max(s, axis=-1, keepdims=True)
        if has_ctx:
            sc = _dot_nt(qh, kc_ref[0])
            m = jnp.maximum(m, jnp.max(sc, axis=-1, keepdims=True))
            ec = jnp.exp(sc - m)
        e = jnp.exp(s - m)
        den = jnp.sum(e, axis=-1, keepdims=True)
        if has_ctx:
            den = den + jnp.sum(ec, axis=-1, keepdims=True)
            w.append((e / den, ec / den))
        else:
            w.append((e / den,))
    o = _dot(w[0][0] - lam * w[1][0], v)
    if has_ctx:
        o = o + _dot(w[0][1] - lam * w[1][1], vc_ref[0])
    o_ref[...] = _rms(o, g_ref[...]) * (1.0 - lam_init)


def _attention(q, kr, v, row0, nseq, seq, ctx, att_lam, subln, lam_init):
    has_ctx = ctx is not None
    nq = seq // TM
    blk0 = row0 // seq
    in_specs = [pl.BlockSpec((TM, ATT_VD), lambda b, h, i: ((row0 // TM) + b * nq + i, h)),
                pl.BlockSpec((seq, ATT_VD), lambda b, h, i: (blk0 + b, h)),
                pl.BlockSpec((seq, ATT_VD), lambda b, h, i: (blk0 + b, h))]
    args = [q, kr, v]
    if has_ctx:
        in_specs += [pl.BlockSpec((1, PAST_LEN, ATT_VD), lambda b, h, i: (b, 0, h))] * 2
        args += list(ctx)
    in_specs += [_const_spec((4, ATT_HD)), _const_spec((1, ATT_VD))]
    args += [att_lam, subln.reshape(1, ATT_VD)]
    return pl.pallas_call(
        functools.partial(_attn_kernel, has_ctx, lam_init),
        grid=(nseq, ATT_HEADS, nq),
        in_specs=in_specs,
        out_specs=pl.BlockSpec((TM, ATT_VD), lambda b, h, i: (b * nq + i, h)),
        out_shape=jax.ShapeDtypeStruct((nseq * seq, ATT_WIDTH), F32),
        compiler_params=pltpu.CompilerParams(vmem_limit_bytes=VMEM_LIMIT),
        name="diffattn",
    )(*args)


def _merge_kernel(x_ref, au_ref, av_ref, hsc_ref, hsl_ref, rg_ref, oc_ref, ol_ref, br_ref, g1_ref, an_ref, ws_ref,
                  bs_ref, wa_ref, wb_ref, wc_ref, wo_ref, xo_ref):
    is_ctx = pl.program_id(0) < CTX_TILES
    hs = jnp.where(is_ctx, hsc_ref[...], hsl_ref[...])
    o = jnp.where(is_ctx, oc_ref[...], ol_ref[...])
    vn = _rms(av_ref[...], an_ref[...])
    rows = []
    for c in range(TM // CHUNK):
        cols = []
        for g in range(A_GROUPS):
            blk = vn[c * CHUNK:(c + 1) * CHUNK, g * 128:(g + 1) * 128]
            cols.append(_dot(ws_ref[g], blk))
        rows.append(jnp.concatenate(cols, axis=1) + bs_ref[...])
    y_a = au_ref[...] * jnp.concatenate(rows, axis=0)
    y_b = hs * jax.nn.gelu(rg_ref[...])
    merged = (jax.nn.sigmoid(br_ref[:, 0:D_MODEL]) * _dot(y_a, wa_ref[...])
              + jax.nn.sigmoid(br_ref[:, D_MODEL:2 * D_MODEL]) * _dot(y_b, wb_ref[...])
              + jax.nn.sigmoid(br_ref[:, 2 * D_MODEL:3 * D_MODEL]) * _dot(o, wc_ref[...]))
    xo_ref[...] = x_ref[...] + g1_ref[...] * _dot(merged, wo_ref[...])


def _merge(x, au, av, hs_ctx, hs_lat, rg, o_ctx, o_lat, br, mod_l, a_norm, a_ws, bias, wa, wb, wc, wo):
    tok = lambda w: pl.BlockSpec((TM, w), lambda i: (i, 0))
    ctx = pl.BlockSpec((TM, 512), lambda i: (jnp.minimum(i, CTX_TILES - 1), 0))
    lat = pl.BlockSpec((TM, 512), lambda i: (jnp.maximum(i - CTX_TILES, 0), 0))
    return pl.pallas_call(
        _merge_kernel,
        grid=(N_TILES,),
        in_specs=[tok(D_MODEL), tok(512), tok(512), ctx, lat, tok(512), ctx, lat, tok(N_BRANCH * D_MODEL),
                  _mod_spec(2), _const_spec((1, A_WIDTH)), _const_spec((A_GROUPS, CHUNK, CHUNK)),
                  _const_spec((CHUNK, A_WIDTH)),
                  _const_spec((A_WIDTH, D_MODEL)), _const_spec((LRU_WIDTH, D_MODEL)),
                  _const_spec((ATT_WIDTH, D_MODEL)), _const_spec((D_MODEL, D_MODEL))],
        out_specs=tok(D_MODEL),
        out_shape=jax.ShapeDtypeStruct((N_TOK, D_MODEL), F32),
        compiler_params=pltpu.CompilerParams(vmem_limit_bytes=VMEM_LIMIT),
        name="merge",
    )(x, au, av, hs_ctx, hs_lat, rg, o_ctx, o_lat, br, mod_l, a_norm.reshape(1, A_WIDTH), a_ws, bias, wa, wb, wc, wo)


def _top16(s, n):
    pos = lax.broadcasted_iota(jnp.int32, s.shape, 0).astype(F32)
    vals, idxs = [], []
    for _ in range(PEER_TOPK):
        m = jnp.max(s, axis=0, keepdims=True)
        am = jnp.min(jnp.where(s == m, pos, float(n)), axis=0, keepdims=True)
        vals.append(m)
        idxs.append(am)
        s = jnp.where(pos == am, -jnp.inf, s)
    return jnp.concatenate(vals, axis=0), jnp.concatenate(idxs, axis=0)


def _take16(table, sel):
    out = jnp.zeros_like(table)
    for a in range(PEER_TOPK):
        out = jnp.where(sel == float(a), table[a:a + 1], out)
    return out


def _route_hidden(x_ref, sh_ref, sc_ref, n2_ref):
    return _rms(x_ref[...], n2_ref[...]) * (1.0 + sc_ref[...]) + sh_ref[...]


def _route_head(h2b, wq_h, keys_ref, idx_ref, gate_ref, rows):
    tn = h2b.shape[0]
    q = jnp.dot(h2b, wq_h, preferred_element_type=F32)
    v1, i1 = _top16(_dot_nt(keys_ref[0], q[:, 0:PEER_HALF]), N_KEYS)
    v2, i2 = _top16(_dot_nt(keys_ref[1], q[:, PEER_HALF:PEER_QDIM]), N_KEYS)
    cand = jnp.concatenate([v1[a:a + 1] + v2[0:n] for a, n in enumerate(CAND_COLS)]
                           + [jnp.full((CAND_PAD, tn), -jnp.inf, F32)], axis=0)
    top_s, pos = _top16(cand, N_CAND + CAND_PAD)
    a_sel = jnp.zeros_like(pos)
    b_sel = pos
    for a in range(1, PEER_TOPK):
        later = pos >= float(CAND_START[a])
        a_sel = a_sel + jnp.where(later, 1.0, 0.0)
        b_sel = b_sel - jnp.where(later, float(CAND_COLS[a - 1]), 0.0)
    expert = _take16(i1, a_sel) * N_KEYS + _take16(i2, b_sel)
    e = jnp.exp(top_s - top_s[0:1])
    gate_ref[rows, :] = e / jnp.sum(e, axis=0, keepdims=True)
    idx_ref[rows, :] = expert.astype(jnp.int32) * ROW_SUB


def _route_kernel(x_ref, sh_ref, sc_ref, n2_ref, wq_ref, keys_ref, h2_ref, idx_ref, gate_ref):
    h2 = _route_hidden(x_ref, sh_ref, sc_ref, n2_ref)
    h2_ref[...] = h2
    h2b = h2.astype(BF16)
    for h in range(PEER_HEADS):
        _route_head(h2b, wq_ref[h], keys_ref, idx_ref, gate_ref, slice(h * PEER_TOPK, (h + 1) * PEER_TOPK))


def _route_out_shape():
    return [jax.ShapeDtypeStruct((N_THIRD, D_MODEL), F32),
            jax.ShapeDtypeStruct((N_PICK, N_THIRD), jnp.int32),
            jax.ShapeDtypeStruct((N_PICK, N_THIRD), F32)]


def _route_first(x, mod_l, norm2, wq, keys):
    tok = pl.BlockSpec((TM, D_MODEL), lambda i: (i, 0))
    pick = pl.BlockSpec((N_PICK, TM), lambda i: (0, i))
    return pl.pallas_call(
        _route_kernel,
        grid=(N_THIRD // TM,),
        in_specs=[tok, _mod_spec(3), _mod_spec(4), _const_spec((1, D_MODEL)),
                  _const_spec((PEER_HEADS, D_MODEL, PEER_QDIM)), _const_spec((2, N_KEYS, PEER_HALF))],
        out_specs=[tok, pick, pick],
        out_shape=_route_out_shape(),
        compiler_params=pltpu.CompilerParams(vmem_limit_bytes=VMEM_LIMIT),
        name="peer_route",
    )(x, mod_l, mod_l, norm2.reshape(1, D_MODEL), wq, keys)


def _route_side_specs(third):
    off = third * (N_THIRD // TP)

    def mod(chunk):
        return pl.BlockSpec((None, 1, D_MODEL), lambda i: (_cond_row((off + i) * TP // TM), 0, chunk))

    in_specs = [pl.BlockSpec((TP, D_MODEL), lambda i: (off + i, 0)), mod(3), mod(4), _const_spec((1, D_MODEL)),
                pl.BlockSpec((PEER_HEADS, D_MODEL, PEER_QDIM), lambda i: (0, 0, 0), pipeline_mode=pl.Buffered(1)),
                _const_spec((2, N_KEYS, PEER_HALF))]
    pick = pl.BlockSpec((N_PICK, TP), lambda i: (0, i))
    out_specs = [pl.BlockSpec((TP, D_MODEL), lambda i: (i, 0)), pick, pick]
    return in_specs, out_specs


def _route_side(in_refs, out_refs, h2b_ref):
    x_ref, sh_ref, sc_ref, n2_ref, wq_ref, keys_ref = in_refs
    h2_ref, idx_ref, gate_ref = out_refs
    h2 = _route_hidden(x_ref, sh_ref, sc_ref, n2_ref)
    h2_ref[...] = h2
    h2b_ref[...] = h2.astype(BF16)

    def side(h):
        rows = pl.ds(pl.multiple_of(h * PEER_TOPK, PEER_TOPK), PEER_TOPK)
        _route_head(h2b_ref[...], wq_ref[h], keys_ref, idx_ref, gate_ref, rows)

    return side


def _pack_kernel(t_ref, o_ref):
    def rounded_bits(v):
        return lax.bitcast_convert_type(v.astype(BF16).astype(F32), jnp.int32)

    lo = lax.shift_right_logical(rounded_bits(t_ref[:, 0:ROW_WORDS]), 16)
    hi = rounded_bits(t_ref[:, ROW_WORDS:D_MODEL]) & jnp.int32(-65536)
    words = lo | hi
    for g in range(PACK_ROWS // 8):
        for s in range(ROW_SUB):
            o_ref[pl.ds(8 * g * ROW_SUB + s, 8, stride=ROW_SUB), :] = words[8 * g:8 * g + 8, s * 128:(s + 1) * 128]


def _pack_table(tabs, layer):
    return pl.pallas_call(
        _pack_kernel,
        grid=(N_EXPERTS // PACK_ROWS,),
        in_specs=[pl.BlockSpec((None, PACK_ROWS, D_MODEL), lambda i: (layer, i, 0))],
        out_specs=pl.BlockSpec((PACK_ROWS * ROW_SUB, 128), lambda i: (i, 0)),
        out_shape=jax.ShapeDtypeStruct((N_EXPERTS * ROW_SUB, 128), jnp.int32),
        name="pack_table",
    )(tabs)


def _unpack(words):
    lo = lax.bitcast_convert_type(words << 16, F32)
    hi = lax.bitcast_convert_type(words & jnp.int32(-65536), F32)
    return lo, hi


def _gather_planes(tab_ref, idx_ref, t, g_ref):
    for k in range(N_PICK):
        row = pl.multiple_of(idx_ref[t, k], ROW_SUB)
        g_ref[pl.ds(k, ROW_SUB, stride=PLANE_STRIDE), :] = tab_ref[pl.ds(row, ROW_SUB), :]


def _gather_rows(tab_ref, idx_ref, t, g_ref):
    for k in range(N_PICK):
        row = pl.multiple_of(idx_ref[t, k], ROW_SUB)
        g_ref[k * ROW_SUB:(k + 1) * ROW_SUB, :] = tab_ref[pl.ds(row, ROW_SUB), :]


def _plane(g_ref, s):
    return g_ref[s * PLANE_STRIDE:s * PLANE_STRIDE + N_PICK, :]


def _split_bf16(v):
    hi = v.astype(BF16)
    return hi, (v - hi.astype(F32)).astype(BF16)


def _pipelined_tokens(gather, tab_ref, idx_ref, bufs, compute, side):
    gather(tab_ref, idx_ref, 0, bufs[0])

    def group(j, carry):
        if side is not None:
            side(j)
        for p in range(TOKEN_UNROLL):
            t = TOKEN_UNROLL * j + p
            compute(t, pl.multiple_of(TOKEN_UNROLL * j + p // 8 * 8, 8), p % 8, bufs[p % 2])
            gather(tab_ref, idx_ref, jnp.minimum(t + 1, TP - 1), bufs[(p + 1) % 2])
        return carry

    lax.fori_loop(0, TP // TOKEN_UNROLL, group, 0)


def _split_refs(with_route, refs, n_in, n_scratch):
    n_rin, n_rout = (6, 3) if with_route else (0, 0)
    cuts = [n_in, n_rin, 1, n_rout, n_scratch]
    parts, at = [], 0
    for n in cuts:
        parts.append(refs[at:at + n])
        at += n
    parts.append(refs[at:])
    return parts


def _peer_act_kernel(with_route, *refs):
    (idx_ref, x_ref, gate_ref, tab_ref), rin, (o_ref,), rout, (ga_ref, gb_ref), rscratch = _split_refs(
        with_route, refs, 4, 2)
    side = _route_side(rin, rout, rscratch[0]) if with_route else None
    ones = jnp.ones((8, 2 * 128), BF16)

    def token(t, t8, r, g_ref):
        def x_row(j):
            return x_ref[pl.ds(t8, 8), j * 128:(j + 1) * 128][r:r + 1]

        acc = jnp.zeros((N_PICK, 128), F32)
        for s in range(ROW_SUB):
            lo, hi = _unpack(_plane(g_ref, s))
            acc = acc + lo * x_row(s) + hi * x_row(ROW_SUB + s)
        act = _dot_nt(ones, jnp.concatenate(_split_bf16(acc), axis=1))
        o_ref[pl.ds(t, 1), :] = jax.nn.gelu(act[0:1]) * gate_ref[pl.ds(t, 1), :]

    _pipelined_tokens(_gather_planes, tab_ref, idx_ref, (ga_ref, gb_ref), token, side)


def _peer_out_kernel(with_route, *refs):
    (idx_ref, coef_ref, tab_ref), rin, (o_ref,), rout, (ga_ref, gb_ref, st_ref), rscratch = _split_refs(
        with_route, refs, 3, 3)
    side = _route_side(rin, rout, rscratch[0]) if with_route else None

    keep = (lax.broadcasted_iota(jnp.int32, (8, 8 * N_PICK), 1) % 8
            == lax.broadcasted_iota(jnp.int32, (8, 8 * N_PICK), 0))

    def token(t, t8, r, g_ref):
        coef8 = coef_ref[pl.ds(t8, 8), :][r:r + 1]
        lhs = jnp.concatenate(_split_bf16(jnp.where(keep, coef8, 0.0)), axis=0)
        w = pltpu.bitcast(g_ref[...], BF16)
        acc = jnp.dot(lhs, w, preferred_element_type=F32)
        acc = acc[0:8] + acc[8:16]
        for q in range(8):
            j = (q % 2) * ROW_SUB + q // 2
            st_ref[8 * j + r:8 * j + r + 1, :] = acc[q:q + 1]
        if r == 7:
            for j in range(D_MODEL // 128):
                o_ref[pl.ds(t8, 8), j * 128:(j + 1) * 128] = st_ref[8 * j:8 * j + 8, :]

    _pipelined_tokens(_gather_rows, tab_ref, idx_ref, (ga_ref, gb_ref), token, side)


def _expert_call(body, name, in_specs, args, out_spec, out_width, scratch, side):
    out_specs, out_shape = [out_spec], [jax.ShapeDtypeStruct((N_THIRD, out_width), F32)]
    if side is not None:
        assert TP // TOKEN_UNROLL == PEER_HEADS
        third, x, mod_l, norm2, wq, keys = side
        rin, rout = _route_side_specs(third)
        in_specs = in_specs + rin
        args = args + (x, mod_l, mod_l, norm2.reshape(1, D_MODEL), wq, keys)
        out_specs = out_specs + rout
        out_shape = out_shape + _route_out_shape()
        scratch = scratch + [pltpu.VMEM((TP, D_MODEL), BF16)]
    outs = pl.pallas_call(
        functools.partial(body, side is not None),
        grid=(N_THIRD // TP,),
        in_specs=in_specs,
        out_specs=out_specs,
        out_shape=out_shape,
        scratch_shapes=scratch,
        compiler_params=pltpu.CompilerParams(vmem_limit_bytes=VMEM_LIMIT),
        name=name,
    )(*args)
    return outs[0], tuple(outs[1:])


_SMEM_IDX = pl.BlockSpec((TP, N_PICK), lambda i: (i, 0), memory_space=pltpu.SMEM)
_TABLE = pl.BlockSpec((N_EXPERTS * ROW_SUB, 128), lambda i: (0, 0), pipeline_mode=pl.Buffered(1))
_PLANES = pltpu.VMEM((ROW_SUB * PLANE_STRIDE, 128), jnp.int32)


def _peer_act(idx, h2, gate, u_words, side=None):
    rows = pl.BlockSpec((TP, D_MODEL), lambda i: (i, 0))
    pick = pl.BlockSpec((TP, N_PICK), lambda i: (i, 0))
    coef, routed = _expert_call(_peer_act_kernel, "peer_act", [_SMEM_IDX, rows, pick, _TABLE],
                                (idx, h2, gate, u_words), pick, N_PICK, [_PLANES, _PLANES], side)
    return jnp.repeat(coef, 8, axis=1), routed


def _peer_out(idx, coef, v_words, side=None):
    rows = pl.BlockSpec((TP, D_MODEL), lambda i: (i, 0))
    stage = pltpu.VMEM((8 * (D_MODEL // 128), 128), F32)
    picked = pltpu.VMEM((N_PICK * ROW_SUB, 128), jnp.int32)
    return _expert_call(_peer_out_kernel, "peer_out", [_SMEM_IDX, rows, _TABLE], (idx, coef, v_words),
                        rows, D_MODEL, [picked, picked, stage], side)


def _peer(x, mod_l, norm2, wq, keys, u_words, v_words):
    route_args = (x, mod_l, norm2, wq, keys)
    h2, idx, gate = _route_first(*route_args)
    outs = []
    for third in range(3):
        coef, _ = _peer_act(idx.T, h2, gate.T, u_words)
        side = (third + 1,) + route_args if third < 2 else None
        p, routed = _peer_out(idx.T, coef, v_words, side)
        outs.append(p)
        if side is not None:
            h2, idx, gate = routed
    return tuple(outs)


def _final_kernel(x_ref, pa_ref, pb_ref, pc_ref, g2_ref, n_ref, o_ref):
    o_ref[...] = _rms(x_ref[...] + g2_ref[...] * _pick_third((pa_ref, pb_ref, pc_ref)), n_ref[...])


def _final(x, p, mod_l, final_norm):
    tok = pl.BlockSpec((TM, D_MODEL), lambda i: (i, 0))
    return pl.pallas_call(
        _final_kernel,
        grid=(N_TILES,),
        in_specs=[tok] + _third_specs(D_MODEL) + [_mod_spec(5), _const_spec((1, D_MODEL))],
        out_specs=tok,
        out_shape=jax.ShapeDtypeStruct((N_TOK, D_MODEL), F32),
        name="final_norm",
    )(x, *p, mod_l, final_norm.reshape(1, D_MODEL))


def _rope_tables():
    rows = DEC_SEQ // GRID_W
    row_ids = jnp.repeat(jnp.arange(rows), GRID_W).astype(F32)
    col_ids = jnp.tile(jnp.arange(GRID_W), rows).astype(F32)
    inv_freq = ROPE_BASE ** (-jnp.arange(ROPE_FREQS, dtype=F32) / ROPE_FREQS)
    ang_r = row_ids[:, None] * inv_freq
    ang_c = col_ids[:, None] * inv_freq
    cos = jnp.concatenate([jnp.cos(ang_r), jnp.cos(ang_r), jnp.cos(ang_c), jnp.cos(ang_c)], axis=1)
    sin = jnp.concatenate([-jnp.sin(ang_r), jnp.sin(ang_r), -jnp.sin(ang_c), jnp.sin(ang_c)], axis=1)
    reps = ATT_WIDTH // ATT_HD
    cos = jnp.concatenate([jnp.ones((TM, ATT_WIDTH), F32), jnp.tile(cos, (1, reps))], axis=0)
    sin = jnp.concatenate([jnp.zeros((TM, ATT_WIDTH), F32), jnp.tile(sin, (1, reps))], axis=0)
    return cos, sin


def kernel(x_prompt, x_sample, cache_k, cache_v, state_lru, c, c_ctx, w_mod, b_mod, norm1, norm2, w_in, a_norm, a_ws, a_bs, lru_conv_w, lru_conv_b, lru_w_r, lru_b_r, lru_w_i, lru_b_i, lru_lam, att_lam, att_subln, w_up_a, w_up_b, w_up_c, w_out, peer_wq, peer_keys, peer_u, peer_v, final_norm):
    x = jnp.concatenate([x_prompt.reshape(N_CTX_TOK, D_MODEL), x_sample.reshape(N_LAT_TOK, D_MODEL)], axis=0)
    cond = jnp.concatenate([c_ctx[None, :], c, jnp.zeros((N_COND - 1 - DEC_BATCH, D_MODEL), F32)], axis=0)
    mod = _modulation(cond, w_mod, b_mod).reshape(DEPTH, N_COND, 1, N_MOD * D_MODEL)
    cos_t, sin_t = _rope_tables()
    zero_h0 = jnp.zeros((BATCH, 2, LRU_WIDTH), F32)
    ks, vs, hs = [], [], []
    res = None
    for i in range(DEPTH):
        lam_init = 0.8 - 0.6 * math.exp(-0.3 * i)
        outs = _inproj(x, res, mod[i], norm1[i], w_in[i].astype(BF16), cos_t, sin_t)
        au, av, rx, rg, q, k, kr, v, br = outs[:9]
        if res is not None:
            x = outs[9]
        w_r = _blockdiag128(lru_w_r[i])
        w_i = _blockdiag128(lru_w_i[i])
        lru_args = (lru_conv_w[i], lru_conv_b[i], w_r, lru_b_r[i], w_i, lru_b_i[i], lru_lam[i])
        hs_ctx, hl_ctx = _lru(rx, 0, BATCH, SEQ, *lru_args, zero_h0)
        hs_lat, _ = _lru(rx, N_CTX_TOK, DEC_BATCH, DEC_SEQ, *lru_args, state_lru[:, i])
        o_ctx = _attention(q, kr, v, 0, BATCH, SEQ, None, att_lam[i], att_subln[i], lam_init)
        ctx = (cache_k[:, i].reshape(DEC_BATCH, PAST_LEN, ATT_WIDTH), cache_v[:, i].reshape(DEC_BATCH, PAST_LEN, ATT_WIDTH))
        o_lat = _attention(q, kr, v, N_CTX_TOK, DEC_BATCH, DEC_SEQ, ctx, att_lam[i], att_subln[i], lam_init)
        bias = jnp.repeat(a_bs[i].T, CHUNK, axis=1)
        x = _merge(x, au, av, hs_ctx, hs_lat, rg, o_ctx, o_lat, br, mod[i], a_norm[i], a_ws[i], bias, w_up_a[i].astype(BF16), w_up_b[i].astype(BF16),
                   w_up_c[i].astype(BF16), w_out[i].astype(BF16))
        wq = peer_wq[i].astype(BF16).reshape(D_MODEL, PEER_HEADS, PEER_QDIM).transpose(1, 0, 2)
        p = _peer(x, mod[i], norm2[i], wq, peer_keys[i], _pack_table(peer_u, i), _pack_table(peer_v, i))
        res = (p, mod[i])
        ks.append(k[:N_CTX_TOK].reshape(BATCH, SEQ, ATT_HEADS, ATT_VD))
        vs.append(v[:N_CTX_TOK].reshape(BATCH, SEQ, ATT_HEADS, ATT_VD))
        hs.append(hl_ctx)
    y = _final(x, res[0], res[1], final_norm)
    return (y[:N_CTX_TOK].reshape(BATCH, SEQ, D_MODEL), y[N_CTX_TOK:].reshape(DEC_BATCH, DEC_SEQ, D_MODEL),
            jnp.stack(ks, axis=1), jnp.stack(vs, axis=1), jnp.stack(hs, axis=1))
```

```python
import functools
import math

import jax
import jax.numpy as jnp
from jax import lax
from jax.experimental import pallas as pl
from jax.experimental.pallas import tpu as pltpu

D_MODEL = 1024
BATCH = 16
SEQ = 256
DEPTH = 2
DEC_BATCH = 4
DEC_SEQ = 2048
PAST_LEN = 512
GRID_W = 64
EPS = 1e-6
N_MOD = 6
CHUNK = 128
A_GROUPS = 4
A_WIDTH = 512
LRU_BLOCKS = 8
LRU_BLOCK_W = 64
LRU_WIDTH = 512
CONV_W = 4
LRU_C = 8.0
ATT_HEADS = 4
ATT_HD = 64
ATT_VD = 128
ATT_WIDTH = 512
ROPE_BASE = 10000.0
ROPE_FREQS = 16
N_BRANCH = 3
IN_SPLITS = (512, 1024, 1536, 2048, 2560, 3072, 3584)
IN_WIDTH = 3584 + N_BRANCH * D_MODEL
PEER_HEADS = 8
N_KEYS = 128
N_EXPERTS = N_KEYS * N_KEYS
PEER_QDIM = 256
PEER_HALF = 128
PEER_TOPK = 16
N_PICK = PEER_HEADS * PEER_TOPK
CAND_COLS = tuple(PEER_TOPK // (a + 1) for a in range(PEER_TOPK))
CAND_START = tuple(sum(CAND_COLS[:a]) for a in range(PEER_TOPK))
N_CAND = sum(CAND_COLS)
CAND_PAD = -N_CAND % 8

N_CTX_TOK = BATCH * SEQ
N_LAT_TOK = DEC_BATCH * DEC_SEQ
N_TOK = N_CTX_TOK + N_LAT_TOK
TM = 256
N_TILES = N_TOK // TM
N_THIRD = N_TOK // 3
THIRD_TILES = N_THIRD // TM
CTX_TILES = N_CTX_TOK // TM
LAT_TILES_PER_SEQ = DEC_SEQ // TM
N_COND = 8
TP = 128
TOKEN_UNROLL = 16
PACK_ROWS = 1024
ROW_WORDS = D_MODEL // 2
ROW_SUB = ROW_WORDS // 128
PLANE_STRIDE = N_PICK + 8
VMEM_LIMIT = 56 * 1024 * 1024

F32 = jnp.float32
BF16 = jnp.bfloat16
HI = lax.Precision.HIGHEST


def _cond_row(i):
    return jnp.maximum(i - LAT_TILES_PER_SEQ, 0) // LAT_TILES_PER_SEQ


def _pos_block(i):
    return jnp.where(i < CTX_TILES, 0, 1 + i % LAT_TILES_PER_SEQ)


def _mod_spec(chunk):
    return pl.BlockSpec((None, 1, D_MODEL), lambda i: (_cond_row(i), 0, chunk))


def _third_specs(width):
    return [pl.BlockSpec((TM, width), lambda i, k=k: (jnp.clip(i - k * THIRD_TILES, 0, THIRD_TILES - 1), 0))
            for k in range(3)]


def _pick_third(refs):
    i = pl.program_id(0)
    return jnp.where(i < THIRD_TILES, refs[0][...], jnp.where(i < 2 * THIRD_TILES, refs[1][...], refs[2][...]))


def _const_spec(shape):
    nd = len(shape)
    return pl.BlockSpec(shape, lambda *_: (0,) * nd)


def _rms(x, gain):
    return x * lax.rsqrt(jnp.mean(x * x, axis=-1, keepdims=True) + EPS) * gain


def _dot(a, b):
    return jnp.dot(a.astype(BF16), b.astype(BF16), preferred_element_type=F32)


def _dot_nt(a, b):
    return lax.dot_general(a.astype(BF16), b.astype(BF16), (((1,), (1,)), ((), ())), preferred_element_type=F32)


def _mod_kernel(cond_ref, w_ref, b_ref, o_ref):
    cond = cond_ref[...]
    act = cond * jax.nn.sigmoid(cond)
    o_ref[...] = jnp.dot(act, w_ref[...], preferred_element_type=F32, precision=HI) + b_ref[...]


def _modulation(cond, w_mod, b_mod):
    nc = 4
    cw = N_MOD * D_MODEL // nc
    return pl.pallas_call(
        _mod_kernel,
        grid=(DEPTH, nc),
        in_specs=[pl.BlockSpec((N_COND, D_MODEL), lambda l, j: (0, 0)),
                  pl.BlockSpec((None, D_MODEL, cw), lambda l, j: (l, 0, j)),
                  pl.BlockSpec((None, 1, cw), lambda l, j: (l, 0, j))],
        out_specs=pl.BlockSpec((None, N_COND, cw), lambda l, j: (l, 0, j)),
        out_shape=jax.ShapeDtypeStruct((DEPTH, N_COND, N_MOD * D_MODEL), F32),
        compiler_params=pltpu.CompilerParams(vmem_limit_bytes=VMEM_LIMIT),
        name="modulation",
    )(cond, w_mod, b_mod.reshape(DEPTH, 1, N_MOD * D_MODEL))


def _inproj_kernel(has_res, *refs):
    if has_res:
        x_ref, g2_ref = refs[0], refs[4]
        x = x_ref[...] + g2_ref[...] * _pick_third(refs[1:4])
        refs = refs[5:]
    else:
        x_ref = refs[0]
        refs = refs[1:]
        x = x_ref[...]
    (sh_ref, sc_ref, n1_ref, w_ref, cos_ref, sin_ref,
     au_ref, av_ref, rx_ref, rg_ref, q_ref, k_ref, kr_ref, v_ref, br_ref) = refs[:15]
    if has_res:
        refs[15][...] = x
    h = (_rms(x, n1_ref[...]) * (1.0 + sc_ref[...]) + sh_ref[...]).astype(BF16)

    def proj(lo, hi):
        return jnp.dot(h, w_ref[:, lo:hi], preferred_element_type=F32)

    au_ref[...] = proj(0, IN_SPLITS[0])
    av_ref[...] = proj(IN_SPLITS[0], IN_SPLITS[1])
    rx_ref[...] = proj(IN_SPLITS[1], IN_SPLITS[2])
    rg_ref[...] = proj(IN_SPLITS[2], IN_SPLITS[3])
    q = proj(IN_SPLITS[3], IN_SPLITS[4])
    k = proj(IN_SPLITS[4], IN_SPLITS[5])
    v_ref[...] = proj(IN_SPLITS[5], IN_SPLITS[6])
    for j in range(N_BRANCH):
        lo = IN_SPLITS[6] + j * D_MODEL
        br_ref[:, j * D_MODEL:(j + 1) * D_MODEL] = proj(lo, lo + D_MODEL)
    k_ref[...] = k
    lane = lax.broadcasted_iota(jnp.int32, (TM, ATT_WIDTH), 1)
    first = (lane % (2 * ROPE_FREQS)) < ROPE_FREQS
    cos = cos_ref[...]
    sin = sin_ref[...]

    def rot(t):
        partner = jnp.where(first, pltpu.roll(t, ATT_WIDTH - ROPE_FREQS, 1), pltpu.roll(t, ROPE_FREQS, 1))
        return t * cos + partner * sin

    q_ref[...] = rot(q) * (ATT_HD ** -0.5)
    kr_ref[...] = rot(k)


def _inproj(x, res, mod_l, norm1, w_in, cos_t, sin_t):
    has_res = res is not None
    tok = lambda w: pl.BlockSpec((TM, w), lambda i: (i, 0))
    in_specs = [tok(D_MODEL)]
    args = [x]
    if has_res:
        p, mod_prev = res
        in_specs += _third_specs(D_MODEL) + [_mod_spec(5)]
        args += [*p, mod_prev]
    in_specs += [_mod_spec(0), _mod_spec(1), _const_spec((1, D_MODEL)),
                 pl.BlockSpec((D_MODEL, IN_WIDTH), lambda i: (0, 0), pipeline_mode=pl.Buffered(1)),
                 pl.BlockSpec((TM, ATT_WIDTH), lambda i: (_pos_block(i), 0)),
                 pl.BlockSpec((TM, ATT_WIDTH), lambda i: (_pos_block(i), 0))]
    args += [mod_l, mod_l, norm1.reshape(1, D_MODEL), w_in, cos_t, sin_t]
    widths = [512] * 8 + [N_BRANCH * D_MODEL]
    out_specs = [tok(w) for w in widths]
    out_shape = [jax.ShapeDtypeStruct((N_TOK, w), F32) for w in widths]
    if has_res:
        out_specs.append(tok(D_MODEL))
        out_shape.append(jax.ShapeDtypeStruct((N_TOK, D_MODEL), F32))
    return pl.pallas_call(
        functools.partial(_inproj_kernel, has_res),
        grid=(N_TILES,),
        in_specs=in_specs,
        out_specs=out_specs,
        out_shape=out_shape,
        compiler_params=pltpu.CompilerParams(vmem_limit_bytes=VMEM_LIMIT),
        name="inproj",
    )(*args)


def _lru_kernel(seq, x_ref, cw_ref, cb_ref, wr_ref, br_ref, wi_ref, bi_ref, lam_ref, h0_ref, hs_ref, hl_ref):
    x = x_ref[...]
    t = lax.broadcasted_iota(jnp.int32, (seq, 128), 0)

    def shifted(v, k, fill):
        r = pltpu.roll(v, k % seq, 0)
        ok = (t >= k) if k > 0 else (t < seq + k)
        return jnp.where(ok, r, fill)

    cw = cw_ref[...]
    xc = (shifted(x, 2, 0.0) * cw[0:1] + shifted(x, 1, 0.0) * cw[1:2] + x * cw[2:3]
          + shifted(x, -1, 0.0) * cw[3:4] + cb_ref[...])

    xc_hi, xc_lo = _split_bf16(xc)

    def gate_dot(w):
        w_hi, w_lo = _split_bf16(w)
        return (jnp.dot(xc_hi, w_hi, preferred_element_type=F32) + jnp.dot(xc_lo, w_hi, preferred_element_type=F32)
                + jnp.dot(xc_hi, w_lo, preferred_element_type=F32))

    def direction(d):
        r = jax.nn.sigmoid(gate_dot(wr_ref[d, 0]) + br_ref[d:d + 1])
        g = jax.nn.sigmoid(gate_dot(wi_ref[d, 0]) + bi_ref[d:d + 1])
        z = -lam_ref[d:d + 1]
        softplus = jnp.maximum(z, 0.0) + jnp.log(1.0 + jnp.exp(-jnp.abs(z)))
        log_a = -LRU_C * r * softplus
        a = jnp.exp(log_a)
        b = jnp.sqrt(1.0 - jnp.exp(2.0 * log_a)) * (g * xc)
        sgn = 1 if d == 0 else -1
        local = t % 8
        for k in (1, 2, 4):
            inside = (local >= k) if d == 0 else (local < 8 - k)
            a_prev = jnp.where(inside, pltpu.roll(a, (sgn * k) % seq, 0), 1.0)
            b_prev = jnp.where(inside, pltpu.roll(b, (sgn * k) % seq, 0), 0.0)
            b = a * b_prev + b
            a = a * a_prev
        edge = 7 if d == 0 else 0
        h = h0_ref[0, d:d + 1]
        tiles = range(seq // 8) if d == 0 else reversed(range(seq // 8))
        for j in tiles:
            rows = slice(8 * j, 8 * j + 8)
            h_tile = a[rows] * h + b[rows]
            if d == 0:
                hs_ref[rows, :] = h_tile
            else:
                hs_ref[rows, :] += h_tile
            h = h_tile[edge:edge + 1]
        return h

    h_f = direction(0)
    h_b = direction(1)
    hl_ref[0] = jnp.concatenate([h_f, h_b], axis=0)


def _lru(rx, row0, nseq, seq, conv_w, conv_b, w_r, b_r, w_i, b_i, lam, h0):
    nb = LRU_WIDTH // 128
    blk0 = row0 // seq
    par = lambda shape: pl.BlockSpec(shape, lambda b, c: (0,) * (len(shape) - 1) + (c,))
    return pl.pallas_call(
        functools.partial(_lru_kernel, seq),
        grid=(nseq, nb),
        in_specs=[pl.BlockSpec((seq, 128), lambda b, c: (blk0 + b, c)),
                  par((CONV_W, 128)), par((1, 128)),
                  pl.BlockSpec((2, 1, 128, 128), lambda b, c: (0, c, 0, 0)), par((2, 128)),
                  pl.BlockSpec((2, 1, 128, 128), lambda b, c: (0, c, 0, 0)), par((2, 128)),
                  par((2, 128)),
                  pl.BlockSpec((1, 2, 128), lambda b, c: (b, 0, c))],
        out_specs=[pl.BlockSpec((seq, 128), lambda b, c: (b, c)),
                   pl.BlockSpec((1, 2, 128), lambda b, c: (b, 0, c))],
        out_shape=[jax.ShapeDtypeStruct((nseq * seq, LRU_WIDTH), F32),
                   jax.ShapeDtypeStruct((nseq, 2, LRU_WIDTH), F32)],
        compiler_params=pltpu.CompilerParams(vmem_limit_bytes=VMEM_LIMIT),
        name="rglru",
    )(rx, conv_w, conv_b.reshape(1, LRU_WIDTH), w_r, b_r, w_i, b_i, lam, h0)


def _blockdiag128(w):
    w = w.reshape(2, LRU_BLOCKS // 2, 2, LRU_BLOCK_W, LRU_BLOCK_W)
    z = jnp.zeros_like(w[:, :, 0])
    top = jnp.concatenate([w[:, :, 0], z], axis=-1)
    bot = jnp.concatenate([z, w[:, :, 1]], axis=-1)
    return jnp.concatenate([top, bot], axis=-2)


def _attn_kernel(has_ctx, lam_init, *refs):
    if has_ctx:
        q_ref, k_ref, v_ref, kc_ref, vc_ref, lp_ref, g_ref, o_ref = refs
    else:
        q_ref, k_ref, v_ref, lp_ref, g_ref, o_ref = refs
    lp = lp_ref[...]
    lam = (jnp.exp(jnp.sum(lp[0:1] * lp[1:2], axis=-1, keepdims=True))
           - jnp.exp(jnp.sum(lp[2:3] * lp[3:4], axis=-1, keepdims=True)) + lam_init)
    q = q_ref[...]
    lane = lax.broadcasted_iota(jnp.int32, q.shape, 1)
    halves = (jnp.where(lane < ATT_HD, q, 0.0), jnp.where(lane >= ATT_HD, q, 0.0))
    k = k_ref[...]
    v = v_ref[...]
    w = []
    for qh in halves:
        s = _dot_nt(qh, k)
        m = jnp.max(s, axis=-1, keepdims=True)
        if has_ctx:
            sc = _dot_nt(qh, kc_ref[0])
            m = jnp.maximum(m, jnp.max(sc, axis=-1, keepdims=True))
            ec = jnp.exp(sc - m)
        e = jnp.exp(s - m)
        den = jnp.sum(e, axis=-1, keepdims=True)
        if has_ctx:
            den = den + jnp.sum(ec, axis=-1, keepdims=True)
            w.append((e / den, ec / den))
        else:
            w.append((e / den,))
    o = _dot(w[0][0] - lam * w[1][0], v)
    if has_ctx:
        o = o + _dot(w[0][1] - lam * w[1][1], vc_ref[0])
    o_ref[...] = _rms(o, g_ref[...]) * (1.0 - lam_init)


def _attention(q, kr, v, row0, nseq, seq, ctx, att_lam, subln, lam_init):
    has_ctx = ctx is not None
    nq = seq // TM
    blk0 = row0 // seq
    in_specs = [pl.BlockSpec((TM, ATT_VD), lambda b, h, i: ((row0 // TM) + b * nq + i, h)),
                pl.BlockSpec((seq, ATT_VD), lambda b, h, i: (blk0 + b, h)),
                pl.BlockSpec((seq, ATT_VD), lambda b, h, i: (blk0 + b, h))]
    args = [q, kr, v]
    if has_ctx:
        in_specs += [pl.BlockSpec((1, PAST_LEN, ATT_VD), lambda b, h, i: (b, 0, h))] * 2
        args += list(ctx)
    in_specs += [_const_spec((4, ATT_HD)), _const_spec((1, ATT_VD))]
    args += [att_lam, subln.reshape(1, ATT_VD)]
    return pl.pallas_call(
        functools.partial(_attn_kernel, has_ctx, lam_init),
        grid=(nseq, ATT_HEADS, nq),
        in_specs=in_specs,
        out_specs=pl.BlockSpec((TM, ATT_VD), lambda b, h, i: (b * nq + i, h)),
        out_shape=jax.ShapeDtypeStruct((nseq * seq, ATT_WIDTH), F32),
        compiler_params=pltpu.CompilerParams(vmem_limit_bytes=VMEM_LIMIT),
        name="diffattn",
    )(*args)


def _merge_kernel(x_ref, au_ref, av_ref, hsc_ref, hsl_ref, rg_ref, oc_ref, ol_ref, br_ref, g1_ref, an_ref, ws_ref,
                  bs_ref, wa_ref, wb_ref, wc_ref, wo_ref, xo_ref):
    is_ctx = pl.program_id(0) < CTX_TILES
    hs = jnp.where(is_ctx, hsc_ref[...], hsl_ref[...])
    o = jnp.where(is_ctx, oc_ref[...], ol_ref[...])
    vn = _rms(av_ref[...], an_ref[...])
    rows = []
    for c in range(TM // CHUNK):
        cols = []
        for g in range(A_GROUPS):
            blk = vn[c * CHUNK:(c + 1) * CHUNK, g * 128:(g + 1) * 128]
            cols.append(_dot(ws_ref[g], blk))
        rows.append(jnp.concatenate(cols, axis=1) + bs_ref[...])
    y_a = au_ref[...] * jnp.concatenate(rows, axis=0)
    y_b = hs * jax.nn.gelu(rg_ref[...])
    merged = (jax.nn.sigmoid(br_ref[:, 0:D_MODEL]) * _dot(y_a, wa_ref[...])
              + jax.nn.sigmoid(br_ref[:, D_MODEL:2 * D_MODEL]) * _dot(y_b, wb_ref[...])
              + jax.nn.sigmoid(br_ref[:, 2 * D_MODEL:3 * D_MODEL]) * _dot(o, wc_ref[...]))
    xo_ref[...] = x_ref[...] + g1_ref[...] * _dot(merged, wo_ref[...])


def _merge(x, au, av, hs_ctx, hs_lat, rg, o_ctx, o_lat, br, mod_l, a_norm, a_ws, bias, wa, wb, wc, wo):
    tok = lambda w: pl.BlockSpec((TM, w), lambda i: (i, 0))
    ctx = pl.BlockSpec((TM, 512), lambda i: (jnp.minimum(i, CTX_TILES - 1), 0))
    lat = pl.BlockSpec((TM, 512), lambda i: (jnp.maximum(i - CTX_TILES, 0), 0))
    return pl.pallas_call(
        _merge_kernel,
        grid=(N_TILES,),
        in_specs=[tok(D_MODEL), tok(512), tok(512), ctx, lat, tok(512), ctx, lat, tok(N_BRANCH * D_MODEL),
                  _mod_spec(2), _const_spec((1, A_WIDTH)), _const_spec((A_GROUPS, CHUNK, CHUNK)),
                  _const_spec((CHUNK, A_WIDTH)),
                  _const_spec((A_WIDTH, D_MODEL)), _const_spec((LRU_WIDTH, D_MODEL)),
                  _const_spec((ATT_WIDTH, D_MODEL)), _const_spec((D_MODEL, D_MODEL))],
        out_specs=tok(D_MODEL),
        out_shape=jax.ShapeDtypeStruct((N_TOK, D_MODEL), F32),
        compiler_params=pltpu.CompilerParams(vmem_limit_bytes=VMEM_LIMIT),
        name="merge",
    )(x, au, av, hs_ctx, hs_lat, rg, o_ctx, o_lat, br, mod_l, a_norm.reshape(1, A_WIDTH), a_ws, bias, wa, wb, wc, wo)


def _top16(s, n):
    pos = lax.broadcasted_iota(jnp.int32, s.shape, 0).astype(F32)
    vals, idxs = [], []
    for _ in range(PEER_TOPK):
        m = jnp.max(s, axis=0, keepdims=True)
        am = jnp.min(jnp.where(s == m, pos, float(n)), axis=0, keepdims=True)
        vals.append(m)
        idxs.append(am)
        s = jnp.where(pos == am, -jnp.inf, s)
    return jnp.concatenate(vals, axis=0), jnp.concatenate(idxs, axis=0)


def _take16(table, sel):
    out = jnp.zeros_like(table)
    for a in range(PEER_TOPK):
        out = jnp.where(sel == float(a), table[a:a + 1], out)
    return out


def _route_hidden(x_ref, sh_ref, sc_ref, n2_ref):
    return _rms(x_ref[...], n2_ref[...]) * (1.0 + sc_ref[...]) + sh_ref[...]


def _route_head(h2b, wq_h, keys_ref, idx_ref, gate_ref, rows):
    tn = h2b.shape[0]
    q = jnp.dot(h2b, wq_h, preferred_element_type=F32)
    v1, i1 = _top16(_dot_nt(keys_ref[0], q[:, 0:PEER_HALF]), N_KEYS)
    v2, i2 = _top16(_dot_nt(keys_ref[1], q[:, PEER_HALF:PEER_QDIM]), N_KEYS)
    cand = jnp.concatenate([v1[a:a + 1] + v2[0:n] for a, n in enumerate(CAND_COLS)]
                           + [jnp.full((CAND_PAD, tn), -jnp.inf, F32)], axis=0)
    top_s, pos = _top16(cand, N_CAND + CAND_PAD)
    a_sel = jnp.zeros_like(pos)
    b_sel = pos
    for a in range(1, PEER_TOPK):
        later = pos >= float(CAND_START[a])
        a_sel = a_sel + jnp.where(later, 1.0, 0.0)
        b_sel = b_sel - jnp.where(later, float(CAND_COLS[a - 1]), 0.0)
    expert = _take16(i1, a_sel) * N_KEYS + _take16(i2, b_sel)
    e = jnp.exp(top_s - top_s[0:1])
    gate_ref[rows, :] = e / jnp.sum(e, axis=0, keepdims=True)
    idx_ref[rows, :] = expert * float(ROW_SUB)


def _route_finish(it_ref, gt_ref, idx_ref, gate_ref):
    idx_ref[...] = it_ref[...].T.astype(jnp.int32)
    gate_ref[...] = gt_ref[...].T


def _route_kernel(x_ref, sh_ref, sc_ref, n2_ref, wq_ref, keys_ref, h2_ref, idx_ref, gate_ref, it_ref, gt_ref):
    h2 = _route_hidden(x_ref, sh_ref, sc_ref, n2_ref)
    h2_ref[...] = h2
    h2b = h2.astype(BF16)
    for h in range(PEER_HEADS):
        _route_head(h2b, wq_ref[h], keys_ref, it_ref, gt_ref, slice(h * PEER_TOPK, (h + 1) * PEER_TOPK))
    _route_finish(it_ref, gt_ref, idx_ref, gate_ref)


def _route_out_shape():
    return [jax.ShapeDtypeStruct((N_THIRD, D_MODEL), F32),
            jax.ShapeDtypeStruct((N_THIRD, N_PICK), jnp.int32),
            jax.ShapeDtypeStruct((N_THIRD, N_PICK), F32)]


def _route_scratch(tokens):
    return [pltpu.VMEM((N_PICK, tokens), F32), pltpu.VMEM((N_PICK, tokens), F32)]


def _route_first(x, mod_l, norm2, wq, keys):
    tok = pl.BlockSpec((TM, D_MODEL), lambda i: (i, 0))
    pick = pl.BlockSpec((TM, N_PICK), lambda i: (i, 0))
    return pl.pallas_call(
        _route_kernel,
        grid=(N_THIRD // TM,),
        in_specs=[tok, _mod_spec(3), _mod_spec(4), _const_spec((1, D_MODEL)),
                  _const_spec((PEER_HEADS, D_MODEL, PEER_QDIM)), _const_spec((2, N_KEYS, PEER_HALF))],
        out_specs=[tok, pick, pick],
        out_shape=_route_out_shape(),
        scratch_shapes=_route_scratch(TM),
        compiler_params=pltpu.CompilerParams(vmem_limit_bytes=VMEM_LIMIT),
        name="peer_route",
    )(x, mod_l, mod_l, norm2.reshape(1, D_MODEL), wq, keys)


def _route_side_specs(third):
    off = third * (N_THIRD // TP)

    def mod(chunk):
        return pl.BlockSpec((None, 1, D_MODEL), lambda i: (_cond_row((off + i) * TP // TM), 0, chunk))

    in_specs = [pl.BlockSpec((TP, D_MODEL), lambda i: (off + i, 0)), mod(3), mod(4), _const_spec((1, D_MODEL)),
                pl.BlockSpec((PEER_HEADS, D_MODEL, PEER_QDIM), lambda i: (0, 0, 0), pipeline_mode=pl.Buffered(1)),
                _const_spec((2, N_KEYS, PEER_HALF))]
    pick = pl.BlockSpec((TP, N_PICK), lambda i: (i, 0))
    out_specs = [pl.BlockSpec((TP, D_MODEL), lambda i: (i, 0)), pick, pick]
    return in_specs, out_specs


def _route_side(in_refs, out_refs, scratch):
    x_ref, sh_ref, sc_ref, n2_ref, wq_ref, keys_ref = in_refs
    h2_ref, idx_ref, gate_ref = out_refs
    h2b_ref, it_ref, gt_ref = scratch
    h2 = _route_hidden(x_ref, sh_ref, sc_ref, n2_ref)
    h2_ref[...] = h2
    h2b_ref[...] = h2.astype(BF16)

    def side(h):
        rows = pl.ds(pl.multiple_of(h * PEER_TOPK, PEER_TOPK), PEER_TOPK)
        _route_head(h2b_ref[...], wq_ref[h], keys_ref, it_ref, gt_ref, rows)

    return side, functools.partial(_route_finish, it_ref, gt_ref, idx_ref, gate_ref)


def _pack_kernel(t_ref, o_ref):
    def rounded_bits(v):
        return lax.bitcast_convert_type(v.astype(BF16).astype(F32), jnp.int32)

    lo = lax.shift_right_logical(rounded_bits(t_ref[:, 0:ROW_WORDS]), 16)
    hi = rounded_bits(t_ref[:, ROW_WORDS:D_MODEL]) & jnp.int32(-65536)
    words = lo | hi
    for g in range(PACK_ROWS // 8):
        for s in range(ROW_SUB):
            o_ref[pl.ds(8 * g * ROW_SUB + s, 8, stride=ROW_SUB), :] = words[8 * g:8 * g + 8, s * 128:(s + 1) * 128]


def _pack_table(tabs, layer):
    return pl.pallas_call(
        _pack_kernel,
        grid=(N_EXPERTS // PACK_ROWS,),
        in_specs=[pl.BlockSpec((None, PACK_ROWS, D_MODEL), lambda i: (layer, i, 0))],
        out_specs=pl.BlockSpec((PACK_ROWS * ROW_SUB, 128), lambda i: (i, 0)),
        out_shape=jax.ShapeDtypeStruct((N_EXPERTS * ROW_SUB, 128), jnp.int32),
        name="pack_table",
    )(tabs)


def _unpack(words):
    lo = lax.bitcast_convert_type(words << 16, F32)
    hi = lax.bitcast_convert_type(words & jnp.int32(-65536), F32)
    return lo, hi


def _gather_planes(tab_ref, idx_ref, t, g_ref):
    for k in range(N_PICK):
        row = pl.multiple_of(idx_ref[t, k], ROW_SUB)
        g_ref[pl.ds(k, ROW_SUB, stride=PLANE_STRIDE), :] = tab_ref[pl.ds(row, ROW_SUB), :]


def _plane(g_ref, s):
    return g_ref[s * PLANE_STRIDE:s * PLANE_STRIDE + N_PICK, :]


def _split_bf16(v):
    hi = v.astype(BF16)
    return hi, (v - hi.astype(F32)).astype(BF16)


def _pipelined_tokens(tab_ref, idx_ref, bufs, compute, side):
    _gather_planes(tab_ref, idx_ref, 0, bufs[0])

    def group(j, carry):
        if side is not None:
            side(j)
        for p in range(TOKEN_UNROLL):
            t = TOKEN_UNROLL * j + p
            compute(t, pl.multiple_of(TOKEN_UNROLL * j + p // 8 * 8, 8), p % 8, bufs[p % 2])
            _gather_planes(tab_ref, idx_ref, jnp.minimum(t + 1, TP - 1), bufs[(p + 1) % 2])
        return carry

    lax.fori_loop(0, TP // TOKEN_UNROLL, group, 0)


def _split_refs(with_route, refs, n_in, n_scratch):
    n_rin, n_rout = (6, 3) if with_route else (0, 0)
    cuts = [n_in, n_rin, 1, n_rout, n_scratch]
    parts, at = [], 0
    for n in cuts:
        parts.append(refs[at:at + n])
        at += n
    parts.append(refs[at:])
    return parts


def _peer_act_kernel(with_route, idx_ref, x_ref, gate_ref, tab_ref, o_ref, ga_ref, gb_ref):
    assert not with_route
    ones = jnp.ones((8, 2 * 128), BF16)

    def token(t, t8, r, g_ref):
        def x_row(j):
            return x_ref[pl.ds(t8, 8), j * 128:(j + 1) * 128][r:r + 1]

        acc = jnp.zeros((N_PICK, 128), F32)
        for s in range(ROW_SUB):
            lo, hi = _unpack(_plane(g_ref, s))
            acc = acc + lo * x_row(s) + hi * x_row(ROW_SUB + s)
        act = _dot_nt(ones, jnp.concatenate(_split_bf16(acc), axis=1))
        o_ref[pl.ds(t, 1), :] = jax.nn.gelu(act[0:1]) * gate_ref[pl.ds(t, 1), :]

    _pipelined_tokens(tab_ref, idx_ref, (ga_ref, gb_ref), token, None)


def _peer_out_kernel(with_route, *refs):
    (idx_ref, coef_ref, tab_ref), rin, (o_ref,), rout, (ga_ref, gb_ref, st_ref), rscratch = _split_refs(
        with_route, refs, 3, 3)
    side, finish = _route_side(rin, rout, rscratch) if with_route else (None, None)

    def token(t, t8, r, g_ref):
        c_hi, c_lo = _split_bf16(coef_ref[t])
        lhs = jnp.concatenate([c_hi, c_lo, jnp.zeros((4, 2 * N_PICK), BF16)], axis=0)
        for s in range(ROW_SUB):
            w = pltpu.bitcast(_plane(g_ref, s), BF16)
            acc = jnp.dot(lhs, w, preferred_element_type=F32)
            st_ref[8 * s + r:8 * s + r + 1, :] = acc[0:1] + acc[2:3]
            st_ref[8 * (ROW_SUB + s) + r:8 * (ROW_SUB + s) + r + 1, :] = acc[1:2] + acc[3:4]
        if r == 7:
            for j in range(D_MODEL // 128):
                o_ref[pl.ds(t8, 8), j * 128:(j + 1) * 128] = st_ref[8 * j:8 * j + 8, :]

    _pipelined_tokens(tab_ref, idx_ref, (ga_ref, gb_ref), token, side)
    if with_route:
        finish()


def _expert_call(body, name, in_specs, args, out_spec, out_width, scratch, side):
    out_specs, out_shape = [out_spec], [jax.ShapeDtypeStruct((N_THIRD, out_width), F32)]
    if side is not None:
        assert TP // TOKEN_UNROLL == PEER_HEADS
        third, x, mod_l, norm2, wq, keys = side
        rin, rout = _route_side_specs(third)
        in_specs = in_specs + rin
        args = args + (x, mod_l, mod_l, norm2.reshape(1, D_MODEL), wq, keys)
        out_specs = out_specs + rout
        out_shape = out_shape + _route_out_shape()
        scratch = scratch + [pltpu.VMEM((TP, D_MODEL), BF16)] + _route_scratch(TP)
    outs = pl.pallas_call(
        functools.partial(body, side is not None),
        grid=(N_THIRD // TP,),
        in_specs=in_specs,
        out_specs=out_specs,
        out_shape=out_shape,
        scratch_shapes=scratch,
        compiler_params=pltpu.CompilerParams(vmem_limit_bytes=VMEM_LIMIT),
        name=name,
    )(*args)
    return outs[0], tuple(outs[1:])


_SMEM_IDX = pl.BlockSpec((TP, N_PICK), lambda i: (i, 0), memory_space=pltpu.SMEM)
_TABLE = pl.BlockSpec((N_EXPERTS * ROW_SUB, 128), lambda i: (0, 0), pipeline_mode=pl.Buffered(1))
_PLANES = pltpu.VMEM((ROW_SUB * PLANE_STRIDE, 128), jnp.int32)


def _peer_act(idx, h2, gate, u_words):
    rows = pl.BlockSpec((TP, D_MODEL), lambda i: (i, 0))
    pick = pl.BlockSpec((TP, N_PICK), lambda i: (i, 0))
    coef, _ = _expert_call(_peer_act_kernel, "peer_act", [_SMEM_IDX, rows, pick, _TABLE],
                           (idx, h2, gate, u_words), pick, N_PICK, [_PLANES, _PLANES], None)
    zero = jnp.zeros_like(coef)
    coef = jnp.stack([jnp.stack([coef, zero], axis=-1), jnp.stack([zero, coef], axis=-1)], axis=1)
    return coef.reshape(N_THIRD, 2, 2 * N_PICK)


def _peer_out(idx, coef, v_words, side=None):
    rows = pl.BlockSpec((TP, D_MODEL), lambda i: (i, 0))
    pair = pl.BlockSpec((TP, 2, 2 * N_PICK), lambda i: (i, 0, 0))
    stage = pltpu.VMEM((8 * (D_MODEL // 128), 128), F32)
    return _expert_call(_peer_out_kernel, "peer_out", [_SMEM_IDX, pair, _TABLE], (idx, coef, v_words),
                        rows, D_MODEL, [_PLANES, _PLANES, stage], side)


def _peer(x, mod_l, norm2, wq, keys, u_words, v_words):
    route_args = (x, mod_l, norm2, wq, keys)
    h2, idx, gate = _route_first(*route_args)
    outs = []
    for third in range(3):
        coef = _peer_act(idx, h2, gate, u_words)
        side = (third + 1,) + route_args if third < 2 else None
        p, routed = _peer_out(idx, coef, v_words, side)
        outs.append(p)
        if side is not None:
            h2, idx, gate = routed
    return tuple(outs)


def _final_kernel(x_ref, pa_ref, pb_ref, pc_ref, g2_ref, n_ref, o_ref):
    o_ref[...] = _rms(x_ref[...] + g2_ref[...] * _pick_third((pa_ref, pb_ref, pc_ref)), n_ref[...])


def _final(x, p, mod_l, final_norm):
    tok = pl.BlockSpec((TM, D_MODEL), lambda i: (i, 0))
    return pl.pallas_call(
        _final_kernel,
        grid=(N_TILES,),
        in_specs=[tok] + _third_specs(D_MODEL) + [_mod_spec(5), _const_spec((1, D_MODEL))],
        out_specs=tok,
        out_shape=jax.ShapeDtypeStruct((N_TOK, D_MODEL), F32),
        name="final_norm",
    )(x, *p, mod_l, final_norm.reshape(1, D_MODEL))


def _rope_tables():
    rows = DEC_SEQ // GRID_W
    row_ids = jnp.repeat(jnp.arange(rows), GRID_W).astype(F32)
    col_ids = jnp.tile(jnp.arange(GRID_W), rows).astype(F32)
    inv_freq = ROPE_BASE ** (-jnp.arange(ROPE_FREQS, dtype=F32) / ROPE_FREQS)
    ang_r = row_ids[:, None] * inv_freq
    ang_c = col_ids[:, None] * inv_freq
    cos = jnp.concatenate([jnp.cos(ang_r), jnp.cos(ang_r), jnp.cos(ang_c), jnp.cos(ang_c)], axis=1)
    sin = jnp.concatenate([-jnp.sin(ang_r), jnp.sin(ang_r), -jnp.sin(ang_c), jnp.sin(ang_c)], axis=1)
    reps = ATT_WIDTH // ATT_HD
    cos = jnp.concatenate([jnp.ones((TM, ATT_WIDTH), F32), jnp.tile(cos, (1, reps))], axis=0)
    sin = jnp.concatenate([jnp.zeros((TM, ATT_WIDTH), F32), jnp.tile(sin, (1, reps))], axis=0)
    return cos, sin


def kernel(x_prompt, x_sample, cache_k, cache_v, state_lru, c, c_ctx, w_mod, b_mod, norm1, norm2, w_in, a_norm, a_ws, a_bs, lru_conv_w, lru_conv_b, lru_w_r, lru_b_r, lru_w_i, lru_b_i, lru_lam, att_lam, att_subln, w_up_a, w_up_b, w_up_c, w_out, peer_wq, peer_keys, peer_u, peer_v, final_norm):
    x = jnp.concatenate([x_prompt.reshape(N_CTX_TOK, D_MODEL), x_sample.reshape(N_LAT_TOK, D_MODEL)], axis=0)
    cond = jnp.concatenate([c_ctx[None, :], c, jnp.zeros((N_COND - 1 - DEC_BATCH, D_MODEL), F32)], axis=0)
    mod = _modulation(cond, w_mod, b_mod).reshape(DEPTH, N_COND, 1, N_MOD * D_MODEL)
    cos_t, sin_t = _rope_tables()
    zero_h0 = jnp.zeros((BATCH, 2, LRU_WIDTH), F32)
    ks, vs, hs = [], [], []
    res = None
    for i in range(DEPTH):
        lam_init = 0.8 - 0.6 * math.exp(-0.3 * i)
        outs = _inproj(x, res, mod[i], norm1[i], w_in[i].astype(BF16), cos_t, sin_t)
        au, av, rx, rg, q, k, kr, v, br = outs[:9]
        if res is not None:
            x = outs[9]
        w_r = _blockdiag128(lru_w_r[i])
        w_i = _blockdiag128(lru_w_i[i])
        lru_args = (lru_conv_w[i], lru_conv_b[i], w_r, lru_b_r[i], w_i, lru_b_i[i], lru_lam[i])
        hs_ctx, hl_ctx = _lru(rx, 0, BATCH, SEQ, *lru_args, zero_h0)
        hs_lat, _ = _lru(rx, N_CTX_TOK, DEC_BATCH, DEC_SEQ, *lru_args, state_lru[:, i])
        o_ctx = _attention(q, kr, v, 0, BATCH, SEQ, None, att_lam[i], att_subln[i], lam_init)
        ctx = (cache_k[:, i].reshape(DEC_BATCH, PAST_LEN, ATT_WIDTH), cache_v[:, i].reshape(DEC_BATCH, PAST_LEN, ATT_WIDTH))
        o_lat = _attention(q, kr, v, N_CTX_TOK, DEC_BATCH, DEC_SEQ, ctx, att_lam[i], att_subln[i], lam_init)
        bias = jnp.repeat(a_bs[i].T, CHUNK, axis=1)
        x = _merge(x, au, av, hs_ctx, hs_lat, rg, o_ctx, o_lat, br, mod[i], a_norm[i], a_ws[i], bias, w_up_a[i].astype(BF16), w_up_b[i].astype(BF16),
                   w_up_c[i].astype(BF16), w_out[i].astype(BF16))
        wq = peer_wq[i].astype(BF16).reshape(D_MODEL, PEER_HEADS, PEER_QDIM).transpose(1, 0, 2)
        p = _peer(x, mod[i], norm2[i], wq, peer_keys[i], _pack_table(peer_u, i), _pack_table(peer_v, i))
        res = (p, mod[i])
        ks.append(k[:N_CTX_TOK].reshape(BATCH, SEQ, ATT_HEADS, ATT_VD))
        vs.append(v[:N_CTX_TOK].reshape(BATCH, SEQ, ATT_HEADS, ATT_VD))
        hs.append(hl_ctx)
    y = _final(x, res[0], res[1], final_norm)
    return (y[:N_CTX_TOK].reshape(BATCH, SEQ, D_MODEL), y[N_CTX_TOK:].reshape(DEC_BATCH, DEC_SEQ, D_MODEL),
            jnp.stack(ks, axis=1), jnp.stack(vs, axis=1), jnp.stack(hs, axis=1))
```

```python
import functools
import math

import jax
import jax.numpy as jnp
from jax import lax
from jax.experimental import pallas as pl
from jax.experimental.pallas import tpu as pltpu

D_MODEL = 1024
BATCH = 16
SEQ = 256
DEPTH = 2
DEC_BATCH = 4
DEC_SEQ = 2048
PAST_LEN = 512
GRID_W = 64
EPS = 1e-6
N_MOD = 6
CHUNK = 128
A_GROUPS = 4
A_WIDTH = 512
LRU_BLOCKS = 8
LRU_BLOCK_W = 64
LRU_WIDTH = 512
CONV_W = 4
LRU_C = 8.0
ATT_HEADS = 4
ATT_HD = 64
ATT_VD = 128
ATT_WIDTH = 512
ROPE_BASE = 10000.0
ROPE_FREQS = 16
N_BRANCH = 3
IN_SPLITS = (512, 1024, 1536, 2048, 2560, 3072, 3584)
IN_WIDTH = 3584 + N_BRANCH * D_MODEL
PEER_HEADS = 8
N_KEYS = 128
N_EXPERTS = N_KEYS * N_KEYS
PEER_QDIM = 256
PEER_HALF = 128
PEER_TOPK = 16
N_PICK = PEER_HEADS * PEER_TOPK
CAND_COLS = tuple(PEER_TOPK // (a + 1) for a in range(PEER_TOPK))
CAND_START = tuple(sum(CAND_COLS[:a]) for a in range(PEER_TOPK))
N_CAND = sum(CAND_COLS)
CAND_PAD = -N_CAND % 8

N_CTX_TOK = BATCH * SEQ
N_LAT_TOK = DEC_BATCH * DEC_SEQ
N_TOK = N_CTX_TOK + N_LAT_TOK
TM = 256
N_TILES = N_TOK // TM
N_THIRD = N_TOK // 3
THIRD_TILES = N_THIRD // TM
CTX_TILES = N_CTX_TOK // TM
LAT_TILES_PER_SEQ = DEC_SEQ // TM
N_COND = 8
TP = 128
TOKEN_UNROLL = 16
PACK_ROWS = 1024
ROW_WORDS = D_MODEL // 2
ROW_SUB = ROW_WORDS // 128
PLANE_STRIDE = N_PICK + 8
VMEM_LIMIT = 56 * 1024 * 1024

F32 = jnp.float32
BF16 = jnp.bfloat16
HI = lax.Precision.HIGHEST


def _cond_row(i):
    return jnp.maximum(i - LAT_TILES_PER_SEQ, 0) // LAT_TILES_PER_SEQ


def _pos_block(i):
    return jnp.where(i < CTX_TILES, 0, 1 + i % LAT_TILES_PER_SEQ)


def _mod_spec(chunk):
    return pl.BlockSpec((None, 1, D_MODEL), lambda i: (_cond_row(i), 0, chunk))


def _third_specs(width):
    return [pl.BlockSpec((TM, width), lambda i, k=k: (jnp.clip(i - k * THIRD_TILES, 0, THIRD_TILES - 1), 0))
            for k in range(3)]


def _pick_third(refs):
    i = pl.program_id(0)
    return jnp.where(i < THIRD_TILES, refs[0][...], jnp.where(i < 2 * THIRD_TILES, refs[1][...], refs[2][...]))


def _const_spec(shape):
    nd = len(shape)
    return pl.BlockSpec(shape, lambda *_: (0,) * nd)


def _rms(x, gain):
    return x * lax.rsqrt(jnp.mean(x * x, axis=-1, keepdims=True) + EPS) * gain


def _sigmoid(z):
    return 0.5 * jnp.tanh(0.5 * z) + 0.5


def _dot(a, b):
    return jnp.dot(a.astype(BF16), b.astype(BF16), preferred_element_type=F32)


def _dot_nt(a, b):
    return lax.dot_general(a.astype(BF16), b.astype(BF16), (((1,), (1,)), ((), ())), preferred_element_type=F32)


def _mod_kernel(cond_ref, w_ref, b_ref, o_ref):
    cond = cond_ref[...]
    act = cond * jax.nn.sigmoid(cond)
    o_ref[...] = jnp.dot(act, w_ref[...], preferred_element_type=F32, precision=HI) + b_ref[...]


def _modulation(cond, w_mod, b_mod):
    nc = 4
    cw = N_MOD * D_MODEL // nc
    return pl.pallas_call(
        _mod_kernel,
        grid=(DEPTH, nc),
        in_specs=[pl.BlockSpec((N_COND, D_MODEL), lambda l, j: (0, 0)),
                  pl.BlockSpec((None, D_MODEL, cw), lambda l, j: (l, 0, j)),
                  pl.BlockSpec((None, 1, cw), lambda l, j: (l, 0, j))],
        out_specs=pl.BlockSpec((None, N_COND, cw), lambda l, j: (l, 0, j)),
        out_shape=jax.ShapeDtypeStruct((DEPTH, N_COND, N_MOD * D_MODEL), F32),
        compiler_params=pltpu.CompilerParams(vmem_limit_bytes=VMEM_LIMIT),
        name="modulation",
    )(cond, w_mod, b_mod.reshape(DEPTH, 1, N_MOD * D_MODEL))


def _inproj_kernel(has_res, *refs):
    if has_res:
        x_ref, g2_ref = refs[0], refs[4]
        x = x_ref[...] + g2_ref[...] * _pick_third(refs[1:4])
        refs = refs[5:]
    else:
        x_ref = refs[0]
        refs = refs[1:]
        x = x_ref[...]
    (sh_ref, sc_ref, n1_ref, w_ref, cos_ref, sin_ref,
     au_ref, av_ref, rx_ref, rg_ref, q_ref, k_ref, kr_ref, v_ref, br_ref) = refs[:15]
    if has_res:
        refs[15][...] = x
    h = (_rms(x, n1_ref[...]) * (1.0 + sc_ref[...]) + sh_ref[...]).astype(BF16)

    def proj(lo, hi):
        return jnp.dot(h, w_ref[:, lo:hi], preferred_element_type=F32)

    au_ref[...] = proj(0, IN_SPLITS[0])
    av_ref[...] = proj(IN_SPLITS[0], IN_SPLITS[1])
    rx_ref[...] = proj(IN_SPLITS[1], IN_SPLITS[2])
    rg_ref[...] = proj(IN_SPLITS[2], IN_SPLITS[3])
    q = proj(IN_SPLITS[3], IN_SPLITS[4])
    k = proj(IN_SPLITS[4], IN_SPLITS[5])
    v_ref[...] = proj(IN_SPLITS[5], IN_SPLITS[6])
    for j in range(N_BRANCH):
        lo = IN_SPLITS[6] + j * D_MODEL
        br_ref[:, j * D_MODEL:(j + 1) * D_MODEL] = proj(lo, lo + D_MODEL)
    k_ref[...] = k
    lane = lax.broadcasted_iota(jnp.int32, (TM, ATT_WIDTH), 1)
    first = (lane % (2 * ROPE_FREQS)) < ROPE_FREQS
    cos = cos_ref[...]
    sin = sin_ref[...]

    def rot(t):
        partner = jnp.where(first, pltpu.roll(t, ATT_WIDTH - ROPE_FREQS, 1), pltpu.roll(t, ROPE_FREQS, 1))
        return t * cos + partner * sin

    q_ref[...] = rot(q) * (ATT_HD ** -0.5)
    kr_ref[...] = rot(k)


def _inproj(x, res, mod_l, norm1, w_in, cos_t, sin_t):
    has_res = res is not None
    tok = lambda w: pl.BlockSpec((TM, w), lambda i: (i, 0))
    in_specs = [tok(D_MODEL)]
    args = [x]
    if has_res:
        p, mod_prev = res
        in_specs += _third_specs(D_MODEL) + [_mod_spec(5)]
        args += [*p, mod_prev]
    in_specs += [_mod_spec(0), _mod_spec(1), _const_spec((1, D_MODEL)),
                 pl.BlockSpec((D_MODEL, IN_WIDTH), lambda i: (0, 0), pipeline_mode=pl.Buffered(1)),
                 pl.BlockSpec((TM, ATT_WIDTH), lambda i: (_pos_block(i), 0)),
                 pl.BlockSpec((TM, ATT_WIDTH), lambda i: (_pos_block(i), 0))]
    args += [mod_l, mod_l, norm1.reshape(1, D_MODEL), w_in, cos_t, sin_t]
    widths = [512] * 8 + [N_BRANCH * D_MODEL]
    out_specs = [tok(w) for w in widths]
    out_shape = [jax.ShapeDtypeStruct((N_TOK, w), F32) for w in widths]
    if has_res:
        out_specs.append(tok(D_MODEL))
        out_shape.append(jax.ShapeDtypeStruct((N_TOK, D_MODEL), F32))
    return pl.pallas_call(
        functools.partial(_inproj_kernel, has_res),
        grid=(N_TILES,),
        in_specs=in_specs,
        out_specs=out_specs,
        out_shape=out_shape,
        compiler_params=pltpu.CompilerParams(vmem_limit_bytes=VMEM_LIMIT),
        name="inproj",
    )(*args)


def _lru_kernel(seq, x_ref, cw_ref, cb_ref, wr_ref, br_ref, wi_ref, bi_ref, lam_ref, h0_ref, hs_ref, hl_ref):
    x = x_ref[...]
    t = lax.broadcasted_iota(jnp.int32, (seq, 128), 0)

    def shifted(v, k, fill):
        r = pltpu.roll(v, k % seq, 0)
        ok = (t >= k) if k > 0 else (t < seq + k)
        return jnp.where(ok, r, fill)

    cw = cw_ref[...]
    xc = (shifted(x, 2, 0.0) * cw[0:1] + shifted(x, 1, 0.0) * cw[1:2] + x * cw[2:3]
          + shifted(x, -1, 0.0) * cw[3:4] + cb_ref[...])

    xc_hi, xc_lo = _split_bf16(xc)

    def gate_dot(w):
        w_hi, w_lo = _split_bf16(w)
        return (jnp.dot(xc_hi, w_hi, preferred_element_type=F32) + jnp.dot(xc_lo, w_hi, preferred_element_type=F32)
                + jnp.dot(xc_hi, w_lo, preferred_element_type=F32))

    def direction(d):
        r = _sigmoid(gate_dot(wr_ref[d, 0]) + br_ref[d:d + 1])
        g = _sigmoid(gate_dot(wi_ref[d, 0]) + bi_ref[d:d + 1])
        z = -lam_ref[d:d + 1]
        softplus = jnp.maximum(z, 0.0) + jnp.log(1.0 + jnp.exp(-jnp.abs(z)))
        log_a = -LRU_C * r * softplus
        a = jnp.exp(log_a)
        b = jnp.sqrt(1.0 - a * a) * (g * xc)
        sgn = 1 if d == 0 else -1
        local = t % 8
        for k in (1, 2, 4):
            inside = (local >= k) if d == 0 else (local < 8 - k)
            a_prev = jnp.where(inside, pltpu.roll(a, (sgn * k) % seq, 0), 1.0)
            b_prev = jnp.where(inside, pltpu.roll(b, (sgn * k) % seq, 0), 0.0)
            b = a * b_prev + b
            a = a * a_prev
        edge = 7 if d == 0 else 0
        h = h0_ref[0, d:d + 1]
        tiles = range(seq // 8) if d == 0 else reversed(range(seq // 8))
        for j in tiles:
            rows = slice(8 * j, 8 * j + 8)
            h_tile = a[rows] * h + b[rows]
            if d == 0:
                hs_ref[rows, :] = h_tile
            else:
                hs_ref[rows, :] += h_tile
            h = h_tile[edge:edge + 1]
        return h

    h_f = direction(0)
    h_b = direction(1)
    hl_ref[0] = jnp.concatenate([h_f, h_b], axis=0)


def _lru(rx, row0, nseq, seq, conv_w, conv_b, w_r, b_r, w_i, b_i, lam, h0):
    nb = LRU_WIDTH // 128
    blk0 = row0 // seq
    par = lambda shape: pl.BlockSpec(shape, lambda b, c: (0,) * (len(shape) - 1) + (c,))
    return pl.pallas_call(
        functools.partial(_lru_kernel, seq),
        grid=(nseq, nb),
        in_specs=[pl.BlockSpec((seq, 128), lambda b, c: (blk0 + b, c)),
                  par((CONV_W, 128)), par((1, 128)),
                  pl.BlockSpec((2, 1, 128, 128), lambda b, c: (0, c, 0, 0)), par((2, 128)),
                  pl.BlockSpec((2, 1, 128, 128), lambda b, c: (0, c, 0, 0)), par((2, 128)),
                  par((2, 128)),
                  pl.BlockSpec((1, 2, 128), lambda b, c: (b, 0, c))],
        out_specs=[pl.BlockSpec((seq, 128), lambda b, c: (b, c)),
                   pl.BlockSpec((1, 2, 128), lambda b, c: (b, 0, c))],
        out_shape=[jax.ShapeDtypeStruct((nseq * seq, LRU_WIDTH), F32),
                   jax.ShapeDtypeStruct((nseq, 2, LRU_WIDTH), F32)],
        compiler_params=pltpu.CompilerParams(vmem_limit_bytes=VMEM_LIMIT),
        name="rglru",
    )(rx, conv_w, conv_b.reshape(1, LRU_WIDTH), w_r, b_r, w_i, b_i, lam, h0)


def _blockdiag128(w):
    w = w.reshape(2, LRU_BLOCKS // 2, 2, LRU_BLOCK_W, LRU_BLOCK_W)
    z = jnp.zeros_like(w[:, :, 0])
    top = jnp.concatenate([w[:, :, 0], z], axis=-1)
    bot = jnp.concatenate([z, w[:, :, 1]], axis=-1)
    return jnp.concatenate([top, bot], axis=-2)


def _attn_kernel(has_ctx, lam_init, *refs):
    if has_ctx:
        q_ref, k_ref, v_ref, kc_ref, vc_ref, lp_ref, g_ref, o_ref = refs
    else:
        q_ref, k_ref, v_ref, lp_ref, g_ref, o_ref = refs
    lp = lp_ref[...]
    lam = (jnp.exp(jnp.sum(lp[0:1] * lp[1:2], axis=-1, keepdims=True))
           - jnp.exp(jnp.sum(lp[2:3] * lp[3:4], axis=-1, keepdims=True)) + lam_init)
    q = q_ref[...]
    lane = lax.broadcasted_iota(jnp.int32, q.shape, 1)
    halves = (jnp.where(lane < ATT_HD, q, 0.0), jnp.where(lane >= ATT_HD, q, 0.0))
    k = k_ref[...]
    v = v_ref[...]
    w = []
    for qh in halves:
        s = _dot_nt(qh, k)
        m = jnp.max(s, axis=-1, keepdims=True)
        if has_ctx:
            sc = _dot_nt(qh, kc_ref[0])
            m = jnp.maximum(m, jnp.max(sc, axis=-1, keepdims=True))
            ec = jnp.exp(sc - m)
        e = jnp.exp(s - m)
        den = jnp.sum(e, axis=-1, keepdims=True)
        if has_ctx:
            den = den + jnp.sum(ec, axis=-1, keepdims=True)
            w.append((e / den, ec / den))
        else:
            w.append((e / den,))
    o = _dot(w[0][0] - lam * w[1][0], v)
    if has_ctx:
        o = o + _dot(w[0][1] - lam * w[1][1], vc_ref[0])
    o_ref[...] = _rms(o, g_ref[...]) * (1.0 - lam_init)


def _attention(q, kr, v, row0, nseq, seq, ctx, att_lam, subln, lam_init):
    has_ctx = ctx is not None
    nq = seq // TM
    blk0 = row0 // seq
    in_specs = [pl.BlockSpec((TM, ATT_VD), lambda b, h, i: ((row0 // TM) + b * nq + i, h)),
                pl.BlockSpec((seq, ATT_VD), lambda b, h, i: (blk0 + b, h)),
                pl.BlockSpec((seq, ATT_VD), lambda b, h, i: (blk0 + b, h))]
    args = [q, kr, v]
    if has_ctx:
        in_specs += [pl.BlockSpec((1, PAST_LEN, ATT_VD), lambda b, h, i: (b, 0, h))] * 2
        args += list(ctx)
    in_specs += [_const_spec((4, ATT_HD)), _const_spec((1, ATT_VD))]
    args += [att_lam, subln.reshape(1, ATT_VD)]
    return pl.pallas_call(
        functools.partial(_attn_kernel, has_ctx, lam_init),
        grid=(nseq, ATT_HEADS, nq),
        in_specs=in_specs,
        out_specs=pl.BlockSpec((TM, ATT_VD), lambda b, h, i: (b * nq + i, h)),
        out_shape=jax.ShapeDtypeStruct((nseq * seq, ATT_WIDTH), F32),
        compiler_params=pltpu.CompilerParams(vmem_limit_bytes=VMEM_LIMIT),
        name="diffattn",
    )(*args)


def _merge_kernel(x_ref, au_ref, av_ref, hsc_ref, hsl_ref, rg_ref, oc_ref, ol_ref, br_ref, g1_ref, an_ref, ws_ref,
                  bs_ref, wa_ref, wb_ref, wc_ref, wo_ref, xo_ref):
    is_ctx = pl.program_id(0) < CTX_TILES
    hs = jnp.where(is_ctx, hsc_ref[...], hsl_ref[...])
    o = jnp.where(is_ctx, oc_ref[...], ol_ref[...])
    vn = _rms(av_ref[...], an_ref[...])
    rows = []
    for c in range(TM // CHUNK):
        cols = []
        for g in range(A_GROUPS):
            blk = vn[c * CHUNK:(c + 1) * CHUNK, g * 128:(g + 1) * 128]
            cols.append(_dot(ws_ref[g], blk))
        rows.append(jnp.concatenate(cols, axis=1) + bs_ref[...])
    y_a = au_ref[...] * jnp.concatenate(rows, axis=0)
    y_b = hs * jax.nn.gelu(rg_ref[...])
    merged = (_sigmoid(br_ref[:, 0:D_MODEL]) * _dot(y_a, wa_ref[...])
              + _sigmoid(br_ref[:, D_MODEL:2 * D_MODEL]) * _dot(y_b, wb_ref[...])
              + _sigmoid(br_ref[:, 2 * D_MODEL:3 * D_MODEL]) * _dot(o, wc_ref[...]))
    xo_ref[...] = x_ref[...] + g1_ref[...] * _dot(merged, wo_ref[...])


def _merge(x, au, av, hs_ctx, hs_lat, rg, o_ctx, o_lat, br, mod_l, a_norm, a_ws, bias, wa, wb, wc, wo):
    tok = lambda w: pl.BlockSpec((TM, w), lambda i: (i, 0))
    ctx = pl.BlockSpec((TM, 512), lambda i: (jnp.minimum(i, CTX_TILES - 1), 0))
    lat = pl.BlockSpec((TM, 512), lambda i: (jnp.maximum(i - CTX_TILES, 0), 0))
    return pl.pallas_call(
        _merge_kernel,
        grid=(N_TILES,),
        in_specs=[tok(D_MODEL), tok(512), tok(512), ctx, lat, tok(512), ctx, lat, tok(N_BRANCH * D_MODEL),
                  _mod_spec(2), _const_spec((1, A_WIDTH)), _const_spec((A_GROUPS, CHUNK, CHUNK)),
                  _const_spec((CHUNK, A_WIDTH)),
                  _const_spec((A_WIDTH, D_MODEL)), _const_spec((LRU_WIDTH, D_MODEL)),
                  _const_spec((ATT_WIDTH, D_MODEL)), _const_spec((D_MODEL, D_MODEL))],
        out_specs=tok(D_MODEL),
        out_shape=jax.ShapeDtypeStruct((N_TOK, D_MODEL), F32),
        compiler_params=pltpu.CompilerParams(vmem_limit_bytes=VMEM_LIMIT),
        name="merge",
    )(x, au, av, hs_ctx, hs_lat, rg, o_ctx, o_lat, br, mod_l, a_norm.reshape(1, A_WIDTH), a_ws, bias, wa, wb, wc, wo)


def _top16(s, n):
    pos = lax.broadcasted_iota(jnp.int32, s.shape, 0).astype(F32)
    vals, idxs = [], []
    for _ in range(PEER_TOPK):
        m = jnp.max(s, axis=0, keepdims=True)
        am = jnp.min(jnp.where(s == m, pos, float(n)), axis=0, keepdims=True)
        vals.append(m)
        idxs.append(am)
        s = jnp.where(pos == am, -jnp.inf, s)
    return jnp.concatenate(vals, axis=0), jnp.concatenate(idxs, axis=0)


def _take16(table, sel):
    out = jnp.zeros_like(table)
    for a in range(PEER_TOPK):
        out = jnp.where(sel == float(a), table[a:a + 1], out)
    return out


def _route_hidden(x_ref, sh_ref, sc_ref, n2_ref):
    return _rms(x_ref[...], n2_ref[...]) * (1.0 + sc_ref[...]) + sh_ref[...]


def _route_head(h2b, wq_h, keys_ref, idx_ref, gate_ref, rows):
    tn = h2b.shape[0]
    q = jnp.dot(h2b, wq_h, preferred_element_type=F32)
    v1, i1 = _top16(_dot_nt(keys_ref[0], q[:, 0:PEER_HALF]), N_KEYS)
    v2, i2 = _top16(_dot_nt(keys_ref[1], q[:, PEER_HALF:PEER_QDIM]), N_KEYS)
    cand = jnp.concatenate([v1[a:a + 1] + v2[0:n] for a, n in enumerate(CAND_COLS)]
                           + [jnp.full((CAND_PAD, tn), -jnp.inf, F32)], axis=0)
    top_s, pos = _top16(cand, N_CAND + CAND_PAD)
    a_sel = jnp.zeros_like(pos)
    b_sel = pos
    for a in range(1, PEER_TOPK):
        later = pos >= float(CAND_START[a])
        a_sel = a_sel + jnp.where(later, 1.0, 0.0)
        b_sel = b_sel - jnp.where(later, float(CAND_COLS[a - 1]), 0.0)
    expert = _take16(i1, a_sel) * N_KEYS + _take16(i2, b_sel)
    e = jnp.exp(top_s - top_s[0:1])
    gate_ref[rows, :] = e / jnp.sum(e, axis=0, keepdims=True)
    idx_ref[rows, :] = expert * float(ROW_SUB)


def _route_finish(it_ref, gt_ref, idx_ref, gate_ref):
    idx_ref[...] = it_ref[...].T.astype(jnp.int32)
    gate_ref[...] = gt_ref[...].T


def _route_kernel(x_ref, sh_ref, sc_ref, n2_ref, wq_ref, keys_ref, h2_ref, idx_ref, gate_ref, it_ref, gt_ref):
    h2 = _route_hidden(x_ref, sh_ref, sc_ref, n2_ref)
    h2_ref[...] = h2
    h2b = h2.astype(BF16)
    for h in range(PEER_HEADS):
        _route_head(h2b, wq_ref[h], keys_ref, it_ref, gt_ref, slice(h * PEER_TOPK, (h + 1) * PEER_TOPK))
    _route_finish(it_ref, gt_ref, idx_ref, gate_ref)


def _route_out_shape():
    return [jax.ShapeDtypeStruct((N_THIRD, D_MODEL), F32),
            jax.ShapeDtypeStruct((N_THIRD, N_PICK), jnp.int32),
            jax.ShapeDtypeStruct((N_THIRD, N_PICK), F32)]


def _route_scratch(tokens):
    return [pltpu.VMEM((N_PICK, tokens), F32), pltpu.VMEM((N_PICK, tokens), F32)]


def _route_first(x, mod_l, norm2, wq, keys):
    in_specs, out_specs = _route_side_specs(0)
    return pl.pallas_call(
        _route_kernel,
        grid=(N_THIRD // TP,),
        in_specs=in_specs,
        out_specs=out_specs,
        out_shape=_route_out_shape(),
        scratch_shapes=_route_scratch(TP),
        compiler_params=pltpu.CompilerParams(vmem_limit_bytes=VMEM_LIMIT),
        name="peer_route",
    )(x, mod_l, mod_l, norm2.reshape(1, D_MODEL), wq, keys)


def _route_side_specs(third):
    off = third * (N_THIRD // TP)

    def mod(chunk):
        return pl.BlockSpec((None, 1, D_MODEL), lambda i: (_cond_row((off + i) * TP // TM), 0, chunk))

    in_specs = [pl.BlockSpec((TP, D_MODEL), lambda i: (off + i, 0)), mod(3), mod(4), _const_spec((1, D_MODEL)),
                pl.BlockSpec((PEER_HEADS, D_MODEL, PEER_QDIM), lambda i: (0, 0, 0), pipeline_mode=pl.Buffered(1)),
                _const_spec((2, N_KEYS, PEER_HALF))]
    pick = pl.BlockSpec((TP, N_PICK), lambda i: (i, 0))
    out_specs = [pl.BlockSpec((TP, D_MODEL), lambda i: (i, 0)), pick, pick]
    return in_specs, out_specs


def _route_side(in_refs, out_refs, scratch):
    x_ref, sh_ref, sc_ref, n2_ref, wq_ref, keys_ref = in_refs
    h2_ref, idx_ref, gate_ref = out_refs
    h2b_ref, it_ref, gt_ref = scratch
    h2 = _route_hidden(x_ref, sh_ref, sc_ref, n2_ref)
    h2_ref[...] = h2
    h2b_ref[...] = h2.astype(BF16)

    def side(h):
        rows = pl.ds(pl.multiple_of(h * PEER_TOPK, PEER_TOPK), PEER_TOPK)
        _route_head(h2b_ref[...], wq_ref[h], keys_ref, it_ref, gt_ref, rows)

    return side, functools.partial(_route_finish, it_ref, gt_ref, idx_ref, gate_ref)


def _pack_kernel(t_ref, o_ref):
    def rounded_bits(v):
        return lax.bitcast_convert_type(v.astype(BF16).astype(F32), jnp.int32)

    lo = lax.shift_right_logical(rounded_bits(t_ref[:, 0:ROW_WORDS]), 16)
    hi = rounded_bits(t_ref[:, ROW_WORDS:D_MODEL]) & jnp.int32(-65536)
    words = lo | hi
    for g in range(PACK_ROWS // 8):
        for s in range(ROW_SUB):
            o_ref[pl.ds(8 * g * ROW_SUB + s, 8, stride=ROW_SUB), :] = words[8 * g:8 * g + 8, s * 128:(s + 1) * 128]


def _pack_table(tabs, layer):
    return pl.pallas_call(
        _pack_kernel,
        grid=(N_EXPERTS // PACK_ROWS,),
        in_specs=[pl.BlockSpec((None, PACK_ROWS, D_MODEL), lambda i: (layer, i, 0))],
        out_specs=pl.BlockSpec((PACK_ROWS * ROW_SUB, 128), lambda i: (i, 0)),
        out_shape=jax.ShapeDtypeStruct((N_EXPERTS * ROW_SUB, 128), jnp.int32),
        name="pack_table",
    )(tabs)


def _unpack(words):
    lo = lax.bitcast_convert_type(words << 16, F32)
    hi = lax.bitcast_convert_type(words & jnp.int32(-65536), F32)
    return lo, hi


def _gather_planes(tab_ref, idx_ref, t, g_ref):
    for k in range(N_PICK):
        row = pl.multiple_of(idx_ref[t, k], ROW_SUB)
        g_ref[pl.ds(k, ROW_SUB, stride=PLANE_STRIDE), :] = tab_ref[pl.ds(row, ROW_SUB), :]


def _plane(g_ref, s):
    return g_ref[s * PLANE_STRIDE:s * PLANE_STRIDE + N_PICK, :]


def _split_bf16(v):
    hi = v.astype(BF16)
    return hi, (v - hi.astype(F32)).astype(BF16)


def _pipelined_tokens(tab_ref, idx_ref, bufs, compute, side):
    _gather_planes(tab_ref, idx_ref, 0, bufs[0])

    def group(j, carry):
        if side is not None:
            side(j)
        for p in range(TOKEN_UNROLL):
            t = TOKEN_UNROLL * j + p
            compute(t, pl.multiple_of(TOKEN_UNROLL * j + p // 8 * 8, 8), p % 8, bufs[p % 2])
            _gather_planes(tab_ref, idx_ref, jnp.minimum(t + 1, TP - 1), bufs[(p + 1) % 2])
        return carry

    lax.fori_loop(0, TP // TOKEN_UNROLL, group, 0)


def _split_refs(with_route, refs, n_in, n_scratch):
    n_rin, n_rout = (6, 3) if with_route else (0, 0)
    cuts = [n_in, n_rin, 1, n_rout, n_scratch]
    parts, at = [], 0
    for n in cuts:
        parts.append(refs[at:at + n])
        at += n
    parts.append(refs[at:])
    return parts


def _peer_act_kernel(with_route, idx_ref, x_ref, gate_ref, tab_ref, o_ref, ga_ref, gb_ref):
    assert not with_route
    ones = jnp.ones((8, 2 * 128), BF16)

    def token(t, t8, r, g_ref):
        def x_row(j):
            return x_ref[pl.ds(t8, 8), j * 128:(j + 1) * 128][r:r + 1]

        acc = jnp.zeros((N_PICK, 128), F32)
        for s in range(ROW_SUB):
            lo, hi = _unpack(_plane(g_ref, s))
            acc = acc + lo * x_row(s) + hi * x_row(ROW_SUB + s)
        act = _dot_nt(ones, jnp.concatenate(_split_bf16(acc), axis=1))
        o_ref[pl.ds(t, 1), :] = jax.nn.gelu(act[0:1]) * gate_ref[pl.ds(t, 1), :]

    _pipelined_tokens(tab_ref, idx_ref, (ga_ref, gb_ref), token, None)


def _peer_out_kernel(with_route, *refs):
    (idx_ref, coef_ref, tab_ref), rin, (o_ref,), rout, (ga_ref, gb_ref, st_ref), rscratch = _split_refs(
        with_route, refs, 3, 3)
    side, finish = _route_side(rin, rout, rscratch) if with_route else (None, None)

    def token(t, t8, r, g_ref):
        c_hi, c_lo = _split_bf16(coef_ref[t])
        lhs = jnp.concatenate([c_hi, c_lo, jnp.zeros((4, 2 * N_PICK), BF16)], axis=0)
        for s in range(ROW_SUB):
            w = pltpu.bitcast(_plane(g_ref, s), BF16)
            acc = jnp.dot(lhs, w, preferred_element_type=F32)
            st_ref[8 * s + r:8 * s + r + 1, :] = acc[0:1] + acc[2:3]
            st_ref[8 * (ROW_SUB + s) + r:8 * (ROW_SUB + s) + r + 1, :] = acc[1:2] + acc[3:4]
        if r == 7:
            for j in range(D_MODEL // 128):
                o_ref[pl.ds(t8, 8), j * 128:(j + 1) * 128] = st_ref[8 * j:8 * j + 8, :]

    _pipelined_tokens(tab_ref, idx_ref, (ga_ref, gb_ref), token, side)
    if with_route:
        finish()


def _expert_call(body, name, in_specs, args, out_spec, out_width, scratch, side):
    out_specs, out_shape = [out_spec], [jax.ShapeDtypeStruct((N_THIRD, out_width), F32)]
    if side is not None:
        assert TP // TOKEN_UNROLL == PEER_HEADS
        third, x, mod_l, norm2, wq, keys = side
        rin, rout = _route_side_specs(third)
        in_specs = in_specs + rin
        args = args + (x, mod_l, mod_l, norm2.reshape(1, D_MODEL), wq, keys)
        out_specs = out_specs + rout
        out_shape = out_shape + _route_out_shape()
        scratch = scratch + [pltpu.VMEM((TP, D_MODEL), BF16)] + _route_scratch(TP)
    outs = pl.pallas_call(
        functools.partial(body, side is not None),
        grid=(N_THIRD // TP,),
        in_specs=in_specs,
        out_specs=out_specs,
        out_shape=out_shape,
        scratch_shapes=scratch,
        compiler_params=pltpu.CompilerParams(vmem_limit_bytes=VMEM_LIMIT),
        name=name,
    )(*args)
    return outs[0], tuple(outs[1:])


_SMEM_IDX = pl.BlockSpec((TP, N_PICK), lambda i: (i, 0), memory_space=pltpu.SMEM)
_TABLE = pl.BlockSpec((N_EXPERTS * ROW_SUB, 128), lambda i: (0, 0), pipeline_mode=pl.Buffered(1))
_PLANES = pltpu.VMEM((ROW_SUB * PLANE_STRIDE, 128), jnp.int32)


def _peer_act(idx, h2, gate, u_words):
    rows = pl.BlockSpec((TP, D_MODEL), lambda i: (i, 0))
    pick = pl.BlockSpec((TP, N_PICK), lambda i: (i, 0))
    coef, _ = _expert_call(_peer_act_kernel, "peer_act", [_SMEM_IDX, rows, pick, _TABLE],
                           (idx, h2, gate, u_words), pick, N_PICK, [_PLANES, _PLANES], None)
    zero = jnp.zeros_like(coef)
    coef = jnp.stack([jnp.stack([coef, zero], axis=-1), jnp.stack([zero, coef], axis=-1)], axis=1)
    return coef.reshape(N_THIRD, 2, 2 * N_PICK)


def _peer_out(idx, coef, v_words, side=None):
    rows = pl.BlockSpec((TP, D_MODEL), lambda i: (i, 0))
    pair = pl.BlockSpec((TP, 2, 2 * N_PICK), lambda i: (i, 0, 0))
    stage = pltpu.VMEM((8 * (D_MODEL // 128), 128), F32)
    return _expert_call(_peer_out_kernel, "peer_out", [_SMEM_IDX, pair, _TABLE], (idx, coef, v_words),
                        rows, D_MODEL, [_PLANES, _PLANES, stage], side)


def _peer(x, mod_l, norm2, wq, keys, u_words, v_words):
    route_args = (x, mod_l, norm2, wq, keys)
    h2, idx, gate = _route_first(*route_args)
    outs = []
    for third in range(3):
        coef = _peer_act(idx, h2, gate, u_words)
        side = (third + 1,) + route_args if third < 2 else None
        p, routed = _peer_out(idx, coef, v_words, side)
        outs.append(p)
        if side is not None:
            h2, idx, gate = routed
    return tuple(outs)


def _final_kernel(x_ref, pa_ref, pb_ref, pc_ref, g2_ref, n_ref, o_ref):
    o_ref[...] = _rms(x_ref[...] + g2_ref[...] * _pick_third((pa_ref, pb_ref, pc_ref)), n_ref[...])


def _final(x, p, mod_l, final_norm):
    tok = pl.BlockSpec((TM, D_MODEL), lambda i: (i, 0))
    return pl.pallas_call(
        _final_kernel,
        grid=(N_TILES,),
        in_specs=[tok] + _third_specs(D_MODEL) + [_mod_spec(5), _const_spec((1, D_MODEL))],
        out_specs=tok,
        out_shape=jax.ShapeDtypeStruct((N_TOK, D_MODEL), F32),
        name="final_norm",
    )(x, *p, mod_l, final_norm.reshape(1, D_MODEL))


def _rope_tables():
    rows = DEC_SEQ // GRID_W
    row_ids = jnp.repeat(jnp.arange(rows), GRID_W).astype(F32)
    col_ids = jnp.tile(jnp.arange(GRID_W), rows).astype(F32)
    inv_freq = ROPE_BASE ** (-jnp.arange(ROPE_FREQS, dtype=F32) / ROPE_FREQS)
    ang_r = row_ids[:, None] * inv_freq
    ang_c = col_ids[:, None] * inv_freq
    cos = jnp.concatenate([jnp.cos(ang_r), jnp.cos(ang_r), jnp.cos(ang_c), jnp.cos(ang_c)], axis=1)
    sin = jnp.concatenate([-jnp.sin(ang_r), jnp.sin(ang_r), -jnp.sin(ang_c), jnp.sin(ang_c)], axis=1)
    reps = ATT_WIDTH // ATT_HD
    cos = jnp.concatenate([jnp.ones((TM, ATT_WIDTH), F32), jnp.tile(cos, (1, reps))], axis=0)
    sin = jnp.concatenate([jnp.zeros((TM, ATT_WIDTH), F32), jnp.tile(sin, (1, reps))], axis=0)
    return cos, sin


def kernel(x_prompt, x_sample, cache_k, cache_v, state_lru, c, c_ctx, w_mod, b_mod, norm1, norm2, w_in, a_norm, a_ws, a_bs, lru_conv_w, lru_conv_b, lru_w_r, lru_b_r, lru_w_i, lru_b_i, lru_lam, att_lam, att_subln, w_up_a, w_up_b, w_up_c, w_out, peer_wq, peer_keys, peer_u, peer_v, final_norm):
    x = jnp.concatenate([x_prompt.reshape(N_CTX_TOK, D_MODEL), x_sample.reshape(N_LAT_TOK, D_MODEL)], axis=0)
    cond = jnp.concatenate([c_ctx[None, :], c, jnp.zeros((N_COND - 1 - DEC_BATCH, D_MODEL), F32)], axis=0)
    mod = _modulation(cond, w_mod, b_mod).reshape(DEPTH, N_COND, 1, N_MOD * D_MODEL)
    cos_t, sin_t = _rope_tables()
    zero_h0 = jnp.zeros((BATCH, 2, LRU_WIDTH), F32)
    ks, vs, hs = [], [], []
    res = None
    for i in range(DEPTH):
        lam_init = 0.8 - 0.6 * math.exp(-0.3 * i)
        outs = _inproj(x, res, mod[i], norm1[i], w_in[i].astype(BF16), cos_t, sin_t)
        au, av, rx, rg, q, k, kr, v, br = outs[:9]
        if res is not None:
            x = outs[9]
        w_r = _blockdiag128(lru_w_r[i])
        w_i = _blockdiag128(lru_w_i[i])
        lru_args = (lru_conv_w[i], lru_conv_b[i], w_r, lru_b_r[i], w_i, lru_b_i[i], lru_lam[i])
        hs_ctx, hl_ctx = _lru(rx, 0, BATCH, SEQ, *lru_args, zero_h0)
        hs_lat, _ = _lru(rx, N_CTX_TOK, DEC_BATCH, DEC_SEQ, *lru_args, state_lru[:, i])
        o_ctx = _attention(q, kr, v, 0, BATCH, SEQ, None, att_lam[i], att_subln[i], lam_init)
        ctx = (cache_k[:, i].reshape(DEC_BATCH, PAST_LEN, ATT_WIDTH), cache_v[:, i].reshape(DEC_BATCH, PAST_LEN, ATT_WIDTH))
        o_lat = _attention(q, kr, v, N_CTX_TOK, DEC_BATCH, DEC_SEQ, ctx, att_lam[i], att_subln[i], lam_init)
        bias = jnp.repeat(a_bs[i].T, CHUNK, axis=1)
        x = _merge(x, au, av, hs_ctx, hs_lat, rg, o_ctx, o_lat, br, mod[i], a_norm[i], a_ws[i], bias, w_up_a[i].astype(BF16), w_up_b[i].astype(BF16),
                   w_up_c[i].astype(BF16), w_out[i].astype(BF16))
        wq = peer_wq[i].astype(BF16).reshape(D_MODEL, PEER_HEADS, PEER_QDIM).transpose(1, 0, 2)
        p = _peer(x, mod[i], norm2[i], wq, peer_keys[i], _pack_table(peer_u, i), _pack_table(peer_v, i))
        res = (p, mod[i])
        ks.append(k[:N_CTX_TOK].reshape(BATCH, SEQ, ATT_HEADS, ATT_VD))
        vs.append(v[:N_CTX_TOK].reshape(BATCH, SEQ, ATT_HEADS, ATT_VD))
        hs.append(hl_ctx)
    y = _final(x, res[0], res[1], final_norm)
    return (y[:N_CTX_TOK].reshape(BATCH, SEQ, D_MODEL), y[N_CTX_TOK:].reshape(DEC_BATCH, DEC_SEQ, D_MODEL),
            jnp.stack(ks, axis=1), jnp.stack(vs, axis=1), jnp.stack(hs, axis=1))
```

```python
import functools
import math

import jax
import jax.numpy as jnp
from jax import lax
from jax.experimental import pallas as pl
from jax.experimental.pallas import tpu as pltpu

D_MODEL = 1024
BATCH = 16
SEQ = 256
DEPTH = 2
DEC_BATCH = 4
DEC_SEQ = 2048
PAST_LEN = 512
GRID_W = 64
EPS = 1e-6
N_MOD = 6
CHUNK = 128
A_GROUPS = 4
A_WIDTH = 512
LRU_BLOCKS = 8
LRU_BLOCK_W = 64
LRU_WIDTH = 512
CONV_W = 4
LRU_C = 8.0
ATT_HEADS = 4
ATT_HD = 64
ATT_VD = 128
ATT_WIDTH = 512
ATT_KEY_CHUNK = 512
ROPE_BASE = 10000.0
ROPE_FREQS = 16
N_BRANCH = 3
IN_SPLITS = (512, 1024, 1536, 2048, 2560, 3072, 3584)
IN_WIDTH = 3584 + N_BRANCH * D_MODEL
PEER_HEADS = 8
N_KEYS = 128
N_EXPERTS = N_KEYS * N_KEYS
PEER_QDIM = 256
PEER_HALF = 128
PEER_TOPK = 16
N_PICK = PEER_HEADS * PEER_TOPK
CAND_COLS = tuple(PEER_TOPK // (a + 1) for a in range(PEER_TOPK))
CAND_START = tuple(sum(CAND_COLS[:a]) for a in range(PEER_TOPK))
N_CAND = sum(CAND_COLS)
CAND_PAD = -N_CAND % 8

N_CTX_TOK = BATCH * SEQ
N_LAT_TOK = DEC_BATCH * DEC_SEQ
N_TOK = N_CTX_TOK + N_LAT_TOK
TM = 256
N_TILES = N_TOK // TM
N_THIRD = N_TOK // 3
THIRD_TILES = N_THIRD // TM
CTX_TILES = N_CTX_TOK // TM
LAT_TILES_PER_SEQ = DEC_SEQ // TM
N_COND = 8
TP = 128
TOKEN_UNROLL = 16
PACK_ROWS = 1024
ROW_WORDS = D_MODEL // 2
ROW_SUB = ROW_WORDS // 128
PLANE_STRIDE = N_PICK + 8
VMEM_LIMIT = 56 * 1024 * 1024

F32 = jnp.float32
BF16 = jnp.bfloat16
HI = lax.Precision.HIGHEST


def _cond_row(i):
    return jnp.maximum(i - LAT_TILES_PER_SEQ, 0) // LAT_TILES_PER_SEQ


def _pos_block(i):
    return jnp.where(i < CTX_TILES, 0, 1 + i % LAT_TILES_PER_SEQ)


def _mod_spec(chunk):
    return pl.BlockSpec((None, 1, D_MODEL), lambda i: (_cond_row(i), 0, chunk))


def _third_specs(width):
    return [pl.BlockSpec((TM, width), lambda i, k=k: (jnp.clip(i - k * THIRD_TILES, 0, THIRD_TILES - 1), 0))
            for k in range(3)]


def _pick_third(refs):
    i = pl.program_id(0)
    return jnp.where(i < THIRD_TILES, refs[0][...], jnp.where(i < 2 * THIRD_TILES, refs[1][...], refs[2][...]))


def _const_spec(shape):
    nd = len(shape)
    return pl.BlockSpec(shape, lambda *_: (0,) * nd)


def _rms(x, gain):
    return x * lax.rsqrt(jnp.mean(x * x, axis=-1, keepdims=True) + EPS) * gain


def _sigmoid(z):
    return 0.5 * jnp.tanh(0.5 * z) + 0.5


def _dot(a, b):
    return jnp.dot(a.astype(BF16), b.astype(BF16), preferred_element_type=F32)


def _dot_nt(a, b):
    return lax.dot_general(a.astype(BF16), b.astype(BF16), (((1,), (1,)), ((), ())), preferred_element_type=F32)


def _mod_kernel(cond_ref, w_ref, b_ref, o_ref):
    cond = cond_ref[...]
    act = cond * jax.nn.sigmoid(cond)
    o_ref[...] = jnp.dot(act, w_ref[...], preferred_element_type=F32, precision=HI) + b_ref[...]


def _modulation(cond, w_mod, b_mod):
    nc = 4
    cw = N_MOD * D_MODEL // nc
    return pl.pallas_call(
        _mod_kernel,
        grid=(DEPTH, nc),
        in_specs=[pl.BlockSpec((N_COND, D_MODEL), lambda l, j: (0, 0)),
                  pl.BlockSpec((None, D_MODEL, cw), lambda l, j: (l, 0, j)),
                  pl.BlockSpec((None, 1, cw), lambda l, j: (l, 0, j))],
        out_specs=pl.BlockSpec((None, N_COND, cw), lambda l, j: (l, 0, j)),
        out_shape=jax.ShapeDtypeStruct((DEPTH, N_COND, N_MOD * D_MODEL), F32),
        compiler_params=pltpu.CompilerParams(vmem_limit_bytes=VMEM_LIMIT),
        name="modulation",
    )(cond, w_mod, b_mod.reshape(DEPTH, 1, N_MOD * D_MODEL))


def _inproj_kernel(has_res, *refs):
    if has_res:
        x_ref, g2_ref = refs[0], refs[4]
        x = x_ref[...] + g2_ref[...] * _pick_third(refs[1:4])
        refs = refs[5:]
    else:
        x_ref = refs[0]
        refs = refs[1:]
        x = x_ref[...]
    (sh_ref, sc_ref, n1_ref, w_ref, cos_ref, sin_ref,
     au_ref, av_ref, rx_ref, rg_ref, q_ref, k_ref, kr_ref, v_ref, br_ref) = refs[:15]
    if has_res:
        refs[15][...] = x
    h = (_rms(x, n1_ref[...]) * (1.0 + sc_ref[...]) + sh_ref[...]).astype(BF16)

    def proj(lo, hi):
        return jnp.dot(h, w_ref[:, lo:hi], preferred_element_type=F32)

    au_ref[...] = proj(0, IN_SPLITS[0])
    av_ref[...] = proj(IN_SPLITS[0], IN_SPLITS[1])
    rx_ref[...] = proj(IN_SPLITS[1], IN_SPLITS[2])
    rg_ref[...] = proj(IN_SPLITS[2], IN_SPLITS[3])
    q = proj(IN_SPLITS[3], IN_SPLITS[4])
    k = proj(IN_SPLITS[4], IN_SPLITS[5])
    v_ref[...] = proj(IN_SPLITS[5], IN_SPLITS[6])
    for j in range(N_BRANCH):
        lo = IN_SPLITS[6] + j * D_MODEL
        br_ref[:, j * D_MODEL:(j + 1) * D_MODEL] = proj(lo, lo + D_MODEL)
    k_ref[...] = k
    lane = lax.broadcasted_iota(jnp.int32, (TM, ATT_WIDTH), 1)
    first = (lane % (2 * ROPE_FREQS)) < ROPE_FREQS
    cos = cos_ref[...]
    sin = sin_ref[...]

    def rot(t):
        partner = jnp.where(first, pltpu.roll(t, ATT_WIDTH - ROPE_FREQS, 1), pltpu.roll(t, ROPE_FREQS, 1))
        return t * cos + partner * sin

    q_ref[...] = rot(q) * (ATT_HD ** -0.5)
    kr_ref[...] = rot(k)


def _inproj(x, res, mod_l, norm1, w_in, cos_t, sin_t):
    has_res = res is not None
    tok = lambda w: pl.BlockSpec((TM, w), lambda i: (i, 0))
    in_specs = [tok(D_MODEL)]
    args = [x]
    if has_res:
        p, mod_prev = res
        in_specs += _third_specs(D_MODEL) + [_mod_spec(5)]
        args += [*p, mod_prev]
    in_specs += [_mod_spec(0), _mod_spec(1), _const_spec((1, D_MODEL)),
                 pl.BlockSpec((D_MODEL, IN_WIDTH), lambda i: (0, 0), pipeline_mode=pl.Buffered(1)),
                 pl.BlockSpec((TM, ATT_WIDTH), lambda i: (_pos_block(i), 0)),
                 pl.BlockSpec((TM, ATT_WIDTH), lambda i: (_pos_block(i), 0))]
    args += [mod_l, mod_l, norm1.reshape(1, D_MODEL), w_in, cos_t, sin_t]
    widths = [512] * 8 + [N_BRANCH * D_MODEL]
    out_specs = [tok(w) for w in widths]
    out_shape = [jax.ShapeDtypeStruct((N_TOK, w), F32) for w in widths]
    if has_res:
        out_specs.append(tok(D_MODEL))
        out_shape.append(jax.ShapeDtypeStruct((N_TOK, D_MODEL), F32))
    return pl.pallas_call(
        functools.partial(_inproj_kernel, has_res),
        grid=(N_TILES,),
        in_specs=in_specs,
        out_specs=out_specs,
        out_shape=out_shape,
        compiler_params=pltpu.CompilerParams(vmem_limit_bytes=VMEM_LIMIT),
        name="inproj",
    )(*args)


def _lru_kernel(seq, x_ref, cw_ref, cb_ref, wr_ref, br_ref, wi_ref, bi_ref, lam_ref, h0_ref, hs_ref, hl_ref):
    x = x_ref[...]
    t = lax.broadcasted_iota(jnp.int32, (seq, 128), 0)

    def shifted(v, k, fill):
        r = pltpu.roll(v, k % seq, 0)
        ok = (t >= k) if k > 0 else (t < seq + k)
        return jnp.where(ok, r, fill)

    cw = cw_ref[...]
    xc = (shifted(x, 2, 0.0) * cw[0:1] + shifted(x, 1, 0.0) * cw[1:2] + x * cw[2:3]
          + shifted(x, -1, 0.0) * cw[3:4] + cb_ref[...])

    xc_hi, xc_lo = _split_bf16(xc)

    def gate_dot(w):
        w_hi, w_lo = _split_bf16(w)
        return (jnp.dot(xc_hi, w_hi, preferred_element_type=F32) + jnp.dot(xc_lo, w_hi, preferred_element_type=F32)
                + jnp.dot(xc_hi, w_lo, preferred_element_type=F32))

    def direction(d):
        r = _sigmoid(gate_dot(wr_ref[d, 0]) + br_ref[d:d + 1])
        g = _sigmoid(gate_dot(wi_ref[d, 0]) + bi_ref[d:d + 1])
        z = -lam_ref[d:d + 1]
        softplus = jnp.maximum(z, 0.0) + jnp.log(1.0 + jnp.exp(-jnp.abs(z)))
        log_a = -LRU_C * r * softplus
        a = jnp.exp(log_a)
        b = jnp.sqrt(1.0 - a * a) * (g * xc)
        sgn = 1 if d == 0 else -1
        local = t % 8
        for k in (1, 2, 4):
            inside = (local >= k) if d == 0 else (local < 8 - k)
            a_prev = jnp.where(inside, pltpu.roll(a, (sgn * k) % seq, 0), 1.0)
            b_prev = jnp.where(inside, pltpu.roll(b, (sgn * k) % seq, 0), 0.0)
            b = a * b_prev + b
            a = a * a_prev
        edge = 7 if d == 0 else 0
        h = h0_ref[0, d:d + 1]
        tiles = range(seq // 8) if d == 0 else reversed(range(seq // 8))
        for j in tiles:
            rows = slice(8 * j, 8 * j + 8)
            h_tile = a[rows] * h + b[rows]
            if d == 0:
                hs_ref[rows, :] = h_tile
            else:
                hs_ref[rows, :] += h_tile
            h = h_tile[edge:edge + 1]
        return h

    h_f = direction(0)
    h_b = direction(1)
    hl_ref[0] = jnp.concatenate([h_f, h_b], axis=0)


def _lru(rx, row0, nseq, seq, conv_w, conv_b, w_r, b_r, w_i, b_i, lam, h0):
    nb = LRU_WIDTH // 128
    blk0 = row0 // seq
    par = lambda shape: pl.BlockSpec(shape, lambda b, c: (0,) * (len(shape) - 1) + (c,))
    return pl.pallas_call(
        functools.partial(_lru_kernel, seq),
        grid=(nseq, nb),
        in_specs=[pl.BlockSpec((seq, 128), lambda b, c: (blk0 + b, c)),
                  par((CONV_W, 128)), par((1, 128)),
                  pl.BlockSpec((2, 1, 128, 128), lambda b, c: (0, c, 0, 0)), par((2, 128)),
                  pl.BlockSpec((2, 1, 128, 128), lambda b, c: (0, c, 0, 0)), par((2, 128)),
                  par((2, 128)),
                  pl.BlockSpec((1, 2, 128), lambda b, c: (b, 0, c))],
        out_specs=[pl.BlockSpec((seq, 128), lambda b, c: (b, c)),
                   pl.BlockSpec((1, 2, 128), lambda b, c: (b, 0, c))],
        out_shape=[jax.ShapeDtypeStruct((nseq * seq, LRU_WIDTH), F32),
                   jax.ShapeDtypeStruct((nseq, 2, LRU_WIDTH), F32)],
        compiler_params=pltpu.CompilerParams(vmem_limit_bytes=VMEM_LIMIT),
        name="rglru",
    )(rx, conv_w, conv_b.reshape(1, LRU_WIDTH), w_r, b_r, w_i, b_i, lam, h0)


def _blockdiag128(w):
    w = w.reshape(2, LRU_BLOCKS // 2, 2, LRU_BLOCK_W, LRU_BLOCK_W)
    z = jnp.zeros_like(w[:, :, 0])
    top = jnp.concatenate([w[:, :, 0], z], axis=-1)
    bot = jnp.concatenate([z, w[:, :, 1]], axis=-1)
    return jnp.concatenate([top, bot], axis=-2)


def _attn_kernel(has_ctx, lam_init, *refs):
    if has_ctx:
        q_ref, k_ref, v_ref, kc_ref, vc_ref, lp_ref, g_ref, o_ref = refs
    else:
        q_ref, k_ref, v_ref, lp_ref, g_ref, o_ref = refs
    lp = lp_ref[...]
    lam = (jnp.exp(jnp.sum(lp[0:1] * lp[1:2], axis=-1, keepdims=True))
           - jnp.exp(jnp.sum(lp[2:3] * lp[3:4], axis=-1, keepdims=True)) + lam_init)
    q = q_ref[...]
    lane = lax.broadcasted_iota(jnp.int32, q.shape, 1)
    halves = (jnp.where(lane < ATT_HD, q, 0.0), jnp.where(lane >= ATT_HD, q, 0.0))
    k = k_ref[...]
    v = v_ref[...]
    chunk = min(ATT_KEY_CHUNK, k.shape[0])
    parts = []
    for qh in halves:
        pieces = [_dot_nt(qh, k)] + ([_dot_nt(qh, kc_ref[0])] if has_ctx else [])
        m = functools.reduce(jnp.maximum, [jnp.max(s, axis=-1, keepdims=True) for s in pieces])
        den = sum(jnp.sum(jnp.exp(s[:, c:c + chunk] - m), axis=-1, keepdims=True)
                  for s in pieces for c in range(0, s.shape[1], chunk))
        parts.append((pieces, m + jnp.log(den)))
    values = [v] + ([vc_ref[0]] if has_ctx else [])
    o = jnp.zeros((q.shape[0], ATT_VD), F32)
    for j, vals in enumerate(values):
        for c in range(0, vals.shape[0], chunk):
            w = (jnp.exp(parts[0][0][j][:, c:c + chunk] - parts[0][1])
                 - lam * jnp.exp(parts[1][0][j][:, c:c + chunk] - parts[1][1]))
            o = o + _dot(w, vals[c:c + chunk])
    o_ref[...] = _rms(o, g_ref[...]) * (1.0 - lam_init)


def _attention(q, kr, v, row0, nseq, seq, ctx, att_lam, subln, lam_init):
    has_ctx = ctx is not None
    nq = seq // TM
    blk0 = row0 // seq
    in_specs = [pl.BlockSpec((TM, ATT_VD), lambda b, h, i: ((row0 // TM) + b * nq + i, h)),
                pl.BlockSpec((seq, ATT_VD), lambda b, h, i: (blk0 + b, h)),
                pl.BlockSpec((seq, ATT_VD), lambda b, h, i: (blk0 + b, h))]
    args = [q, kr, v]
    if has_ctx:
        in_specs += [pl.BlockSpec((1, PAST_LEN, ATT_VD), lambda b, h, i: (b, 0, h))] * 2
        args += list(ctx)
    in_specs += [_const_spec((4, ATT_HD)), _const_spec((1, ATT_VD))]
    args += [att_lam, subln.reshape(1, ATT_VD)]
    return pl.pallas_call(
        functools.partial(_attn_kernel, has_ctx, lam_init),
        grid=(nseq, ATT_HEADS, nq),
        in_specs=in_specs,
        out_specs=pl.BlockSpec((TM, ATT_VD), lambda b, h, i: (b * nq + i, h)),
        out_shape=jax.ShapeDtypeStruct((nseq * seq, ATT_WIDTH), F32),
        compiler_params=pltpu.CompilerParams(vmem_limit_bytes=VMEM_LIMIT),
        name="diffattn",
    )(*args)


def _merge_kernel(x_ref, au_ref, av_ref, hsc_ref, hsl_ref, rg_ref, oc_ref, ol_ref, br_ref, g1_ref, an_ref, ws_ref,
                  bs_ref, wa_ref, wb_ref, wc_ref, wo_ref, xo_ref):
    is_ctx = pl.program_id(0) < CTX_TILES
    hs = jnp.where(is_ctx, hsc_ref[...], hsl_ref[...])
    o = jnp.where(is_ctx, oc_ref[...], ol_ref[...])
    vn = _rms(av_ref[...], an_ref[...])
    rows = []
    for c in range(TM // CHUNK):
        cols = []
        for g in range(A_GROUPS):
            blk = vn[c * CHUNK:(c + 1) * CHUNK, g * 128:(g + 1) * 128]
            cols.append(_dot(ws_ref[g], blk))
        rows.append(jnp.concatenate(cols, axis=1) + bs_ref[...])
    y_a = au_ref[...] * jnp.concatenate(rows, axis=0)
    y_b = hs * jax.nn.gelu(rg_ref[...])
    merged = (_sigmoid(br_ref[:, 0:D_MODEL]) * _dot(y_a, wa_ref[...])
              + _sigmoid(br_ref[:, D_MODEL:2 * D_MODEL]) * _dot(y_b, wb_ref[...])
              + _sigmoid(br_ref[:, 2 * D_MODEL:3 * D_MODEL]) * _dot(o, wc_ref[...]))
    xo_ref[...] = x_ref[...] + g1_ref[...] * _dot(merged, wo_ref[...])


def _merge(x, au, av, hs_ctx, hs_lat, rg, o_ctx, o_lat, br, mod_l, a_norm, a_ws, bias, wa, wb, wc, wo):
    tok = lambda w: pl.BlockSpec((TM, w), lambda i: (i, 0))
    ctx = pl.BlockSpec((TM, 512), lambda i: (jnp.minimum(i, CTX_TILES - 1), 0))
    lat = pl.BlockSpec((TM, 512), lambda i: (jnp.maximum(i - CTX_TILES, 0), 0))
    return pl.pallas_call(
        _merge_kernel,
        grid=(N_TILES,),
        in_specs=[tok(D_MODEL), tok(512), tok(512), ctx, lat, tok(512), ctx, lat, tok(N_BRANCH * D_MODEL),
                  _mod_spec(2), _const_spec((1, A_WIDTH)), _const_spec((A_GROUPS, CHUNK, CHUNK)),
                  _const_spec((CHUNK, A_WIDTH)),
                  _const_spec((A_WIDTH, D_MODEL)), _const_spec((LRU_WIDTH, D_MODEL)),
                  _const_spec((ATT_WIDTH, D_MODEL)), _const_spec((D_MODEL, D_MODEL))],
        out_specs=tok(D_MODEL),
        out_shape=jax.ShapeDtypeStruct((N_TOK, D_MODEL), F32),
        compiler_params=pltpu.CompilerParams(vmem_limit_bytes=VMEM_LIMIT),
        name="merge",
    )(x, au, av, hs_ctx, hs_lat, rg, o_ctx, o_lat, br, mod_l, a_norm.reshape(1, A_WIDTH), a_ws, bias, wa, wb, wc, wo)


def _top16(s, n):
    pos = lax.broadcasted_iota(jnp.int32, s.shape, 0).astype(F32)
    vals, idxs = [], []
    for _ in range(PEER_TOPK):
        m = jnp.max(s, axis=0, keepdims=True)
        am = jnp.min(jnp.where(s == m, pos, float(n)), axis=0, keepdims=True)
        vals.append(m)
        idxs.append(am)
        s = jnp.where(pos == am, -jnp.inf, s)
    return jnp.concatenate(vals, axis=0), jnp.concatenate(idxs, axis=0)


def _take16(table, sel):
    out = jnp.zeros_like(table)
    for a in range(PEER_TOPK):
        out = jnp.where(sel == float(a), table[a:a + 1], out)
    return out


def _route_hidden(x_ref, sh_ref, sc_ref, n2_ref):
    return _rms(x_ref[...], n2_ref[...]) * (1.0 + sc_ref[...]) + sh_ref[...]


def _route_head(h2b, wq_h, keys_ref, idx_ref, gate_ref, rows):
    tn = h2b.shape[0]
    q = jnp.dot(h2b, wq_h, preferred_element_type=F32)
    v1, i1 = _top16(_dot_nt(keys_ref[0], q[:, 0:PEER_HALF]), N_KEYS)
    v2, i2 = _top16(_dot_nt(keys_ref[1], q[:, PEER_HALF:PEER_QDIM]), N_KEYS)
    cand = jnp.concatenate([v1[a:a + 1] + v2[0:n] for a, n in enumerate(CAND_COLS)]
                           + [jnp.full((CAND_PAD, tn), -jnp.inf, F32)], axis=0)
    top_s, pos = _top16(cand, N_CAND + CAND_PAD)
    a_sel = jnp.zeros_like(pos)
    b_sel = pos
    for a in range(1, PEER_TOPK):
        later = pos >= float(CAND_START[a])
        a_sel = a_sel + jnp.where(later, 1.0, 0.0)
        b_sel = b_sel - jnp.where(later, float(CAND_COLS[a - 1]), 0.0)
    expert = _take16(i1, a_sel) * N_KEYS + _take16(i2, b_sel)
    e = jnp.exp(top_s - top_s[0:1])
    gate_ref[rows, :] = e / jnp.sum(e, axis=0, keepdims=True)
    idx_ref[rows, :] = expert * float(ROW_SUB)


def _route_finish(it_ref, gt_ref, idx_ref, gate_ref):
    idx_ref[...] = it_ref[...].T.astype(jnp.int32)
    gate_ref[...] = gt_ref[...].T


def _route_kernel(x_ref, sh_ref, sc_ref, n2_ref, wq_ref, keys_ref, h2_ref, idx_ref, gate_ref, it_ref, gt_ref):
    h2 = _route_hidden(x_ref, sh_ref, sc_ref, n2_ref)
    h2_ref[...] = h2
    h2b = h2.astype(BF16)
    for h in range(PEER_HEADS):
        _route_head(h2b, wq_ref[h], keys_ref, it_ref, gt_ref, slice(h * PEER_TOPK, (h + 1) * PEER_TOPK))
    _route_finish(it_ref, gt_ref, idx_ref, gate_ref)


def _route_out_shape():
    return [jax.ShapeDtypeStruct((N_THIRD, D_MODEL), F32),
            jax.ShapeDtypeStruct((N_THIRD, N_PICK), jnp.int32),
            jax.ShapeDtypeStruct((N_THIRD, N_PICK), F32)]


def _route_scratch(tokens):
    return [pltpu.VMEM((N_PICK, tokens), F32), pltpu.VMEM((N_PICK, tokens), F32)]


def _route_first(x, mod_l, norm2, wq, keys):
    tok = pl.BlockSpec((TM, D_MODEL), lambda i: (i, 0))
    pick = pl.BlockSpec((TM, N_PICK), lambda i: (i, 0))
    return pl.pallas_call(
        _route_kernel,
        grid=(N_THIRD // TM,),
        in_specs=[tok, _mod_spec(3), _mod_spec(4), _const_spec((1, D_MODEL)),
                  _const_spec((PEER_HEADS, D_MODEL, PEER_QDIM)), _const_spec((2, N_KEYS, PEER_HALF))],
        out_specs=[tok, pick, pick],
        out_shape=_route_out_shape(),
        scratch_shapes=_route_scratch(TM),
        compiler_params=pltpu.CompilerParams(vmem_limit_bytes=VMEM_LIMIT),
        name="peer_route",
    )(x, mod_l, mod_l, norm2.reshape(1, D_MODEL), wq, keys)


def _route_side_specs(third):
    off = third * (N_THIRD // TP)

    def mod(chunk):
        return pl.BlockSpec((None, 1, D_MODEL), lambda i: (_cond_row((off + i) * TP // TM), 0, chunk))

    in_specs = [pl.BlockSpec((TP, D_MODEL), lambda i: (off + i, 0)), mod(3), mod(4), _const_spec((1, D_MODEL)),
                pl.BlockSpec((PEER_HEADS, D_MODEL, PEER_QDIM), lambda i: (0, 0, 0), pipeline_mode=pl.Buffered(1)),
                _const_spec((2, N_KEYS, PEER_HALF))]
    pick = pl.BlockSpec((TP, N_PICK), lambda i: (i, 0))
    out_specs = [pl.BlockSpec((TP, D_MODEL), lambda i: (i, 0)), pick, pick]
    return in_specs, out_specs


def _route_side(in_refs, out_refs, scratch):
    x_ref, sh_ref, sc_ref, n2_ref, wq_ref, keys_ref = in_refs
    h2_ref, idx_ref, gate_ref = out_refs
    h2b_ref, it_ref, gt_ref = scratch
    h2 = _route_hidden(x_ref, sh_ref, sc_ref, n2_ref)
    h2_ref[...] = h2
    h2b_ref[...] = h2.astype(BF16)

    def side(h):
        rows = pl.ds(pl.multiple_of(h * PEER_TOPK, PEER_TOPK), PEER_TOPK)
        _route_head(h2b_ref[...], wq_ref[h], keys_ref, it_ref, gt_ref, rows)

    return side, functools.partial(_route_finish, it_ref, gt_ref, idx_ref, gate_ref)


def _pack_kernel(t_ref, o_ref):
    def rounded_bits(v):
        return lax.bitcast_convert_type(v.astype(BF16).astype(F32), jnp.int32)

    lo = lax.shift_right_logical(rounded_bits(t_ref[:, 0:ROW_WORDS]), 16)
    hi = rounded_bits(t_ref[:, ROW_WORDS:D_MODEL]) & jnp.int32(-65536)
    words = lo | hi
    for g in range(PACK_ROWS // 8):
        for s in range(ROW_SUB):
            o_ref[pl.ds(8 * g * ROW_SUB + s, 8, stride=ROW_SUB), :] = words[8 * g:8 * g + 8, s * 128:(s + 1) * 128]


def _pack_table(tabs, layer):
    return pl.pallas_call(
        _pack_kernel,
        grid=(N_EXPERTS // PACK_ROWS,),
        in_specs=[pl.BlockSpec((None, PACK_ROWS, D_MODEL), lambda i: (layer, i, 0))],
        out_specs=pl.BlockSpec((PACK_ROWS * ROW_SUB, 128), lambda i: (i, 0)),
        out_shape=jax.ShapeDtypeStruct((N_EXPERTS * ROW_SUB, 128), jnp.int32),
        name="pack_table",
    )(tabs)


def _unpack(words):
    lo = lax.bitcast_convert_type(words << 16, F32)
    hi = lax.bitcast_convert_type(words & jnp.int32(-65536), F32)
    return lo, hi


def _gather_planes(tab_ref, idx_ref, t, g_ref):
    for k in range(N_PICK):
        row = pl.multiple_of(idx_ref[t, k], ROW_SUB)
        g_ref[pl.ds(k, ROW_SUB, stride=PLANE_STRIDE), :] = tab_ref[pl.ds(row, ROW_SUB), :]


def _plane(g_ref, s):
    return g_ref[s * PLANE_STRIDE:s * PLANE_STRIDE + N_PICK, :]


def _split_bf16(v):
    hi = v.astype(BF16)
    return hi, (v - hi.astype(F32)).astype(BF16)


def _pipelined_tokens(tab_ref, idx_ref, bufs, compute, side, unroll=TOKEN_UNROLL):
    _gather_planes(tab_ref, idx_ref, 0, bufs[0])

    def group(j, carry):
        if side is not None:
            side(j)
        for p in range(unroll):
            t = unroll * j + p
            compute(t, pl.multiple_of(unroll * j + p // 8 * 8, 8), p % 8, bufs[p % 2])
            _gather_planes(tab_ref, idx_ref, jnp.minimum(t + 1, TP - 1), bufs[(p + 1) % 2])
        return carry

    lax.fori_loop(0, TP // unroll, group, 0)


def _split_refs(with_route, refs, n_in, n_scratch):
    n_rin, n_rout = (6, 3) if with_route else (0, 0)
    cuts = [n_in, n_rin, 1, n_rout, n_scratch]
    parts, at = [], 0
    for n in cuts:
        parts.append(refs[at:at + n])
        at += n
    parts.append(refs[at:])
    return parts


def _peer_act_kernel(with_route, idx_ref, x_ref, gate_ref, tab_ref, o_ref, ga_ref, gb_ref):
    assert not with_route
    ones = jnp.ones((8, 2 * 128), BF16)

    def token(t, t8, r, g_ref):
        def x_row(j):
            return x_ref[pl.ds(t8, 8), j * 128:(j + 1) * 128][r:r + 1]

        acc = jnp.zeros((N_PICK, 128), F32)
        for s in range(ROW_SUB):
            lo, hi = _unpack(_plane(g_ref, s))
            acc = acc + lo * x_row(s) + hi * x_row(ROW_SUB + s)
        act = _dot_nt(ones, jnp.concatenate(_split_bf16(acc), axis=1))
        o_ref[pl.ds(t, 1), :] = jax.nn.gelu(act[0:1]) * gate_ref[pl.ds(t, 1), :]

    _pipelined_tokens(tab_ref, idx_ref, (ga_ref, gb_ref), token, None, 2 * TOKEN_UNROLL)


def _peer_out_kernel(with_route, *refs):
    (idx_ref, coef_ref, tab_ref), rin, (o_ref,), rout, (ga_ref, gb_ref, st_ref), rscratch = _split_refs(
        with_route, refs, 3, 3)
    side, finish = _route_side(rin, rout, rscratch) if with_route else (None, None)

    def token(t, t8, r, g_ref):
        c_hi, c_lo = _split_bf16(coef_ref[t])
        lhs = jnp.concatenate([c_hi, c_lo, jnp.zeros((4, 2 * N_PICK), BF16)], axis=0)
        for s in range(ROW_SUB):
            w = pltpu.bitcast(_plane(g_ref, s), BF16)
            acc = jnp.dot(lhs, w, preferred_element_type=F32)
            st_ref[8 * s + r:8 * s + r + 1, :] = acc[0:1] + acc[2:3]
            st_ref[8 * (ROW_SUB + s) + r:8 * (ROW_SUB + s) + r + 1, :] = acc[1:2] + acc[3:4]
        if r == 7:
            for j in range(D_MODEL // 128):
                o_ref[pl.ds(t8, 8), j * 128:(j + 1) * 128] = st_ref[8 * j:8 * j + 8, :]

    _pipelined_tokens(tab_ref, idx_ref, (ga_ref, gb_ref), token, side)
    if with_route:
        finish()


def _expert_call(body, name, in_specs, args, out_spec, out_width, scratch, side):
    out_specs, out_shape = [out_spec], [jax.ShapeDtypeStruct((N_THIRD, out_width), F32)]
    if side is not None:
        assert TP // TOKEN_UNROLL == PEER_HEADS
        third, x, mod_l, norm2, wq, keys = side
        rin, rout = _route_side_specs(third)
        in_specs = in_specs + rin
        args = args + (x, mod_l, mod_l, norm2.reshape(1, D_MODEL), wq, keys)
        out_specs = out_specs + rout
        out_shape = out_shape + _route_out_shape()
        scratch = scratch + [pltpu.VMEM((TP, D_MODEL), BF16)] + _route_scratch(TP)
    outs = pl.pallas_call(
        functools.partial(body, side is not None),
        grid=(N_THIRD // TP,),
        in_specs=in_specs,
        out_specs=out_specs,
        out_shape=out_shape,
        scratch_shapes=scratch,
        compiler_params=pltpu.CompilerParams(vmem_limit_bytes=VMEM_LIMIT),
        name=name,
    )(*args)
    return outs[0], tuple(outs[1:])


_SMEM_IDX = pl.BlockSpec((TP, N_PICK), lambda i: (i, 0), memory_space=pltpu.SMEM)
_TABLE = pl.BlockSpec((N_EXPERTS * ROW_SUB, 128), lambda i: (0, 0), pipeline_mode=pl.Buffered(1))
_PLANES = pltpu.VMEM((ROW_SUB * PLANE_STRIDE, 128), jnp.int32)


def _peer_act(idx, h2, gate, u_words):
    rows = pl.BlockSpec((TP, D_MODEL), lambda i: (i, 0))
    pick = pl.BlockSpec((TP, N_PICK), lambda i: (i, 0))
    coef, _ = _expert_call(_peer_act_kernel, "peer_act", [_SMEM_IDX, rows, pick, _TABLE],
                           (idx, h2, gate, u_words), pick, N_PICK, [_PLANES, _PLANES], None)
    zero = jnp.zeros_like(coef)
    coef = jnp.stack([jnp.stack([coef, zero], axis=-1), jnp.stack([zero, coef], axis=-1)], axis=1)
    return coef.reshape(N_THIRD, 2, 2 * N_PICK)


def _peer_out(idx, coef, v_words, side=None):
    rows = pl.BlockSpec((TP, D_MODEL), lambda i: (i, 0))
    pair = pl.BlockSpec((TP, 2, 2 * N_PICK), lambda i: (i, 0, 0))
    stage = pltpu.VMEM((8 * (D_MODEL // 128), 128), F32)
    return _expert_call(_peer_out_kernel, "peer_out", [_SMEM_IDX, pair, _TABLE], (idx, coef, v_words),
                        rows, D_MODEL, [_PLANES, _PLANES, stage], side)


def _peer(x, mod_l, norm2, wq, keys, u_words, v_words):
    route_args = (x, mod_l, norm2, wq, keys)
    h2, idx, gate = _route_first(*route_args)
    outs = []
    for third in range(3):
        coef = _peer_act(idx, h2, gate, u_words)
        side = (third + 1,) + route_args if third < 2 else None
        p, routed = _peer_out(idx, coef, v_words, side)
        outs.append(p)
        if side is not None:
            h2, idx, gate = routed
    return tuple(outs)


def _final_kernel(x_ref, pa_ref, pb_ref, pc_ref, g2_ref, n_ref, o_ref):
    o_ref[...] = _rms(x_ref[...] + g2_ref[...] * _pick_third((pa_ref, pb_ref, pc_ref)), n_ref[...])


def _final(x, p, mod_l, final_norm):
    tok = pl.BlockSpec((TM, D_MODEL), lambda i: (i, 0))
    return pl.pallas_call(
        _final_kernel,
        grid=(N_TILES,),
        in_specs=[tok] + _third_specs(D_MODEL) + [_mod_spec(5), _const_spec((1, D_MODEL))],
        out_specs=tok,
        out_shape=jax.ShapeDtypeStruct((N_TOK, D_MODEL), F32),
        name="final_norm",
    )(x, *p, mod_l, final_norm.reshape(1, D_MODEL))


def _rope_tables():
    rows = DEC_SEQ // GRID_W
    row_ids = jnp.repeat(jnp.arange(rows), GRID_W).astype(F32)
    col_ids = jnp.tile(jnp.arange(GRID_W), rows).astype(F32)
    inv_freq = ROPE_BASE ** (-jnp.arange(ROPE_FREQS, dtype=F32) / ROPE_FREQS)
    ang_r = row_ids[:, None] * inv_freq
    ang_c = col_ids[:, None] * inv_freq
    cos = jnp.concatenate([jnp.cos(ang_r), jnp.cos(ang_r), jnp.cos(ang_c), jnp.cos(ang_c)], axis=1)
    sin = jnp.concatenate([-jnp.sin(ang_r), jnp.sin(ang_r), -jnp.sin(ang_c), jnp.sin(ang_c)], axis=1)
    reps = ATT_WIDTH // ATT_HD
    cos = jnp.concatenate([jnp.ones((TM, ATT_WIDTH), F32), jnp.tile(cos, (1, reps))], axis=0)
    sin = jnp.concatenate([jnp.zeros((TM, ATT_WIDTH), F32), jnp.tile(sin, (1, reps))], axis=0)
    return cos, sin


def kernel(x_prompt, x_sample, cache_k, cache_v, state_lru, c, c_ctx, w_mod, b_mod, norm1, norm2, w_in, a_norm, a_ws, a_bs, lru_conv_w, lru_conv_b, lru_w_r, lru_b_r, lru_w_i, lru_b_i, lru_lam, att_lam, att_subln, w_up_a, w_up_b, w_up_c, w_out, peer_wq, peer_keys, peer_u, peer_v, final_norm):
    x = jnp.concatenate([x_prompt.reshape(N_CTX_TOK, D_MODEL), x_sample.reshape(N_LAT_TOK, D_MODEL)], axis=0)
    cond = jnp.concatenate([c_ctx[None, :], c, jnp.zeros((N_COND - 1 - DEC_BATCH, D_MODEL), F32)], axis=0)
    mod = _modulation(cond, w_mod, b_mod).reshape(DEPTH, N_COND, 1, N_MOD * D_MODEL)
    cos_t, sin_t = _rope_tables()
    zero_h0 = jnp.zeros((BATCH, 2, LRU_WIDTH), F32)
    ks, vs, hs = [], [], []
    res = None
    for i in range(DEPTH):
        lam_init = 0.8 - 0.6 * math.exp(-0.3 * i)
        outs = _inproj(x, res, mod[i], norm1[i], w_in[i].astype(BF16), cos_t, sin_t)
        au, av, rx, rg, q, k, kr, v, br = outs[:9]
        if res is not None:
            x = outs[9]
        w_r = _blockdiag128(lru_w_r[i])
        w_i = _blockdiag128(lru_w_i[i])
        lru_args = (lru_conv_w[i], lru_conv_b[i], w_r, lru_b_r[i], w_i, lru_b_i[i], lru_lam[i])
        hs_ctx, hl_ctx = _lru(rx, 0, BATCH, SEQ, *lru_args, zero_h0)
        hs_lat, _ = _lru(rx, N_CTX_TOK, DEC_BATCH, DEC_SEQ, *lru_args, state_lru[:, i])
        o_ctx = _attention(q, kr, v, 0, BATCH, SEQ, None, att_lam[i], att_subln[i], lam_init)
        ctx = (cache_k[:, i].reshape(DEC_BATCH, PAST_LEN, ATT_WIDTH), cache_v[:, i].reshape(DEC_BATCH, PAST_LEN, ATT_WIDTH))
        o_lat = _attention(q, kr, v, N_CTX_TOK, DEC_BATCH, DEC_SEQ, ctx, att_lam[i], att_subln[i], lam_init)
        bias = jnp.repeat(a_bs[i].T, CHUNK, axis=1)
        x = _merge(x, au, av, hs_ctx, hs_lat, rg, o_ctx, o_lat, br, mod[i], a_norm[i], a_ws[i], bias, w_up_a[i].astype(BF16), w_up_b[i].astype(BF16),
                   w_up_c[i].astype(BF16), w_out[i].astype(BF16))
        wq = peer_wq[i].astype(BF16).reshape(D_MODEL, PEER_HEADS, PEER_QDIM).transpose(1, 0, 2)
        p = _peer(x, mod[i], norm2[i], wq, peer_keys[i], _pack_table(peer_u, i), _pack_table(peer_v, i))
        res = (p, mod[i])
        ks.append(k[:N_CTX_TOK].reshape(BATCH, SEQ, ATT_HEADS, ATT_VD))
        vs.append(v[:N_CTX_TOK].reshape(BATCH, SEQ, ATT_HEADS, ATT_VD))
        hs.append(hl_ctx)
    y = _final(x, res[0], res[1], final_norm)
    return (y[:N_CTX_TOK].reshape(BATCH, SEQ, D_MODEL), y[N_CTX_TOK:].reshape(DEC_BATCH, DEC_SEQ, D_MODEL),
            jnp.stack(ks, axis=1), jnp.stack(vs, axis=1), jnp.stack(hs, axis=1))
```
